```python
import math
import jax, jax.numpy as jnp
from jax import lax
import numpy as np

D_MODEL = 1024
BATCH = 2
SEQ = 8192
DEPTH = 1

ATTN_HEADS = 8
ATTN_HEAD_DIM = 64
ATTN_V_DIM = 2 * ATTN_HEAD_DIM
ATTN_QK_WIDTH = ATTN_HEADS * 2 * ATTN_HEAD_DIM
ATTN_V_WIDTH = ATTN_HEADS * ATTN_V_DIM
ROPE_THETA = 10000.0
Q_BLOCK = 128
SSD_D_INNER = 2 * D_MODEL
SSD_HEAD_DIM = 64
SSD_HEADS = SSD_D_INNER // SSD_HEAD_DIM
SSD_GROUPS = 4
SSD_HEADS_PER_GROUP = SSD_HEADS // SSD_GROUPS
SSD_D_STATE = 128
SSD_CONV = 4
SSD_CHUNK = 128
SSD_CONV_DIM = SSD_D_INNER + 2 * SSD_GROUPS * SSD_D_STATE
IN_SIZES = (ATTN_QK_WIDTH, ATTN_QK_WIDTH, ATTN_V_WIDTH, D_MODEL, D_MODEL, SSD_D_INNER, SSD_CONV_DIM, SSD_HEADS)
IN_WIDTH = ATTN_QK_WIDTH * 2 + ATTN_V_WIDTH + 2 * D_MODEL + SSD_D_INNER + SSD_CONV_DIM + SSD_HEADS
N_EXPERTS = 64
TOP_K = 8
N_EXPERT_GROUPS = 8
TOPK_GROUPS = 4
EXPERT_DIM = 256
SHARED_DIM = 256
ROUTED_SCALE = 2.5
MOE_TOKEN_BLOCK = 128
EPS = 1e-6

kernel_name = "hybrid_diffattn_ssd_moe_adaln"


def rmsnorm(x, g):
    x32 = x.astype(jnp.float32)
    r = x32 * lax.rsqrt(jnp.mean(x32 * x32, axis=-1, keepdims=True) + EPS)
    return (r * g.astype(jnp.float32)).astype(x.dtype)


def rope(x, positions):
    d = x.shape[-1]
    half = d // 2
    inv_freq = ROPE_THETA ** (-jnp.arange(half, dtype=jnp.float32) * 2.0 / d)
    ang = positions.astype(jnp.float32)[..., None] * inv_freq
    ang = ang.reshape(ang.shape[:2] + (1,) * (x.ndim - 3) + (half,))
    cos, sin = jnp.cos(ang), jnp.sin(ang)
    x32 = x.astype(jnp.float32)
    x1, x2 = x32[..., :half], x32[..., half:]
    return jnp.concatenate([x1 * cos - x2 * sin, x2 * cos + x1 * sin], axis=-1).astype(x.dtype)


def diff_attention(q, k, v, positions, qn_g, kn_g, lam_vecs, subln_g, lambda_init):
    b, s, h, _, d = q.shape
    q = rope(rmsnorm(q, qn_g), positions) * (d ** -0.5)
    k = rope(rmsnorm(k, kn_g), positions)
    lv = lam_vecs.astype(jnp.float32)
    lam = jnp.exp(jnp.sum(lv[0] * lv[1])) - jnp.exp(jnp.sum(lv[2] * lv[3])) + lambda_init
    nblk = s // Q_BLOCK
    qb = q.reshape(b, nblk, Q_BLOCK, h, 2, d).transpose(1, 0, 2, 3, 4, 5)
    key_idx = jnp.arange(s)

    def block(args):
        q_blk, i = args
        scores = jnp.einsum('bqhmd,bkhmd->bhmqk', q_blk, k).astype(jnp.float32)
        q_idx = i * Q_BLOCK + jnp.arange(Q_BLOCK)
        mask = key_idx[None, :] <= q_idx[:, None]
        p = jax.nn.softmax(jnp.where(mask, scores, -jnp.inf), axis=-1)
        w = p[:, :, 0] - lam * p[:, :, 1]
        return jnp.einsum('bhqk,bkhe->bqhe', w.astype(v.dtype), v)

    o = lax.map(block, (qb, jnp.arange(nblk)))
    o = o.transpose(1, 0, 2, 3, 4).reshape(b, s, h, ATTN_V_DIM)
    o = rmsnorm(o, subln_g) * (1.0 - lambda_init)
    return o.reshape(b, s, h * ATTN_V_DIM)


def causal_depthwise_conv(u, w, bias):
    ch = u.shape[-1]
    out = lax.conv_general_dilated(u, w[:, None, :].astype(u.dtype), window_strides=(1,),
                                   padding=[(SSD_CONV - 1, 0)],
                                   dimension_numbers=('NWC', 'WIO', 'NWC'),
                                   feature_group_count=ch)
    return out + bias


def segsum_exp(a_cum):
    l = a_cum.shape[-1]
    diff = a_cum[..., :, None] - a_cum[..., None, :]
    mask = jnp.tril(jnp.ones((l, l), dtype=bool))
    return jnp.exp(jnp.where(mask, diff, -jnp.inf))


def ssd_chunked(x, a, bm, cm):
    b, s, g, j, p = x.shape
    n = bm.shape[-1]
    c, l = s // SSD_CHUNK, SSD_CHUNK
    x = x.reshape(b, c, l, g, j, p)
    bm = bm.reshape(b, c, l, g, n)
    cm = cm.reshape(b, c, l, g, n)
    a = a.reshape(b, c, l, g, j).transpose(0, 3, 4, 1, 2).astype(jnp.float32)
    a_cum = jnp.cumsum(a, axis=-1)
    lmat = segsum_exp(a_cum)
    cb = jnp.einsum('bclgn,bcsgn->bgcls', cm, bm)
    y_diag = jnp.einsum('bgcls,bgjcls,bcsgjp->bclgjp', cb, lmat, x)
    decay_states = jnp.exp(a_cum[..., -1:] - a_cum)
    states = jnp.einsum('bclgn,bgjcl,bclgjp->bcgjpn', bm, decay_states, x)
    chunk_decay = jnp.exp(a_cum[..., -1])

    def step(hstate, inp):
        st, dec = inp
        return hstate * dec[..., None, None] + st, hstate

    h0 = jnp.zeros((b, g, j, p, n), jnp.float32)
    _, prev = lax.scan(step, h0, (states.transpose(1, 0, 2, 3, 4, 5).astype(jnp.float32),
                                  chunk_decay.transpose(3, 0, 1, 2)))
    y_off = jnp.einsum('bclgn,cbgjpn,bgjcl->bclgjp', cm, prev, jnp.exp(a_cum))
    return (y_diag + y_off).reshape(b, s, g, j, p)


def ssd_mixer(z, xbc, dt, conv_w, conv_b, dt_bias, a_log, d_skip, norm_g):
    b, s, _ = z.shape
    xbc = jax.nn.silu(causal_depthwise_conv(xbc, conv_w, conv_b))
    gn = SSD_GROUPS * SSD_D_STATE
    xs = xbc[..., :SSD_D_INNER].reshape(b, s, SSD_GROUPS, SSD_HEADS_PER_GROUP, SSD_HEAD_DIM)
    bm = xbc[..., SSD_D_INNER:SSD_D_INNER + gn].reshape(b, s, SSD_GROUPS, SSD_D_STATE)
    cm = xbc[..., SSD_D_INNER + gn:].reshape(b, s, SSD_GROUPS, SSD_D_STATE)
    dtp = jax.nn.softplus(dt.astype(jnp.float32) + dt_bias.astype(jnp.float32))
    dtp = dtp.reshape(b, s, SSD_GROUPS, SSD_HEADS_PER_GROUP)
    a_neg = -jnp.exp(a_log.astype(jnp.float32)).reshape(SSD_GROUPS, SSD_HEADS_PER_GROUP)
    y = ssd_chunked(xs * dtp[..., None], dtp * a_neg, bm, cm)
    y = y + xs * d_skip.reshape(SSD_GROUPS, SSD_HEADS_PER_GROUP)[..., None]
    y = y.reshape(b, s, SSD_D_INNER).astype(z.dtype) * jax.nn.silu(z)
    y = rmsnorm(y.reshape(b, s, SSD_GROUPS, SSD_D_INNER // SSD_GROUPS),
                norm_g.reshape(SSD_GROUPS, SSD_D_INNER // SSD_GROUPS))
    return y.reshape(b, s, SSD_D_INNER)


def moe_ffn(h, w_router, router_bias, w_eg, w_eu, w_ed, w_sg, w_su, w_sd):
    b, s, d = h.shape
    t = h.reshape(b * s, d)
    ntok = b * s
    scores = jax.nn.sigmoid((t @ w_router).astype(jnp.float32))
    choice = scores + router_bias.astype(jnp.float32)
    per_group = N_EXPERTS // N_EXPERT_GROUPS
    grp_score = lax.top_k(choice.reshape(ntok, N_EXPERT_GROUPS, per_group), 2)[0].sum(-1)
    _, top_grp = lax.top_k(grp_score, TOPK_GROUPS)
    grp_mask = jax.nn.one_hot(top_grp, N_EXPERT_GROUPS, dtype=jnp.float32).sum(-2)
    expert_mask = jnp.repeat(grp_mask, per_group, axis=-1) > 0
    _, idx = lax.top_k(jnp.where(expert_mask, choice, -jnp.inf), TOP_K)
    w = jnp.take_along_axis(scores, idx, axis=-1)
    w = w / jnp.sum(w, axis=-1, keepdims=True) * ROUTED_SCALE
    gates = jnp.einsum('tk,tke->te', w, jax.nn.one_hot(idx, N_EXPERTS, dtype=jnp.float32))
    nb = ntok // MOE_TOKEN_BLOCK

    def block(args):
        tb, gb = args
        g = jnp.einsum('td,edf->tef', tb, w_eg)
        u = jnp.einsum('td,edf->tef', tb, w_eu)
        act = jax.nn.silu(g) * u * gb[:, :, None].astype(tb.dtype)
        return jnp.einsum('tef,efd->td', act, w_ed)

    routed = lax.map(block, (t.reshape(nb, MOE_TOKEN_BLOCK, d),
                             gates.reshape(nb, MOE_TOKEN_BLOCK, N_EXPERTS))).reshape(ntok, d)
    shared = (jax.nn.silu(t @ w_sg) * (t @ w_su)) @ w_sd
    return (routed + shared).reshape(b, s, d)


def setup_inputs(seed: int = 0) -> dict:
    key = jax.random.key(seed)
    ks = jax.random.split(key, 32)
    f32 = jnp.float32
    L, D = DEPTH, D_MODEL

    def nrm(k, shape, scale):
        return jax.random.normal(k, shape, f32) * scale

    x = jax.random.normal(ks[0], (BATCH, SEQ, D), f32)
    c = jax.random.normal(ks[1], (BATCH, D), f32)
    offs = jax.random.randint(ks[2], (BATCH, 1), 0, 1024, dtype=jnp.int32)
    positions = offs + jnp.arange(SEQ, dtype=jnp.int32)[None, :]
    dt0 = jnp.exp(jax.random.uniform(ks[12], (L, SSD_HEADS), f32, math.log(1e-3), math.log(1e-1)))
    return {
        "x": x,
        "c": c,
        "positions": positions,
        "w_ada": nrm(ks[3], (L, D, 6 * D), 0.5 * D ** -0.5),
        "b_ada": nrm(ks[4], (L, 6 * D), 0.02),
        "norm1_g": 1.0 + nrm(ks[5], (L, D), 0.1),
        "w_in": nrm(ks[6], (L, D, IN_WIDTH), D ** -0.5),
        "q_norm_g": 1.0 + nrm(ks[7], (L, ATTN_HEAD_DIM), 0.1),
        "k_norm_g": 1.0 + nrm(ks[8], (L, ATTN_HEAD_DIM), 0.1),
        "lambda_qk": nrm(ks[9], (L, 4, ATTN_HEAD_DIM), 0.1),
        "subln_g": 1.0 + nrm(ks[10], (L, ATTN_V_DIM), 0.1),
        "conv_w": nrm(ks[11], (L, SSD_CONV, SSD_CONV_DIM), 0.5),
        "conv_b": nrm(ks[13], (L, SSD_CONV_DIM), 0.02),
        "dt_bias": dt0 + jnp.log(-jnp.expm1(-dt0)),
        "a_log": jnp.log(jax.random.uniform(ks[14], (L, SSD_HEADS), f32, 1.0, 16.0)),
        "d_skip": 1.0 + nrm(ks[15], (L, SSD_HEADS), 0.1),
        "ssd_norm_g": 1.0 + nrm(ks[16], (L, SSD_D_INNER), 0.1),
        "w_branch_a": nrm(ks[17], (L, ATTN_V_WIDTH, D), ATTN_V_WIDTH ** -0.5),
        "w_branch_b": nrm(ks[18], (L, SSD_D_INNER, D), SSD_D_INNER ** -0.5),
        "w_out": nrm(ks[19], (L, D, D), D ** -0.5),
        "norm2_g": 1.0 + nrm(ks[20], (L, D), 0.1),
        "w_router": nrm(ks[21], (L, D, N_EXPERTS), D ** -0.5),
        "router_bias": nrm(ks[22], (L, N_EXPERTS), 0.01),
        "w_exp_gate": nrm(ks[23], (L, N_EXPERTS, D, EXPERT_DIM), D ** -0.5),
        "w_exp_up": nrm(ks[24], (L, N_EXPERTS, D, EXPERT_DIM), D ** -0.5),
        "w_exp_down": nrm(ks[25], (L, N_EXPERTS, EXPERT_DIM, D), EXPERT_DIM ** -0.5),
        "w_sh_gate": nrm(ks[26], (L, D, SHARED_DIM), D ** -0.5),
        "w_sh_up": nrm(ks[27], (L, D, SHARED_DIM), D ** -0.5),
        "w_sh_down": nrm(ks[28], (L, SHARED_DIM, D), SHARED_DIM ** -0.5),
    }


def reference(x, c, positions, w_ada, b_ada, norm1_g, w_in, q_norm_g, k_norm_g, lambda_qk, subln_g,
              conv_w, conv_b, dt_bias, a_log, d_skip, ssd_norm_g, w_branch_a, w_branch_b, w_out,
              norm2_g, w_router, router_bias, w_exp_gate, w_exp_up, w_exp_down,
              w_sh_gate, w_sh_up, w_sh_down):
    b, s, _ = x.shape
    offsets = [int(o) for o in np.cumsum(IN_SIZES)[:-1]]
    for layer in range(DEPTH):
        lambda_init = 0.8 - 0.6 * math.exp(-0.3 * layer)
        mod = jax.nn.silu(c) @ w_ada[layer] + b_ada[layer]
        sh1, sc1, g1, sh2, sc2, g2 = jnp.split(mod[:, None, :], 6, axis=-1)
        h = rmsnorm(x, norm1_g[layer]) * (1.0 + sc1) + sh1
        proj = h @ w_in[layer]
        q, k, v, ga, gb, z, xbc, dt = jnp.split(proj, offsets, axis=-1)
        q = q.reshape(b, s, ATTN_HEADS, 2, ATTN_HEAD_DIM)
        k = k.reshape(b, s, ATTN_HEADS, 2, ATTN_HEAD_DIM)
        v = v.reshape(b, s, ATTN_HEADS, ATTN_V_DIM)
        ya = diff_attention(q, k, v, positions, q_norm_g[layer], k_norm_g[layer],
                            lambda_qk[layer], subln_g[layer], lambda_init)
        yb = ssd_mixer(z, xbc, dt, conv_w[layer], conv_b[layer], dt_bias[layer],
                       a_log[layer], d_skip[layer], ssd_norm_g[layer])
        merged = (jax.nn.sigmoid(ga) * (ya @ w_branch_a[layer])
                  + jax.nn.sigmoid(gb) * (yb @ w_branch_b[layer]))
        x = x + g1 * (merged @ w_out[layer])
        h2 = rmsnorm(x, norm2_g[layer]) * (1.0 + sc2) + sh2
        x = x + g2 * moe_ffn(h2, w_router[layer], router_bias[layer], w_exp_gate[layer],
                             w_exp_up[layer], w_exp_down[layer], w_sh_gate[layer],
                             w_sh_up[layer], w_sh_down[layer])
    return x
```

```python
import functools
import math

import jax
import jax.numpy as jnp
from jax import lax
from jax.experimental import pallas as pl
from jax.experimental.pallas import tpu as pltpu

F32 = jnp.float32
BF16 = jnp.bfloat16

D_MODEL = 1024
ATTN_HEADS = 8
ATTN_HEAD_DIM = 64
ATTN_V_DIM = 128
ROPE_THETA = 10000.0
SSD_D_INNER = 2048
SSD_HEAD_DIM = 64
SSD_HEADS = 32
SSD_GROUPS = 4
SSD_D_STATE = 128
SSD_CONV = 4
SSD_CHUNK = 128
SSD_CONV_DIM = 3072
N_EXPERTS = 64
TOP_K = 8
N_EXPERT_GROUPS = 8
TOPK_GROUPS = 4
EXPERT_DIM = 256
ROUTED_SCALE = 2.5
EPS = 1e-6
NEG = -1e30

LANES = 128
MAIN_COLS = 10240
VMEM_LIMIT = 56 * 1024 * 1024


def _dot(a, b):
    return jnp.dot(a, b, preferred_element_type=F32)


def _sigmoid(x):
    return 1.0 / (1.0 + jnp.exp(-x))


def _split2(a):
    hi = a.astype(BF16)
    lo = (a - hi.astype(F32)).astype(BF16)
    return hi, lo


def _split3(a):
    hi = a.astype(BF16)
    r = a - hi.astype(F32)
    mid = r.astype(BF16)
    lo = (r - mid.astype(F32)).astype(BF16)
    return hi, mid, lo


def _params(sem, vmem=VMEM_LIMIT):
    return pltpu.CompilerParams(dimension_semantics=sem, vmem_limit_bytes=vmem)


def _ada_body(c_ref, w_ref, b_ref, o_ref):
    c = c_ref[...]
    sc = c * _sigmoid(c)
    chi, clo = _split2(sc)
    whi, wlo = _split2(w_ref[...])
    o_ref[...] = _dot(chi, whi) + _dot(chi, wlo) + _dot(clo, whi) + b_ref[...]


def _ada(c_pad, w, b):
    d, n = w.shape
    tn = 1024
    return pl.pallas_call(
        _ada_body,
        grid=(n // tn,),
        in_specs=[pl.BlockSpec((8, d), lambda j: (0, 0)),
                  pl.BlockSpec((d, tn), lambda j: (0, j)),
                  pl.BlockSpec((1, tn), lambda j: (0, j))],
        out_specs=pl.BlockSpec((8, tn), lambda j: (0, j)),
        out_shape=jax.ShapeDtypeStruct((8, n), F32),
        compiler_params=_params(("arbitrary",)),
        name="ada",
    )(c_pad, w, b.reshape(1, n))


def _inproj_body(x_ref, g_ref, sc_ref, sh_ref, w_ref, wdh_ref, wdl_ref, o_ref, dt_ref, h_scr):
    @pl.when(pl.program_id(1) == 0)
    def _():
        x = x_ref[...]
        ms = jnp.mean(x * x, axis=-1, keepdims=True)
        h = x * lax.rsqrt(ms + EPS) * g_ref[...]
        h = h * (1.0 + sc_ref[0]) + sh_ref[0]
        hb, hl = _split2(h)
        h_scr[...] = hb
        dt_ref[...] = _dot(hb, wdh_ref[...]) + _dot(hb, wdl_ref[...]) + _dot(hl, wdh_ref[...])

    o_ref[...] = _dot(h_scr[...], w_ref[...]).astype(BF16)


def _inproj(x2d, g, sc, sh, w_main, wdt_hi, wdt_lo, seq):
    t, d = x2d.shape
    tm = min(1024, seq)
    tn = 1024
    per_b = seq // tm
    return pl.pallas_call(
        _inproj_body,
        grid=(t // tm, MAIN_COLS // tn),
        in_specs=[pl.BlockSpec((tm, d), lambda i, j: (i, 0)),
                  pl.BlockSpec((1, d), lambda i, j: (0, 0)),
                  pl.BlockSpec((1, 1, d), lambda i, j: (i // per_b, 0, 0)),
                  pl.BlockSpec((1, 1, d), lambda i, j: (i // per_b, 0, 0)),
                  pl.BlockSpec((d, tn), lambda i, j: (0, j)),
                  pl.BlockSpec((d, LANES), lambda i, j: (0, 0)),
                  pl.BlockSpec((d, LANES), lambda i, j: (0, 0))],
        out_specs=[pl.BlockSpec((tm, tn), lambda i, j: (i, j)),
                   pl.BlockSpec((tm, LANES), lambda i, j: (i, 0))],
        out_shape=[jax.ShapeDtypeStruct((t, MAIN_COLS), BF16),
                   jax.ShapeDtypeStruct((t, LANES), F32)],
        scratch_shapes=[pltpu.VMEM((tm, d), BF16)],
        compiler_params=_params(("arbitrary", "arbitrary")),
        name="inproj",
    )(x2d, g.reshape(1, d), sc, sh, w_main, wdt_hi, wdt_lo)


def _qkvprep_body(q_ref, k_ref, v_ref, pos_ref, invf_ref, gq_ref, gk_ref, qt_ref, ko_ref, vt_ref, *, tk):
    ts = q_ref.shape[0]
    half = ATTN_HEAD_DIM // 2
    ang = pos_ref[0].astype(F32) * invf_ref[...]
    cos = jnp.cos(ang)
    sin = jnp.sin(ang)
    zeros = jnp.zeros((ATTN_HEAD_DIM, ts), F32)

    def norm_rope(blk, g):
        ms = jnp.mean(blk * blk, axis=0, keepdims=True)
        r = blk * lax.rsqrt(ms + EPS) * g
        x1 = r[:half]
        x2 = r[half:]
        return jnp.concatenate([x1 * cos - x2 * sin, x2 * cos + x1 * sin], axis=0)

    for h in range(ATTN_HEADS):
        sl = slice(h * LANES, (h + 1) * LANES)
        qht = q_ref[:, sl].astype(F32).T
        q1 = norm_rope(qht[:ATTN_HEAD_DIM], gq_ref[...]) * (ATTN_HEAD_DIM ** -0.5)
        q2 = norm_rope(qht[ATTN_HEAD_DIM:], gq_ref[...]) * (ATTN_HEAD_DIM ** -0.5)
        qt_ref[0, 2 * h] = jnp.concatenate([q1, zeros], axis=0).astype(BF16)
        qt_ref[0, 2 * h + 1] = jnp.concatenate([zeros, q2], axis=0).astype(BF16)
        kht = k_ref[:, sl].astype(F32).T
        k1 = norm_rope(kht[:ATTN_HEAD_DIM], gk_ref[...])
        k2 = norm_rope(kht[ATTN_HEAD_DIM:], gk_ref[...])
        ko_ref[0, h] = jnp.concatenate([k1, k2], axis=0).T.astype(BF16)
        vht = v_ref[:, sl].astype(F32).T.astype(BF16)
        for cc in range(ts // tk):
            vt_ref[0, h, cc] = vht[:, cc * tk:(cc + 1) * tk]


def _qkvprep(proj, positions, q_norm_g, k_norm_g, batch, seq, tk):
    ts = min(512, seq)
    ns = seq // ts
    half = ATTN_HEAD_DIM // 2
    inv_freq = ROPE_THETA ** (-jnp.arange(half, dtype=F32) * 2.0 / ATTN_HEAD_DIM)
    invf_b = jnp.broadcast_to(inv_freq[:, None], (half, ts))
    gq_b = jnp.broadcast_to(q_norm_g.astype(F32)[:, None], (ATTN_HEAD_DIM, ts))
    gk_b = jnp.broadcast_to(k_norm_g.astype(F32)[:, None], (ATTN_HEAD_DIM, ts))
    pos3 = positions.reshape(batch, 1, seq)
    const = lambda b, i: (0, 0)
    return pl.pallas_call(
        functools.partial(_qkvprep_body, tk=tk),
        grid=(batch, ns),
        in_specs=[pl.BlockSpec((ts, D_MODEL), lambda b, i: (b * ns + i, 0)),
                  pl.BlockSpec((ts, D_MODEL), lambda b, i: (b * ns + i, 1)),
                  pl.BlockSpec((ts, D_MODEL), lambda b, i: (b * ns + i, 2)),
                  pl.BlockSpec((1, 1, ts), lambda b, i: (b, 0, i)),
                  pl.BlockSpec((half, ts), const),
                  pl.BlockSpec((ATTN_HEAD_DIM, ts), const),
                  pl.BlockSpec((ATTN_HEAD_DIM, ts), const)],
        out_specs=[pl.BlockSpec((1, 2 * ATTN_HEADS, LANES, ts), lambda b, i: (b, 0, 0, i)),
                   pl.BlockSpec((1, ATTN_HEADS, ts, LANES), lambda b, i: (b, 0, i, 0)),
                   pl.BlockSpec((1, ATTN_HEADS, ts // tk, LANES, tk), lambda b, i: (b, 0, i, 0, 0))],
        out_shape=[jax.ShapeDtypeStruct((batch, 2 * ATTN_HEADS, LANES, seq), BF16),
                   jax.ShapeDtypeStruct((batch, ATTN_HEADS, seq, LANES), BF16),
                   jax.ShapeDtypeStruct((batch, ATTN_HEADS, seq // tk, LANES, tk), BF16)],
        compiler_params=_params(("arbitrary", "arbitrary")),
        name="qkvprep",
    )(proj, proj, proj, pos3, invf_b, gq_b, gk_b)


def _attn_body(lam_ref, sg_ref, qt_ref, k_ref, vt_ref, o_ref, acc_scr, *, tq, tk, lambda_init):
    i = pl.program_id(2)
    r = tq // tk
    q_maps = (qt_ref[0, 0], qt_ref[0, 1])
    acc_scr[...] = jnp.zeros(acc_scr.shape, F32)

    def step(j, carry, masked):
        kb = k_ref[0, 0, pl.ds(pl.multiple_of(j * tk, tk), tk), :]
        vb = vt_ref[0, 0, j]
        if masked:
            kidx = j * tk + lax.broadcasted_iota(jnp.int32, (tk, tq), 0)
            qidx = i * tq + lax.broadcasted_iota(jnp.int32, (tk, tq), 1)
            keep = kidx <= qidx
        out = []
        for mi in range(2):
            m_old, l_old = carry[2 * mi], carry[2 * mi + 1]
            s = _dot(kb, q_maps[mi])
            if masked:
                s = jnp.where(keep, s, NEG)
            m_new = jnp.maximum(m_old, jnp.max(s, axis=0, keepdims=True))
            alpha = jnp.exp(m_old - m_new)
            p = jnp.exp(s - m_new)
            l_new = alpha * l_old + jnp.sum(p, axis=0, keepdims=True)
            acc_scr[mi] = alpha * acc_scr[mi] + _dot(vb, p.astype(BF16))
            out += [m_new, l_new]
        return tuple(out)

    init = (jnp.full((1, tq), NEG, F32), jnp.zeros((1, tq), F32)) * 2
    carry = lax.fori_loop(0, i * r, lambda j, cr: step(j, cr, False), init)
    for d in range(r):
        carry = step(i * r + d, carry, True)

    lv = lam_ref[...]
    lam = (jnp.exp(jnp.sum(lv[0:1] * lv[1:2], keepdims=True))
           - jnp.exp(jnp.sum(lv[2:3] * lv[3:4], keepdims=True)) + lambda_init)
    o = acc_scr[0] / carry[1] - lam * (acc_scr[1] / carry[3])
    ms = jnp.mean(o * o, axis=0, keepdims=True)
    o = o * lax.rsqrt(ms + EPS) * sg_ref[...] * (1.0 - lambda_init)
    o_ref[...] = o.T.astype(BF16)


def _attention(qt, kk, vt, lambda_qk, subln_g, lambda_init, batch, seq, tq, tk):
    nq = seq // tq
    sg_b = jnp.broadcast_to(subln_g.astype(F32)[:, None], (ATTN_V_DIM, tq))
    return pl.pallas_call(
        functools.partial(_attn_body, tq=tq, tk=tk, lambda_init=lambda_init),
        grid=(batch, ATTN_HEADS, nq),
        in_specs=[pl.BlockSpec((4, ATTN_HEAD_DIM), lambda b, h, i: (0, 0)),
                  pl.BlockSpec((ATTN_V_DIM, tq), lambda b, h, i: (0, 0)),
                  pl.BlockSpec((1, 2, LANES, tq), lambda b, h, i: (b, h, 0, i)),
                  pl.BlockSpec((1, 1, seq, LANES), lambda b, h, i: (b, h, 0, 0)),
                  pl.BlockSpec((1, 1, seq // tk, LANES, tk), lambda b, h, i: (b, h, 0, 0, 0))],
        out_specs=pl.BlockSpec((tq, LANES), lambda b, h, i: (b * nq + i, h)),
        out_shape=jax.ShapeDtypeStruct((batch * seq, ATTN_HEADS * ATTN_V_DIM), BF16),
        scratch_shapes=[pltpu.VMEM((2, ATTN_V_DIM, tq), F32)],
        compiler_params=_params(("arbitrary", "arbitrary", "arbitrary")),
        name="attn",
    )(lambda_qk.astype(F32), sg_b, qt, kk, vt)


def _ssd_body(z0_ref, z1_ref, x0_ref, x1_ref, bc_ref, dt_ref, cw_ref, cb_ref, dtb_ref, alog_ref,
              dsk_ref, ng_ref, tri_ref, y_ref, ext_scr, st_scr):
    L = SSD_CHUNK
    halo = 8
    c = pl.program_id(1)

    @pl.when(c == 0)
    def _():
        ext_scr[0:halo, :] = jnp.zeros((halo, SSD_CONV_DIM), F32)
        st_scr[...] = jnp.zeros(st_scr.shape, F32)

    ext_scr[halo:halo + L, 0:1024] = x0_ref[...].astype(F32)
    ext_scr[halo:halo + L, 1024:2048] = x1_ref[...].astype(F32)
    ext_scr[halo:halo + L, 2048:3072] = bc_ref[...].astype(F32)
    acc = cb_ref[...]
    for k in range(SSD_CONV):
        off = halo - (SSD_CONV - 1) + k
        acc = acc + cw_ref[k:k + 1, :] * ext_scr[off:off + L, :]
    u = acc * _sigmoid(acc)
    ext_scr[0:halo, :] = ext_scr[L:L + halo, :]

    xdt_in = dt_ref[...] + dtb_ref[...]
    dtp = jnp.maximum(xdt_in, 0.0) + jnp.log(1.0 + jnp.exp(-jnp.abs(xdt_in)))
    a = dtp * (-jnp.exp(alog_ref[...]))
    tri = tri_ref[...]
    a_hi, a_mid, a_lo = _split3(a)
    acum = _dot(tri, a_hi) + _dot(tri, a_mid) + _dot(tri, a_lo)
    acum_t = acum.T
    lane = lax.broadcasted_iota(jnp.int32, (L, LANES), 1)
    first = lane < SSD_HEAD_DIM
    tril = lax.broadcasted_iota(jnp.int32, (L, L), 0) >= lax.broadcasted_iota(jnp.int32, (L, L), 1)
    gn = SSD_GROUPS * SSD_D_STATE
    gw = SSD_D_INNER // SSD_GROUPS

    def col(mat, h):
        return jnp.broadcast_to(mat[:, h:h + 1], (L, LANES))

    for g in range(SSD_GROUPS):
        bg = u[:, SSD_D_INNER + g * SSD_D_STATE:SSD_D_INNER + (g + 1) * SSD_D_STATE]
        cg = u[:, SSD_D_INNER + gn + g * SSD_D_STATE:SSD_D_INNER + gn + (g + 1) * SSD_D_STATE].astype(BF16)
        bgt = bg.T.astype(BF16)
        cb = _dot(cg, bgt)
        sprev = st_scr[g]
        yoff = _dot(cg, sprev.astype(BF16))
        y_parts, xdd_parts, dec_parts = [], [], []
        for p in range(4):
            h0 = 8 * g + 2 * p
            c0, c1 = col(acum, h0), col(acum, h0 + 1)
            ac_exp = jnp.where(first, c0, c1)
            dt_exp = jnp.where(first, col(dtp, h0), col(dtp, h0 + 1))
            xsl = u[:, h0 * SSD_HEAD_DIM:h0 * SSD_HEAD_DIM + LANES]
            xdt = xsl * dt_exp
            l0 = jnp.exp(jnp.where(tril, c0 - acum_t[h0:h0 + 1, :], NEG))
            l1 = jnp.exp(jnp.where(tril, c1 - acum_t[h0 + 1:h0 + 2, :], NEG))
            m2 = jnp.concatenate([(cb * l0).astype(BF16), (cb * l1).astype(BF16)], axis=1)
            xb = xdt.astype(BF16)
            zb = jnp.zeros_like(xb)
            rhs = jnp.concatenate([jnp.where(first, xb, zb), jnp.where(first, zb, xb)], axis=0)
            yd = _dot(m2, rhs)
            yo = yoff[:, p * LANES:(p + 1) * LANES] * jnp.exp(ac_exp)
            dsk = dsk_ref[:, h0 * SSD_HEAD_DIM:h0 * SSD_HEAD_DIM + LANES]
            y_parts.append(yd + yo + xsl * dsk)
            last = ac_exp[L - 1:L, :]
            xdd_parts.append((xdt * jnp.exp(last - ac_exp)).astype(BF16))
            dec_parts.append(jnp.exp(last))
        xdd = jnp.concatenate(xdd_parts, axis=1)
        dec = jnp.concatenate(dec_parts, axis=1)
        st_scr[g] = sprev * dec + _dot(bgt, xdd)
        yg = jnp.concatenate(y_parts, axis=1)
        zref = z0_ref if g < 2 else z1_ref
        zg = zref[:, (g % 2) * gw:(g % 2 + 1) * gw].astype(F32)
        yg = yg * (zg * _sigmoid(zg))
        ms = jnp.mean(yg * yg, axis=-1, keepdims=True)
        yg = yg * lax.rsqrt(ms + EPS) * ng_ref[:, g * gw:(g + 1) * gw]
        y_ref[:, g * gw:(g + 1) * gw] = yg.astype(BF16)


def _ssd(proj, dt, conv_w, conv_b, dt_bias, a_log, d_skip, norm_g, batch, seq):
    L = SSD_CHUNK
    nc = seq // L
    pad = LANES - SSD_HEADS
    dtb = jnp.pad(dt_bias.astype(F32), (0, pad)).reshape(1, LANES)
    alog = jnp.pad(a_log.astype(F32), (0, pad)).reshape(1, LANES)
    dsk = jnp.repeat(d_skip.astype(F32), SSD_HEAD_DIM).reshape(1, SSD_D_INNER)
    tri = jnp.tril(jnp.ones((L, L), F32)).astype(BF16)
    row = lambda b, c: (b * nc + c)
    const = lambda b, c: (0, 0)
    return pl.pallas_call(
        _ssd_body,
        grid=(batch, nc),
        in_specs=[pl.BlockSpec((L, 1024), lambda b, c: (row(b, c), 5)),
                  pl.BlockSpec((L, 1024), lambda b, c: (row(b, c), 6)),
                  pl.BlockSpec((L, 1024), lambda b, c: (row(b, c), 7)),
                  pl.BlockSpec((L, 1024), lambda b, c: (row(b, c), 8)),
                  pl.BlockSpec((L, 1024), lambda b, c: (row(b, c), 9)),
                  pl.BlockSpec((L, LANES), lambda b, c: (row(b, c), 0)),
                  pl.BlockSpec((SSD_CONV, SSD_CONV_DIM), const),
                  pl.BlockSpec((1, SSD_CONV_DIM), const),
                  pl.BlockSpec((1, LANES), const),
                  pl.BlockSpec((1, LANES), const),
                  pl.BlockSpec((1, SSD_D_INNER), const),
                  pl.BlockSpec((1, SSD_D_INNER), const),
                  pl.BlockSpec((L, L), const)],
        out_specs=pl.BlockSpec((L, SSD_D_INNER), lambda b, c: (row(b, c), 0)),
        out_shape=jax.ShapeDtypeStruct((batch * seq, SSD_D_INNER), BF16),
        scratch_shapes=[pltpu.VMEM((L + 8, SSD_CONV_DIM), F32),
                        pltpu.VMEM((SSD_GROUPS, SSD_D_STATE, SSD_D_INNER // SSD_GROUPS), F32)],
        compiler_params=_params(("arbitrary", "arbitrary")),
        name="ssd",
    )(proj, proj, proj, proj, proj, dt, conv_w.astype(F32), conv_b.astype(F32).reshape(1, -1), dtb, alog,
      dsk, norm_g.astype(F32).reshape(1, -1), tri)


def _outproj_body(ya_ref, yb_ref, ga_ref, gb_ref, x_ref, g1_ref, sc_ref, sh_ref, n2_ref,
                  wa_ref, wb_ref, wo_ref, wr_ref, x1_ref, h2_ref, lg_ref):
    pa = _dot(ya_ref[...], wa_ref[...])
    pb = _dot(yb_ref[...], wb_ref[...])
    merged = _sigmoid(ga_ref[...].astype(F32)) * pa + _sigmoid(gb_ref[...].astype(F32)) * pb
    x1 = x_ref[...] + g1_ref[0] * _dot(merged.astype(BF16), wo_ref[...])
    x1_ref[...] = x1
    ms = jnp.mean(x1 * x1, axis=-1, keepdims=True)
    h2 = x1 * lax.rsqrt(ms + EPS) * n2_ref[...]
    h2 = h2 * (1.0 + sc_ref[0]) + sh_ref[0]
    hb = h2.astype(BF16)
    h2_ref[...] = hb
    lg_ref[...] = _dot(hb, wr_ref[...])


def _outproj(ya, yb, proj, x2d, g1, sc2, sh2, norm2_g, wa, wb, wo, wr, seq):
    t, d = x2d.shape
    tm = min(512, seq)
    per_b = seq // tm
    const = lambda i: (0, 0)
    mod = lambda i: (i // per_b, 0, 0)
    return pl.pallas_call(
        _outproj_body,
        grid=(t // tm,),
        in_specs=[pl.BlockSpec((tm, d), lambda i: (i, 0)),
                  pl.BlockSpec((tm, SSD_D_INNER), lambda i: (i, 0)),
                  pl.BlockSpec((tm, d), lambda i: (i, 3)),
                  pl.BlockSpec((tm, d), lambda i: (i, 4)),
                  pl.BlockSpec((tm, d), lambda i: (i, 0)),
                  pl.BlockSpec((1, 1, d), mod),
                  pl.BlockSpec((1, 1, d), mod),
                  pl.BlockSpec((1, 1, d), mod),
                  pl.BlockSpec((1, d), const),
                  pl.BlockSpec((d, d), const),
                  pl.BlockSpec((SSD_D_INNER, d), const),
                  pl.BlockSpec((d, d), const),
                  pl.BlockSpec((d, LANES), const)],
        out_specs=[pl.BlockSpec((tm, d), lambda i: (i, 0)),
                   pl.BlockSpec((tm, d), lambda i: (i, 0)),
                   pl.BlockSpec((tm, LANES), lambda i: (i, 0))],
        out_shape=[jax.ShapeDtypeStruct((t, d), F32),
                   jax.ShapeDtypeStruct((t, d), BF16),
                   jax.ShapeDtypeStruct((t, LANES), F32)],
        compiler_params=_params(("arbitrary",)),
        name="outproj",
    )(ya, yb, proj, proj, x2d, g1, sc2, sh2, norm2_g.astype(F32).reshape(1, d), wa, wb, wo, wr)


def _router_body(lg_ref, bias_ref, gates_ref):
    tr = lg_ref.shape[0]
    per = N_EXPERTS // N_EXPERT_GROUPS
    lt = lg_ref[...].T
    scores = _sigmoid(lt[:N_EXPERTS])
    choice = scores + bias_ref[...]
    iota = lax.broadcasted_iota(jnp.int32, (per, tr), 0)
    grp = [choice[g * per:(g + 1) * per] for g in range(N_EXPERT_GROUPS)]
    sc_g = [scores[g * per:(g + 1) * per] for g in range(N_EXPERT_GROUPS)]

    gsc = jnp.zeros((N_EXPERT_GROUPS, tr), F32)
    for g in range(N_EXPERT_GROUPS):
        top1 = jnp.max(grp[g], axis=0, keepdims=True)
        idx1 = jnp.min(jnp.where(grp[g] == top1, iota, per), axis=0, keepdims=True)
        top2 = jnp.max(jnp.where(iota == idx1, NEG, grp[g]), axis=0, keepdims=True)
        gsc = jnp.where(iota == g, top1 + top2, gsc)

    gsel = jnp.zeros((N_EXPERT_GROUPS, tr), F32)
    cur = gsc
    for _ in range(TOPK_GROUPS):
        mx = jnp.max(cur, axis=0, keepdims=True)
        idx = jnp.min(jnp.where(cur == mx, iota, N_EXPERT_GROUPS), axis=0, keepdims=True)
        hit = iota == idx
        gsel = jnp.where(hit, 1.0, gsel)
        cur = jnp.where(hit, NEG, cur)

    cur_g = [jnp.where(gsel[g:g + 1] > 0.0, grp[g], NEG) for g in range(N_EXPERT_GROUPS)]
    sel_g = [jnp.zeros((per, tr), F32) for _ in range(N_EXPERT_GROUPS)]
    for _ in range(TOP_K):
        mx = cur_g[0]
        for g in range(1, N_EXPERT_GROUPS):
            mx = jnp.maximum(mx, cur_g[g])
        mx = jnp.max(mx, axis=0, keepdims=True)
        idx = jnp.where(cur_g[0] == mx, iota, N_EXPERTS)
        for g in range(1, N_EXPERT_GROUPS):
            idx = jnp.minimum(idx, jnp.where(cur_g[g] == mx, iota + g * per, N_EXPERTS))
        idx = jnp.min(idx, axis=0, keepdims=True)
        for g in range(N_EXPERT_GROUPS):
            hit = (iota + g * per) == idx
            sel_g[g] = jnp.where(hit, 1.0, sel_g[g])
            cur_g[g] = jnp.where(hit, NEG, cur_g[g])

    wsum = jnp.zeros((1, tr), F32)
    for g in range(N_EXPERT_GROUPS):
        wsum = wsum + jnp.sum(sel_g[g] * sc_g[g], axis=0, keepdims=True)
    gates_t = jnp.concatenate([sel_g[g] * sc_g[g] / wsum * ROUTED_SCALE for g in range(N_EXPERT_GROUPS)]
                              + [jnp.zeros((LANES - N_EXPERTS, tr), F32)], axis=0)
    gates_ref[...] = gates_t.T


def _router(logits, router_bias, seq):
    t = logits.shape[0]
    tr = min(1024, seq)
    bias_b = jnp.broadcast_to(router_bias.astype(F32)[:, None], (N_EXPERTS, tr))
    return pl.pallas_call(
        _router_body,
        grid=(t // tr,),
        in_specs=[pl.BlockSpec((tr, LANES), lambda i: (i, 0)),
                  pl.BlockSpec((N_EXPERTS, tr), lambda i: (0, 0))],
        out_specs=pl.BlockSpec((tr, LANES), lambda i: (i, 0)),
        out_shape=jax.ShapeDtypeStruct((t, LANES), F32),
        compiler_params=_params(("arbitrary",)),
        name="router",
    )(logits, bias_b)


def _moe_body(h_ref, gates_ref, x1_ref, g2_ref, wg_ref, wu_ref, wd_ref, sg_ref, su_ref, sd_ref,
              o_ref, acc_scr, *, epb):
    e = pl.program_id(1)
    h = h_ref[...]

    def ffn(wg, wu, wd, gate):
        a = _dot(h, wg.astype(BF16))
        b = _dot(h, wu.astype(BF16))
        act = a * _sigmoid(a) * b
        if gate is not None:
            act = act * gate
        return _dot(act.astype(BF16), wd.astype(BF16))

    @pl.when(e == 0)
    def _():
        acc_scr[...] = ffn(sg_ref[...], su_ref[...], sd_ref[...], None)

    gates = gates_ref[...]
    lane = lax.broadcasted_iota(jnp.int32, gates.shape, 1)
    tot = acc_scr[...]
    for q in range(epb):
        gcol = jnp.sum(jnp.where(lane == e * epb + q, gates, 0.0), axis=-1, keepdims=True)
        tot = tot + ffn(wg_ref[q], wu_ref[q], wd_ref[q], gcol)
    acc_scr[...] = tot

    @pl.when(e == pl.num_programs(1) - 1)
    def _():
        o_ref[...] = x1_ref[...] + g2_ref[0] * acc_scr[...]


def _moe(h2, gates, x1, g2, w_eg, w_eu, w_ed, w_sg, w_su, w_sd, seq):
    t, d = x1.shape
    tm = min(1024, seq)
    per_b = seq // tm
    epb = 2
    f = EXPERT_DIM
    const = lambda i, e: (0, 0)
    return pl.pallas_call(
        functools.partial(_moe_body, epb=epb),
        grid=(t // tm, N_EXPERTS // epb),
        in_specs=[pl.BlockSpec((tm, d), lambda i, e: (i, 0)),
                  pl.BlockSpec((tm, LANES), lambda i, e: (i, 0)),
                  pl.BlockSpec((tm, d), lambda i, e: (i, 0)),
                  pl.BlockSpec((1, 1, d), lambda i, e: (i // per_b, 0, 0)),
                  pl.BlockSpec((epb, d, f), lambda i, e: (e, 0, 0)),
                  pl.BlockSpec((epb, d, f), lambda i, e: (e, 0, 0)),
                  pl.BlockSpec((epb, f, d), lambda i, e: (e, 0, 0)),
                  pl.BlockSpec((d, f), const),
                  pl.BlockSpec((d, f), const),
                  pl.BlockSpec((f, d), const)],
        out_specs=pl.BlockSpec((tm, d), lambda i, e: (i, 0)),
        out_shape=jax.ShapeDtypeStruct((t, d), F32),
        scratch_shapes=[pltpu.VMEM((tm, d), F32)],
        compiler_params=_params(("arbitrary", "arbitrary")),
        name="moe",
    )(h2, gates, x1, g2, w_eg, w_eu, w_ed, w_sg, w_su, w_sd)


def kernel(x, c, positions, w_ada, b_ada, norm1_g, w_in, q_norm_g, k_norm_g, lambda_qk, subln_g, conv_w, conv_b, dt_bias, a_log, d_skip, ssd_norm_g, w_branch_a, w_branch_b, w_out, norm2_g, w_router, router_bias, w_exp_gate, w_exp_up, w_exp_down, w_sh_gate, w_sh_up, w_sh_down):
    batch, seq, d = x.shape
    t = batch * seq
    tq = min(512, seq)
    tk = min(256, seq)
    xf = x.reshape(t, d)
    for layer in range(w_ada.shape[0]):
        lambda_init = 0.8 - 0.6 * math.exp(-0.3 * layer)
        c_pad = jnp.pad(c.astype(F32), ((0, 8 - batch), (0, 0)))
        mod = _ada(c_pad, w_ada[layer], b_ada[layer])[:batch].reshape(batch, 6, 1, d)
        sh1, sc1, g1, sh2, sc2, g2 = (mod[:, n] for n in range(6))

        w_l = w_in[layer]
        w_main = w_l[:, :MAIN_COLS].astype(BF16)
        wdt = jnp.pad(w_l[:, MAIN_COLS:], ((0, 0), (0, LANES - SSD_HEADS)))
        wdt_hi, wdt_lo = _split2(wdt)
        proj, dt = _inproj(xf, norm1_g[layer], sc1, sh1, w_main, wdt_hi, wdt_lo, seq)

        qt, kk, vt = _qkvprep(proj, positions, q_norm_g[layer], k_norm_g[layer], batch, seq, tk)
        ya = _attention(qt, kk, vt, lambda_qk[layer], subln_g[layer], lambda_init, batch, seq, tq, tk)
        yb = _ssd(proj, dt, conv_w[layer], conv_b[layer], dt_bias[layer], a_log[layer], d_skip[layer],
                  ssd_norm_g[layer], batch, seq)

        wr = jnp.pad(w_router[layer], ((0, 0), (0, LANES - N_EXPERTS))).astype(BF16)
        x1, h2, logits = _outproj(ya, yb, proj, xf, g1, sc2, sh2, norm2_g[layer],
                                  w_branch_a[layer].astype(BF16), w_branch_b[layer].astype(BF16),
                                  w_out[layer].astype(BF16), wr, seq)
        gates = _router(logits, router_bias[layer], seq)
        xf = _moe(h2, gates, x1, g2, w_exp_gate[layer], w_exp_up[layer], w_exp_down[layer],
                  w_sh_gate[layer], w_sh_up[layer], w_sh_down[layer], seq)
    return xf.reshape(batch, seq, d)
```

```python
import functools
import math

import jax
import jax.numpy as jnp
from jax import lax
from jax.experimental import pallas as pl
from jax.experimental.pallas import tpu as pltpu

F32 = jnp.float32
BF16 = jnp.bfloat16

D_MODEL = 1024
ATTN_HEADS = 8
ATTN_HEAD_DIM = 64
ATTN_V_DIM = 128
ROPE_THETA = 10000.0
SSD_D_INNER = 2048
SSD_HEAD_DIM = 64
SSD_HEADS = 32
SSD_GROUPS = 4
SSD_D_STATE = 128
SSD_CONV = 4
SSD_CHUNK = 128
SSD_CONV_DIM = 3072
N_EXPERTS = 64
TOP_K = 8
N_EXPERT_GROUPS = 8
TOPK_GROUPS = 4
EXPERT_DIM = 256
ROUTED_SCALE = 2.5
EPS = 1e-6
NEG = -1e30
Q_SCALE = ATTN_HEAD_DIM ** -0.5 * math.log2(math.e)

LANES = 128
MAIN_COLS = 10240
VMEM_LIMIT = 56 * 1024 * 1024


def _dot(a, b):
    return jnp.dot(a, b, preferred_element_type=F32)


def _sigmoid(x):
    return 1.0 / (1.0 + jnp.exp(-x))


def _split2(a):
    hi = a.astype(BF16)
    lo = (a - hi.astype(F32)).astype(BF16)
    return hi, lo


def _split3(a):
    hi = a.astype(BF16)
    r = a - hi.astype(F32)
    mid = r.astype(BF16)
    lo = (r - mid.astype(F32)).astype(BF16)
    return hi, mid, lo


def _params(sem, vmem=VMEM_LIMIT):
    return pltpu.CompilerParams(dimension_semantics=sem, vmem_limit_bytes=vmem)


def _ada_body(c_ref, w_ref, b_ref, o_ref):
    c = c_ref[...]
    sc = c * _sigmoid(c)
    chi, clo = _split2(sc)
    whi, wlo = _split2(w_ref[...])
    o_ref[...] = _dot(chi, whi) + _dot(chi, wlo) + _dot(clo, whi) + b_ref[...]


def _ada(c_pad, w, b):
    d, n = w.shape
    tn = 1024
    return pl.pallas_call(
        _ada_body,
        grid=(n // tn,),
        in_specs=[pl.BlockSpec((8, d), lambda j: (0, 0)),
                  pl.BlockSpec((d, tn), lambda j: (0, j)),
                  pl.BlockSpec((1, tn), lambda j: (0, j))],
        out_specs=pl.BlockSpec((8, tn), lambda j: (0, j)),
        out_shape=jax.ShapeDtypeStruct((8, n), F32),
        compiler_params=_params(("arbitrary",)),
        name="ada",
    )(c_pad, w, b.reshape(1, n))


def _inproj_body(x_ref, g_ref, sc_ref, sh_ref, w_ref, wdh_ref, wdl_ref, o_ref, dt_ref, h_scr):
    @pl.when(pl.program_id(1) == 0)
    def _():
        x = x_ref[...]
        ms = jnp.mean(x * x, axis=-1, keepdims=True)
        h = x * lax.rsqrt(ms + EPS) * g_ref[...]
        h = h * (1.0 + sc_ref[0]) + sh_ref[0]
        hb, hl = _split2(h)
        h_scr[...] = hb
        dt_ref[...] = _dot(hb, wdh_ref[...]) + _dot(hb, wdl_ref[...]) + _dot(hl, wdh_ref[...])

    o_ref[...] = _dot(h_scr[...], w_ref[...]).astype(BF16)


def _inproj(x2d, g, sc, sh, w_main, wdt_hi, wdt_lo, seq):
    t, d = x2d.shape
    tm = min(1024, seq)
    tn = 1024
    per_b = seq // tm
    return pl.pallas_call(
        _inproj_body,
        grid=(t // tm, MAIN_COLS // tn),
        in_specs=[pl.BlockSpec((tm, d), lambda i, j: (i, 0)),
                  pl.BlockSpec((1, d), lambda i, j: (0, 0)),
                  pl.BlockSpec((1, 1, d), lambda i, j: (i // per_b, 0, 0)),
                  pl.BlockSpec((1, 1, d), lambda i, j: (i // per_b, 0, 0)),
                  pl.BlockSpec((d, tn), lambda i, j: (0, j)),
                  pl.BlockSpec((d, LANES), lambda i, j: (0, 0)),
                  pl.BlockSpec((d, LANES), lambda i, j: (0, 0))],
        out_specs=[pl.BlockSpec((tm, tn), lambda i, j: (i, j)),
                   pl.BlockSpec((tm, LANES), lambda i, j: (i, 0))],
        out_shape=[jax.ShapeDtypeStruct((t, MAIN_COLS), BF16),
                   jax.ShapeDtypeStruct((t, LANES), F32)],
        scratch_shapes=[pltpu.VMEM((tm, d), BF16)],
        compiler_params=_params(("arbitrary", "arbitrary")),
        name="inproj",
    )(x2d, g.reshape(1, d), sc, sh, w_main, wdt_hi, wdt_lo)


def _qkvprep_body(q_ref, k_ref, v_ref, pos_ref, invf_ref, gq_ref, gk_ref, qt_ref, ko_ref, vt_ref, *, tk):
    ts = q_ref.shape[0]
    half = ATTN_HEAD_DIM // 2
    ang = pos_ref[0].astype(F32) * invf_ref[...]
    cos = jnp.cos(ang)
    sin = jnp.sin(ang)
    zeros = jnp.zeros((ATTN_HEAD_DIM, ts), F32)

    def norm_rope(blk, g):
        ms = jnp.mean(blk * blk, axis=0, keepdims=True)
        r = blk * lax.rsqrt(ms + EPS) * g
        x1 = r[:half]
        x2 = r[half:]
        return jnp.concatenate([x1 * cos - x2 * sin, x2 * cos + x1 * sin], axis=0)

    for h in range(ATTN_HEADS):
        sl = slice(h * LANES, (h + 1) * LANES)
        qht = q_ref[:, sl].astype(F32).T
        q1 = norm_rope(qht[:ATTN_HEAD_DIM], gq_ref[...]) * Q_SCALE
        q2 = norm_rope(qht[ATTN_HEAD_DIM:], gq_ref[...]) * Q_SCALE
        qt_ref[0, 2 * h] = jnp.concatenate([q1, zeros], axis=0).astype(BF16)
        qt_ref[0, 2 * h + 1] = jnp.concatenate([zeros, q2], axis=0).astype(BF16)
        kht = k_ref[:, sl].astype(F32).T
        k1 = norm_rope(kht[:ATTN_HEAD_DIM], gk_ref[...])
        k2 = norm_rope(kht[ATTN_HEAD_DIM:], gk_ref[...])
        ko_ref[0, h] = jnp.concatenate([k1, k2], axis=0).T.astype(BF16)
        vht = v_ref[:, sl].astype(F32).T.astype(BF16)
        for cc in range(ts // tk):
            vt_ref[0, h, cc] = vht[:, cc * tk:(cc + 1) * tk]


def _qkvprep(proj, positions, q_norm_g, k_norm_g, batch, seq, tk):
    ts = min(512, seq)
    ns = seq // ts
    half = ATTN_HEAD_DIM // 2
    inv_freq = ROPE_THETA ** (-jnp.arange(half, dtype=F32) * 2.0 / ATTN_HEAD_DIM)
    invf_b = jnp.broadcast_to(inv_freq[:, None], (half, ts))
    gq_b = jnp.broadcast_to(q_norm_g.astype(F32)[:, None], (ATTN_HEAD_DIM, ts))
    gk_b = jnp.broadcast_to(k_norm_g.astype(F32)[:, None], (ATTN_HEAD_DIM, ts))
    pos3 = positions.reshape(batch, 1, seq)
    const = lambda b, i: (0, 0)
    return pl.pallas_call(
        functools.partial(_qkvprep_body, tk=tk),
        grid=(batch, ns),
        in_specs=[pl.BlockSpec((ts, D_MODEL), lambda b, i: (b * ns + i, 0)),
                  pl.BlockSpec((ts, D_MODEL), lambda b, i: (b * ns + i, 1)),
                  pl.BlockSpec((ts, D_MODEL), lambda b, i: (b * ns + i, 2)),
                  pl.BlockSpec((1, 1, ts), lambda b, i: (b, 0, i)),
                  pl.BlockSpec((half, ts), const),
                  pl.BlockSpec((ATTN_HEAD_DIM, ts), const),
                  pl.BlockSpec((ATTN_HEAD_DIM, ts), const)],
        out_specs=[pl.BlockSpec((1, 2 * ATTN_HEADS, LANES, ts), lambda b, i: (b, 0, 0, i)),
                   pl.BlockSpec((1, ATTN_HEADS, ts, LANES), lambda b, i: (b, 0, i, 0)),
                   pl.BlockSpec((1, ATTN_HEADS, ts // tk, LANES, tk), lambda b, i: (b, 0, i, 0, 0))],
        out_shape=[jax.ShapeDtypeStruct((batch, 2 * ATTN_HEADS, LANES, seq), BF16),
                   jax.ShapeDtypeStruct((batch, ATTN_HEADS, seq, LANES), BF16),
                   jax.ShapeDtypeStruct((batch, ATTN_HEADS, seq // tk, LANES, tk), BF16)],
        compiler_params=_params(("arbitrary", "arbitrary")),
        name="qkvprep",
    )(proj, proj, proj, pos3, invf_b, gq_b, gk_b)


def _attn_body(lam_ref, sg_ref, qt_ref, k_ref, vt_ref, o_ref, acc_scr, sa_scr, sb_scr, *, tq, tk, lambda_init):
    i = pl.program_id(2)
    q_maps = (qt_ref[0, 0], qt_ref[0, 1])
    acc_scr[...] = jnp.zeros(acc_scr.shape, F32)
    row = lax.broadcasted_iota(jnp.int32, (tk, tq), 0)
    colq = lax.broadcasted_iota(jnp.int32, (tk, tq), 1)

    def qk(j, s_scr, lo=0):
        kb = k_ref[0, 0, pl.ds(pl.multiple_of(j * tk, tk), tk), :]
        for mi in range(2):
            s_scr[mi, :, lo:] = _dot(kb, q_maps[mi][:, lo:])

    def process(j, s_scr, carry, diag=None, lo=0):
        vb = vt_ref[0, 0, j]
        out = []
        for mi in range(2):
            m_old, l_old = carry[2 * mi], carry[2 * mi + 1]
            s = s_scr[mi]
            if diag is not None:
                s = jnp.where(row + diag * tk <= colq, s, NEG)
            m_new = jnp.maximum(m_old, jnp.max(s, axis=0, keepdims=True))
            alpha = jnp.exp2(m_old - m_new)
            p = jnp.exp2(s - m_new)
            l_new = alpha * l_old + jnp.sum(p, axis=0, keepdims=True)
            alpha_b = jnp.broadcast_to(alpha, (ATTN_V_DIM, tq))
            acc_scr[mi, :, lo:] = alpha_b[:, lo:] * acc_scr[mi, :, lo:] + _dot(vb, p[:, lo:].astype(BF16))
            out += [m_new, l_new]
        return tuple(out)

    def pair(p, carry):
        qk(2 * p + 1, sb_scr)
        carry = process(2 * p, sa_scr, carry)
        qk(2 * p + 2, sa_scr)
        return process(2 * p + 1, sb_scr, carry)

    @pl.when(i == 0)
    def _():
        sb_scr[:, :, :tk] = jnp.zeros((2, tk, tk), F32)

    init = (jnp.full((1, tq), NEG, F32), jnp.zeros((1, tq), F32)) * 2
    qk(0, sa_scr)
    carry = lax.fori_loop(0, i, pair, init)
    qk(2 * i + 1, sb_scr, lo=tk)
    carry = process(2 * i, sa_scr, carry, diag=0)
    carry = process(2 * i + 1, sb_scr, carry, diag=1, lo=tk)

    lv = lam_ref[...]
    lam = (jnp.exp(jnp.sum(lv[0:1] * lv[1:2], keepdims=True))
           - jnp.exp(jnp.sum(lv[2:3] * lv[3:4], keepdims=True)) + lambda_init)
    o = acc_scr[0] / carry[1] - lam * (acc_scr[1] / carry[3])
    ms = jnp.mean(o * o, axis=0, keepdims=True)
    o = o * lax.rsqrt(ms + EPS) * sg_ref[...] * (1.0 - lambda_init)
    o_ref[...] = o.T.astype(BF16)


def _attention(qt, kk, vt, lambda_qk, subln_g, lambda_init, batch, seq, tq, tk):
    nq = seq // tq
    sg_b = jnp.broadcast_to(subln_g.astype(F32)[:, None], (ATTN_V_DIM, tq))
    return pl.pallas_call(
        functools.partial(_attn_body, tq=tq, tk=tk, lambda_init=lambda_init),
        grid=(batch, ATTN_HEADS, nq),
        in_specs=[pl.BlockSpec((4, ATTN_HEAD_DIM), lambda b, h, i: (0, 0)),
                  pl.BlockSpec((ATTN_V_DIM, tq), lambda b, h, i: (0, 0)),
                  pl.BlockSpec((1, 2, LANES, tq), lambda b, h, i: (b, h, 0, i)),
                  pl.BlockSpec((1, 1, seq, LANES), lambda b, h, i: (b, h, 0, 0)),
                  pl.BlockSpec((1, 1, seq // tk, LANES, tk), lambda b, h, i: (b, h, 0, 0, 0))],
        out_specs=pl.BlockSpec((tq, LANES), lambda b, h, i: (b * nq + i, h)),
        out_shape=jax.ShapeDtypeStruct((batch * seq, ATTN_HEADS * ATTN_V_DIM), BF16),
        scratch_shapes=[pltpu.VMEM((2, ATTN_V_DIM, tq), F32),
                        pltpu.VMEM((2, tk, tq), F32),
                        pltpu.VMEM((2, tk, tq), F32)],
        compiler_params=_params(("arbitrary", "arbitrary", "arbitrary")),
        name="attn",
    )(lambda_qk.astype(F32), sg_b, qt, kk, vt)


def _ssd_body(z0_ref, z1_ref, x0_ref, x1_ref, bc_ref, dt_ref, cw_ref, cb_ref, dtb_ref, alog_ref,
              dsk_ref, ng_ref, tri_ref, y_ref, ext_scr, st_scr):
    L = SSD_CHUNK
    halo = 8
    c = pl.program_id(1)

    @pl.when(c == 0)
    def _():
        ext_scr[0:halo, :] = jnp.zeros((halo, SSD_CONV_DIM), F32)
        st_scr[...] = jnp.zeros(st_scr.shape, F32)

    ext_scr[halo:halo + L, 0:1024] = x0_ref[...].astype(F32)
    ext_scr[halo:halo + L, 1024:2048] = x1_ref[...].astype(F32)
    ext_scr[halo:halo + L, 2048:3072] = bc_ref[...].astype(F32)
    acc = cb_ref[...]
    for k in range(SSD_CONV):
        off = halo - (SSD_CONV - 1) + k
        acc = acc + cw_ref[k:k + 1, :] * ext_scr[off:off + L, :]
    u = acc * _sigmoid(acc)
    ext_scr[0:halo, :] = ext_scr[L:L + halo, :]

    xdt_in = dt_ref[...] + dtb_ref[...]
    dtp = jnp.maximum(xdt_in, 0.0) + jnp.log(1.0 + jnp.exp(-jnp.abs(xdt_in)))
    a = dtp * (-jnp.exp(alog_ref[...]))
    tri = tri_ref[...]
    a_hi, a_mid, a_lo = _split3(a)
    acum = _dot(tri, a_hi) + _dot(tri, a_mid) + _dot(tri, a_lo)
    acum_t = acum.T
    lane = lax.broadcasted_iota(jnp.int32, (L, LANES), 1)
    first = lane < SSD_HEAD_DIM
    tril = lax.broadcasted_iota(jnp.int32, (L, L), 0) >= lax.broadcasted_iota(jnp.int32, (L, L), 1)
    gn = SSD_GROUPS * SSD_D_STATE
    gw = SSD_D_INNER // SSD_GROUPS

    def col(mat, h):
        return jnp.broadcast_to(mat[:, h:h + 1], (L, LANES))

    for g in range(SSD_GROUPS):
        bg = u[:, SSD_D_INNER + g * SSD_D_STATE:SSD_D_INNER + (g + 1) * SSD_D_STATE]
        cg = u[:, SSD_D_INNER + gn + g * SSD_D_STATE:SSD_D_INNER + gn + (g + 1) * SSD_D_STATE].astype(BF16)
        bgt = bg.T.astype(BF16)
        cb = _dot(cg, bgt)
        sprev = st_scr[g]
        yoff = _dot(cg, sprev.astype(BF16))
        y_parts, xdd_parts, dec_parts = [], [], []
        for p in range(4):
            h0 = 8 * g + 2 * p
            c0, c1 = col(acum, h0), col(acum, h0 + 1)
            ac_exp = jnp.where(first, c0, c1)
            dt_exp = jnp.where(first, col(dtp, h0), col(dtp, h0 + 1))
            xsl = u[:, h0 * SSD_HEAD_DIM:h0 * SSD_HEAD_DIM + LANES]
            xdt = xsl * dt_exp
            l0 = jnp.exp(jnp.where(tril, c0 - acum_t[h0:h0 + 1, :], NEG))
            l1 = jnp.exp(jnp.where(tril, c1 - acum_t[h0 + 1:h0 + 2, :], NEG))
            m2 = jnp.concatenate([(cb * l0).astype(BF16), (cb * l1).astype(BF16)], axis=1)
            xb = xdt.astype(BF16)
            zb = jnp.zeros_like(xb)
            rhs = jnp.concatenate([jnp.where(first, xb, zb), jnp.where(first, zb, xb)], axis=0)
            yd = _dot(m2, rhs)
            yo = yoff[:, p * LANES:(p + 1) * LANES] * jnp.exp(ac_exp)
            dsk = dsk_ref[:, h0 * SSD_HEAD_DIM:h0 * SSD_HEAD_DIM + LANES]
            y_parts.append(yd + yo + xsl * dsk)
            last = ac_exp[L - 1:L, :]
            xdd_parts.append((xdt * jnp.exp(last - ac_exp)).astype(BF16))
            dec_parts.append(jnp.exp(last))
        xdd = jnp.concatenate(xdd_parts, axis=1)
        dec = jnp.concatenate(dec_parts, axis=1)
        st_scr[g] = sprev * dec + _dot(bgt, xdd)
        yg = jnp.concatenate(y_parts, axis=1)
        zref = z0_ref if g < 2 else z1_ref
        zg = zref[:, (g % 2) * gw:(g % 2 + 1) * gw].astype(F32)
        yg = yg * (zg * _sigmoid(zg))
        ms = jnp.mean(yg * yg, axis=-1, keepdims=True)
        yg = yg * lax.rsqrt(ms + EPS) * ng_ref[:, g * gw:(g + 1) * gw]
        y_ref[:, g * gw:(g + 1) * gw] = yg.astype(BF16)


def _ssd(proj, dt, conv_w, conv_b, dt_bias, a_log, d_skip, norm_g, batch, seq):
    L = SSD_CHUNK
    nc = seq // L
    pad = LANES - SSD_HEADS
    dtb = jnp.pad(dt_bias.astype(F32), (0, pad)).reshape(1, LANES)
    alog = jnp.pad(a_log.astype(F32), (0, pad)).reshape(1, LANES)
    dsk = jnp.repeat(d_skip.astype(F32), SSD_HEAD_DIM).reshape(1, SSD_D_INNER)
    tri = jnp.tril(jnp.ones((L, L), F32)).astype(BF16)
    row = lambda b, c: (b * nc + c)
    const = lambda b, c: (0, 0)
    return pl.pallas_call(
        _ssd_body,
        grid=(batch, nc),
        in_specs=[pl.BlockSpec((L, 1024), lambda b, c: (row(b, c), 5)),
                  pl.BlockSpec((L, 1024), lambda b, c: (row(b, c), 6)),
                  pl.BlockSpec((L, 1024), lambda b, c: (row(b, c), 7)),
                  pl.BlockSpec((L, 1024), lambda b, c: (row(b, c), 8)),
                  pl.BlockSpec((L, 1024), lambda b, c: (row(b, c), 9)),
                  pl.BlockSpec((L, LANES), lambda b, c: (row(b, c), 0)),
                  pl.BlockSpec((SSD_CONV, SSD_CONV_DIM), const),
                  pl.BlockSpec((1, SSD_CONV_DIM), const),
                  pl.BlockSpec((1, LANES), const),
                  pl.BlockSpec((1, LANES), const),
                  pl.BlockSpec((1, SSD_D_INNER), const),
                  pl.BlockSpec((1, SSD_D_INNER), const),
                  pl.BlockSpec((L, L), const)],
        out_specs=pl.BlockSpec((L, SSD_D_INNER), lambda b, c: (row(b, c), 0)),
        out_shape=jax.ShapeDtypeStruct((batch * seq, SSD_D_INNER), BF16),
        scratch_shapes=[pltpu.VMEM((L + 8, SSD_CONV_DIM), F32),
                        pltpu.VMEM((SSD_GROUPS, SSD_D_STATE, SSD_D_INNER // SSD_GROUPS), F32)],
        compiler_params=_params(("arbitrary", "arbitrary")),
        name="ssd",
    )(proj, proj, proj, proj, proj, dt, conv_w.astype(F32), conv_b.astype(F32).reshape(1, -1), dtb, alog,
      dsk, norm_g.astype(F32).reshape(1, -1), tri)


def _outproj_body(ya_ref, yb_ref, ga_ref, gb_ref, x_ref, g1_ref, sc_ref, sh_ref, n2_ref,
                  wa_ref, wb_ref, wo_ref, wr_ref, x1_ref, h2_ref, lg_ref):
    pa = _dot(ya_ref[...], wa_ref[...])
    pb = _dot(yb_ref[...], wb_ref[...])
    merged = _sigmoid(ga_ref[...].astype(F32)) * pa + _sigmoid(gb_ref[...].astype(F32)) * pb
    x1 = x_ref[...] + g1_ref[0] * _dot(merged.astype(BF16), wo_ref[...])
    x1_ref[...] = x1
    ms = jnp.mean(x1 * x1, axis=-1, keepdims=True)
    h2 = x1 * lax.rsqrt(ms + EPS) * n2_ref[...]
    h2 = h2 * (1.0 + sc_ref[0]) + sh_ref[0]
    hb = h2.astype(BF16)
    h2_ref[...] = hb
    lg_ref[...] = _dot(hb, wr_ref[...])


def _outproj(ya, yb, proj, x2d, g1, sc2, sh2, norm2_g, wa, wb, wo, wr, seq):
    t, d = x2d.shape
    tm = min(512, seq)
    per_b = seq // tm
    const = lambda i: (0, 0)
    mod = lambda i: (i // per_b, 0, 0)
    return pl.pallas_call(
        _outproj_body,
        grid=(t // tm,),
        in_specs=[pl.BlockSpec((tm, d), lambda i: (i, 0)),
                  pl.BlockSpec((tm, SSD_D_INNER), lambda i: (i, 0)),
                  pl.BlockSpec((tm, d), lambda i: (i, 3)),
                  pl.BlockSpec((tm, d), lambda i: (i, 4)),
                  pl.BlockSpec((tm, d), lambda i: (i, 0)),
                  pl.BlockSpec((1, 1, d), mod),
                  pl.BlockSpec((1, 1, d), mod),
                  pl.BlockSpec((1, 1, d), mod),
                  pl.BlockSpec((1, d), const),
                  pl.BlockSpec((d, d), const),
                  pl.BlockSpec((SSD_D_INNER, d), const),
                  pl.BlockSpec((d, d), const),
                  pl.BlockSpec((d, LANES), const)],
        out_specs=[pl.BlockSpec((tm, d), lambda i: (i, 0)),
                   pl.BlockSpec((tm, d), lambda i: (i, 0)),
                   pl.BlockSpec((tm, LANES), lambda i: (i, 0))],
        out_shape=[jax.ShapeDtypeStruct((t, d), F32),
                   jax.ShapeDtypeStruct((t, d), BF16),
                   jax.ShapeDtypeStruct((t, LANES), F32)],
        compiler_params=_params(("arbitrary",)),
        name="outproj",
    )(ya, yb, proj, proj, x2d, g1, sc2, sh2, norm2_g.astype(F32).reshape(1, d), wa, wb, wo, wr)


def _router_body(lg_ref, bias_ref, gates_ref):
    tr = lg_ref.shape[0]
    per = N_EXPERTS // N_EXPERT_GROUPS
    lt = lg_ref[...].T
    scores = _sigmoid(lt[:N_EXPERTS])
    choice = scores + bias_ref[...]
    iota = lax.broadcasted_iota(jnp.int32, (per, tr), 0)
    grp = [choice[g * per:(g + 1) * per] for g in range(N_EXPERT_GROUPS)]
    sc_g = [scores[g * per:(g + 1) * per] for g in range(N_EXPERT_GROUPS)]

    gsc = jnp.zeros((N_EXPERT_GROUPS, tr), F32)
    for g in range(N_EXPERT_GROUPS):
        top1 = jnp.max(grp[g], axis=0, keepdims=True)
        idx1 = jnp.min(jnp.where(grp[g] == top1, iota, per), axis=0, keepdims=True)
        top2 = jnp.max(jnp.where(iota == idx1, NEG, grp[g]), axis=0, keepdims=True)
        gsc = jnp.where(iota == g, top1 + top2, gsc)

    gsel = jnp.zeros((N_EXPERT_GROUPS, tr), F32)
    cur = gsc
    for _ in range(TOPK_GROUPS):
        mx = jnp.max(cur, axis=0, keepdims=True)
        idx = jnp.min(jnp.where(cur == mx, iota, N_EXPERT_GROUPS), axis=0, keepdims=True)
        hit = iota == idx
        gsel = jnp.where(hit, 1.0, gsel)
        cur = jnp.where(hit, NEG, cur)

    cur_g = [jnp.where(gsel[g:g + 1] > 0.0, grp[g], NEG) for g in range(N_EXPERT_GROUPS)]
    sel_g = [jnp.zeros((per, tr), F32) for _ in range(N_EXPERT_GROUPS)]
    for _ in range(TOP_K):
        mx = cur_g[0]
        for g in range(1, N_EXPERT_GROUPS):
            mx = jnp.maximum(mx, cur_g[g])
        mx = jnp.max(mx, axis=0, keepdims=True)
        idx = jnp.where(cur_g[0] == mx, iota, N_EXPERTS)
        for g in range(1, N_EXPERT_GROUPS):
            idx = jnp.minimum(idx, jnp.where(cur_g[g] == mx, iota + g * per, N_EXPERTS))
        idx = jnp.min(idx, axis=0, keepdims=True)
        for g in range(N_EXPERT_GROUPS):
            hit = (iota + g * per) == idx
            sel_g[g] = jnp.where(hit, 1.0, sel_g[g])
            cur_g[g] = jnp.where(hit, NEG, cur_g[g])

    wsum = jnp.zeros((1, tr), F32)
    for g in range(N_EXPERT_GROUPS):
        wsum = wsum + jnp.sum(sel_g[g] * sc_g[g], axis=0, keepdims=True)
    gates_t = jnp.concatenate([sel_g[g] * sc_g[g] / wsum * ROUTED_SCALE for g in range(N_EXPERT_GROUPS)]
                              + [jnp.zeros((LANES - N_EXPERTS, tr), F32)], axis=0)
    gates_ref[...] = gates_t.T


def _router(logits, router_bias, seq):
    t = logits.shape[0]
    tr = min(1024, seq)
    bias_b = jnp.broadcast_to(router_bias.astype(F32)[:, None], (N_EXPERTS, tr))
    return pl.pallas_call(
        _router_body,
        grid=(t // tr,),
        in_specs=[pl.BlockSpec((tr, LANES), lambda i: (i, 0)),
                  pl.BlockSpec((N_EXPERTS, tr), lambda i: (0, 0))],
        out_specs=pl.BlockSpec((tr, LANES), lambda i: (i, 0)),
        out_shape=jax.ShapeDtypeStruct((t, LANES), F32),
        compiler_params=_params(("arbitrary",)),
        name="router",
    )(logits, bias_b)


def _moe_body(h_ref, gates_ref, x1_ref, g2_ref, wg_ref, wu_ref, wd_ref, sg_ref, su_ref, sd_ref,
              o_ref, acc_scr, *, epb):
    e = pl.program_id(1)
    h = h_ref[...]

    def ffn(wg, wu, wd, gate):
        a = _dot(h, wg.astype(BF16))
        b = _dot(h, wu.astype(BF16))
        act = a * _sigmoid(a) * b
        if gate is not None:
            act = act * gate
        return _dot(act.astype(BF16), wd.astype(BF16))

    @pl.when(e == 0)
    def _():
        acc_scr[...] = ffn(sg_ref[...], su_ref[...], sd_ref[...], None)

    gates = gates_ref[...]
    lane = lax.broadcasted_iota(jnp.int32, gates.shape, 1)
    tot = acc_scr[...]
    for q in range(epb):
        gcol = jnp.sum(jnp.where(lane == e * epb + q, gates, 0.0), axis=-1, keepdims=True)
        tot = tot + ffn(wg_ref[q], wu_ref[q], wd_ref[q], gcol)
    acc_scr[...] = tot

    @pl.when(e == pl.num_programs(1) - 1)
    def _():
        o_ref[...] = x1_ref[...] + g2_ref[0] * acc_scr[...]


def _moe(h2, gates, x1, g2, w_eg, w_eu, w_ed, w_sg, w_su, w_sd, seq):
    t, d = x1.shape
    tm = min(1024, seq)
    per_b = seq // tm
    epb = 2
    f = EXPERT_DIM
    const = lambda i, e: (0, 0)
    return pl.pallas_call(
        functools.partial(_moe_body, epb=epb),
        grid=(t // tm, N_EXPERTS // epb),
        in_specs=[pl.BlockSpec((tm, d), lambda i, e: (i, 0)),
                  pl.BlockSpec((tm, LANES), lambda i, e: (i, 0)),
                  pl.BlockSpec((tm, d), lambda i, e: (i, 0)),
                  pl.BlockSpec((1, 1, d), lambda i, e: (i // per_b, 0, 0)),
                  pl.BlockSpec((epb, d, f), lambda i, e: (e, 0, 0)),
                  pl.BlockSpec((epb, d, f), lambda i, e: (e, 0, 0)),
                  pl.BlockSpec((epb, f, d), lambda i, e: (e, 0, 0)),
                  pl.BlockSpec((d, f), const),
                  pl.BlockSpec((d, f), const),
                  pl.BlockSpec((f, d), const)],
        out_specs=pl.BlockSpec((tm, d), lambda i, e: (i, 0)),
        out_shape=jax.ShapeDtypeStruct((t, d), F32),
        scratch_shapes=[pltpu.VMEM((tm, d), F32)],
        compiler_params=_params(("arbitrary", "arbitrary")),
        name="moe",
    )(h2, gates, x1, g2, w_eg, w_eu, w_ed, w_sg, w_su, w_sd)


def kernel(x, c, positions, w_ada, b_ada, norm1_g, w_in, q_norm_g, k_norm_g, lambda_qk, subln_g, conv_w, conv_b, dt_bias, a_log, d_skip, ssd_norm_g, w_branch_a, w_branch_b, w_out, norm2_g, w_router, router_bias, w_exp_gate, w_exp_up, w_exp_down, w_sh_gate, w_sh_up, w_sh_down):
    batch, seq, d = x.shape
    t = batch * seq
    tq = min(1024, seq)
    tk = tq // 2
    xf = x.reshape(t, d)
    for layer in range(w_ada.shape[0]):
        lambda_init = 0.8 - 0.6 * math.exp(-0.3 * layer)
        c_pad = jnp.pad(c.astype(F32), ((0, 8 - batch), (0, 0)))
        mod = _ada(c_pad, w_ada[layer], b_ada[layer])[:batch].reshape(batch, 6, 1, d)
        sh1, sc1, g1, sh2, sc2, g2 = (mod[:, n] for n in range(6))

        w_l = w_in[layer]
        w_main = w_l[:, :MAIN_COLS].astype(BF16)
        wdt = jnp.pad(w_l[:, MAIN_COLS:], ((0, 0), (0, LANES - SSD_HEADS)))
        wdt_hi, wdt_lo = _split2(wdt)
        proj, dt = _inproj(xf, norm1_g[layer], sc1, sh1, w_main, wdt_hi, wdt_lo, seq)

        qt, kk, vt = _qkvprep(proj, positions, q_norm_g[layer], k_norm_g[layer], batch, seq, tk)
        ya = _attention(qt, kk, vt, lambda_qk[layer], subln_g[layer], lambda_init, batch, seq, tq, tk)
        yb = _ssd(proj, dt, conv_w[layer], conv_b[layer], dt_bias[layer], a_log[layer], d_skip[layer],
                  ssd_norm_g[layer], batch, seq)

        wr = jnp.pad(w_router[layer], ((0, 0), (0, LANES - N_EXPERTS))).astype(BF16)
        x1, h2, logits = _outproj(ya, yb, proj, xf, g1, sc2, sh2, norm2_g[layer],
                                  w_branch_a[layer].astype(BF16), w_branch_b[layer].astype(BF16),
                                  w_out[layer].astype(BF16), wr, seq)
        gates = _router(logits, router_bias[layer], seq)
        xf = _moe(h2, gates, x1, g2, w_exp_gate[layer], w_exp_up[layer], w_exp_down[layer],
                  w_sh_gate[layer], w_sh_up[layer], w_sh_down[layer], seq)
    return xf.reshape(batch, seq, d)
```

```python
import functools
import math

import jax
import jax.numpy as jnp
from jax import lax
from jax.experimental import pallas as pl
from jax.experimental.pallas import tpu as pltpu

F32 = jnp.float32
BF16 = jnp.bfloat16

D_MODEL = 1024
ATTN_HEADS = 8
ATTN_HEAD_DIM = 64
ATTN_V_DIM = 128
ROPE_THETA = 10000.0
SSD_D_INNER = 2048
SSD_HEAD_DIM = 64
SSD_HEADS = 32
SSD_GROUPS = 4
SSD_D_STATE = 128
SSD_CONV = 4
SSD_CHUNK = 128
SSD_CONV_DIM = 3072
N_EXPERTS = 64
TOP_K = 8
N_EXPERT_GROUPS = 8
TOPK_GROUPS = 4
EXPERT_DIM = 256
ROUTED_SCALE = 2.5
EPS = 1e-6
NEG = -1e30
Q_SCALE = ATTN_HEAD_DIM ** -0.5 * math.log2(math.e)

LANES = 128
MAIN_COLS = 10240
VMEM_LIMIT = 56 * 1024 * 1024


def _dot(a, b):
    return jnp.dot(a, b, preferred_element_type=F32)


def _sigmoid(x):
    return 1.0 / (1.0 + jnp.exp(-x))


def _split2(a):
    hi = a.astype(BF16)
    lo = (a - hi.astype(F32)).astype(BF16)
    return hi, lo


def _split3(a):
    hi = a.astype(BF16)
    r = a - hi.astype(F32)
    mid = r.astype(BF16)
    lo = (r - mid.astype(F32)).astype(BF16)
    return hi, mid, lo


def _params(sem, vmem=VMEM_LIMIT):
    return pltpu.CompilerParams(dimension_semantics=sem, vmem_limit_bytes=vmem)


def _ada_body(c_ref, w_ref, b_ref, o_ref):
    c = c_ref[...]
    sc = c * _sigmoid(c)
    chi, clo = _split2(sc)
    whi, wlo = _split2(w_ref[...])
    o_ref[...] = _dot(chi, whi) + _dot(chi, wlo) + _dot(clo, whi) + b_ref[...]


def _ada(c_pad, w, b):
    d, n = w.shape
    tn = 1024
    return pl.pallas_call(
        _ada_body,
        grid=(n // tn,),
        in_specs=[pl.BlockSpec((8, d), lambda j: (0, 0)),
                  pl.BlockSpec((d, tn), lambda j: (0, j)),
                  pl.BlockSpec((1, tn), lambda j: (0, j))],
        out_specs=pl.BlockSpec((8, tn), lambda j: (0, j)),
        out_shape=jax.ShapeDtypeStruct((8, n), F32),
        compiler_params=_params(("arbitrary",)),
        name="ada",
    )(c_pad, w, b.reshape(1, n))


def _inproj_body(x_ref, g_ref, sc_ref, sh_ref, w_ref, wdh_ref, wdl_ref, o_ref, dt_ref, h_scr):
    @pl.when(pl.program_id(1) == 0)
    def _():
        x = x_ref[...]
        ms = jnp.mean(x * x, axis=-1, keepdims=True)
        h = x * lax.rsqrt(ms + EPS) * g_ref[...]
        h = h * (1.0 + sc_ref[0]) + sh_ref[0]
        hb, hl = _split2(h)
        h_scr[...] = hb
        dt_ref[...] = _dot(hb, wdh_ref[...]) + _dot(hb, wdl_ref[...]) + _dot(hl, wdh_ref[...])

    o_ref[...] = _dot(h_scr[...], w_ref[...]).astype(BF16)


def _inproj(x2d, g, sc, sh, w_main, wdt_hi, wdt_lo, seq):
    t, d = x2d.shape
    tm = min(1024, seq)
    tn = 1024
    per_b = seq // tm
    return pl.pallas_call(
        _inproj_body,
        grid=(t // tm, MAIN_COLS // tn),
        in_specs=[pl.BlockSpec((tm, d), lambda i, j: (i, 0)),
                  pl.BlockSpec((1, d), lambda i, j: (0, 0)),
                  pl.BlockSpec((1, 1, d), lambda i, j: (i // per_b, 0, 0)),
                  pl.BlockSpec((1, 1, d), lambda i, j: (i // per_b, 0, 0)),
                  pl.BlockSpec((d, tn), lambda i, j: (0, j)),
                  pl.BlockSpec((d, LANES), lambda i, j: (0, 0)),
                  pl.BlockSpec((d, LANES), lambda i, j: (0, 0))],
        out_specs=[pl.BlockSpec((tm, tn), lambda i, j: (i, j)),
                   pl.BlockSpec((tm, LANES), lambda i, j: (i, 0))],
        out_shape=[jax.ShapeDtypeStruct((t, MAIN_COLS), BF16),
                   jax.ShapeDtypeStruct((t, LANES), F32)],
        scratch_shapes=[pltpu.VMEM((tm, d), BF16)],
        compiler_params=_params(("arbitrary", "arbitrary")),
        name="inproj",
    )(x2d, g.reshape(1, d), sc, sh, w_main, wdt_hi, wdt_lo)


def _qkvprep_body(q_ref, k_ref, v_ref, pos_ref, invf_ref, gq_ref, gk_ref, qt_ref, ko_ref, vt_ref, *, tk):
    ts = q_ref.shape[0]
    half = ATTN_HEAD_DIM // 2
    ang = pos_ref[0].astype(F32) * invf_ref[...]
    cos = jnp.cos(ang)
    sin = jnp.sin(ang)
    zeros = jnp.zeros((ATTN_HEAD_DIM, ts), F32)

    def norm_rope(blk, g):
        ms = jnp.mean(blk * blk, axis=0, keepdims=True)
        r = blk * lax.rsqrt(ms + EPS) * g
        x1 = r[:half]
        x2 = r[half:]
        return jnp.concatenate([x1 * cos - x2 * sin, x2 * cos + x1 * sin], axis=0)

    for h in range(ATTN_HEADS):
        sl = slice(h * LANES, (h + 1) * LANES)
        qht = q_ref[:, sl].astype(F32).T
        q1 = norm_rope(qht[:ATTN_HEAD_DIM], gq_ref[...]) * Q_SCALE
        q2 = norm_rope(qht[ATTN_HEAD_DIM:], gq_ref[...]) * Q_SCALE
        qt_ref[0, 2 * h] = jnp.concatenate([q1, zeros], axis=0).astype(BF16)
        qt_ref[0, 2 * h + 1] = jnp.concatenate([zeros, q2], axis=0).astype(BF16)
        kht = k_ref[:, sl].astype(F32).T
        k1 = norm_rope(kht[:ATTN_HEAD_DIM], gk_ref[...])
        k2 = norm_rope(kht[ATTN_HEAD_DIM:], gk_ref[...])
        ko_ref[0, h] = jnp.concatenate([k1, k2], axis=0).T.astype(BF16)
        vht = v_ref[:, sl].astype(F32).T.astype(BF16)
        for cc in range(ts // tk):
            vt_ref[0, h, cc] = vht[:, cc * tk:(cc + 1) * tk]


def _qkvprep(proj, positions, q_norm_g, k_norm_g, batch, seq, tk):
    ts = min(512, seq)
    ns = seq // ts
    half = ATTN_HEAD_DIM // 2
    inv_freq = ROPE_THETA ** (-jnp.arange(half, dtype=F32) * 2.0 / ATTN_HEAD_DIM)
    invf_b = jnp.broadcast_to(inv_freq[:, None], (half, ts))
    gq_b = jnp.broadcast_to(q_norm_g.astype(F32)[:, None], (ATTN_HEAD_DIM, ts))
    gk_b = jnp.broadcast_to(k_norm_g.astype(F32)[:, None], (ATTN_HEAD_DIM, ts))
    pos3 = positions.reshape(batch, 1, seq)
    const = lambda b, i: (0, 0)
    return pl.pallas_call(
        functools.partial(_qkvprep_body, tk=tk),
        grid=(batch, ns),
        in_specs=[pl.BlockSpec((ts, D_MODEL), lambda b, i: (b * ns + i, 0)),
                  pl.BlockSpec((ts, D_MODEL), lambda b, i: (b * ns + i, 1)),
                  pl.BlockSpec((ts, D_MODEL), lambda b, i: (b * ns + i, 2)),
                  pl.BlockSpec((1, 1, ts), lambda b, i: (b, 0, i)),
                  pl.BlockSpec((half, ts), const),
                  pl.BlockSpec((ATTN_HEAD_DIM, ts), const),
                  pl.BlockSpec((ATTN_HEAD_DIM, ts), const)],
        out_specs=[pl.BlockSpec((1, 2 * ATTN_HEADS, LANES, ts), lambda b, i: (b, 0, 0, i)),
                   pl.BlockSpec((1, ATTN_HEADS, ts, LANES), lambda b, i: (b, 0, i, 0)),
                   pl.BlockSpec((1, ATTN_HEADS, ts // tk, LANES, tk), lambda b, i: (b, 0, i, 0, 0))],
        out_shape=[jax.ShapeDtypeStruct((batch, 2 * ATTN_HEADS, LANES, seq), BF16),
                   jax.ShapeDtypeStruct((batch, ATTN_HEADS, seq, LANES), BF16),
                   jax.ShapeDtypeStruct((batch, ATTN_HEADS, seq // tk, LANES, tk), BF16)],
        compiler_params=_params(("arbitrary", "arbitrary")),
        name="qkvprep",
    )(proj, proj, proj, pos3, invf_b, gq_b, gk_b)


def _attn_body(lam_ref, sg_ref, qt_ref, k_ref, vt_ref, o_ref, acc_scr, sa_scr, sb_scr, *, tq, tk, lambda_init):
    i = pl.program_id(2)
    q_maps = (qt_ref[0, 0], qt_ref[0, 1])
    acc_scr[...] = jnp.zeros(acc_scr.shape, F32)
    row = lax.broadcasted_iota(jnp.int32, (tk, tq), 0)
    colq = lax.broadcasted_iota(jnp.int32, (tk, tq), 1)

    def qk(j, s_scr, lo=0):
        kb = k_ref[0, 0, pl.ds(pl.multiple_of(j * tk, tk), tk), :]
        for mi in range(2):
            s_scr[mi, :, lo:] = _dot(kb, q_maps[mi][:, lo:])

    def process(j, s_scr, carry, diag=None, lo=0):
        vb = vt_ref[0, 0, j]
        out = []
        for mi in range(2):
            m_old, l_old = carry[2 * mi], carry[2 * mi + 1]
            s = s_scr[mi]
            if diag is not None:
                s = jnp.where(row + diag * tk <= colq, s, NEG)
            m_new = jnp.maximum(m_old, jnp.max(s, axis=0, keepdims=True))
            alpha = jnp.exp2(m_old - m_new)
            p = jnp.exp2(s - m_new)
            l_new = alpha * l_old + jnp.sum(p, axis=0, keepdims=True)
            alpha_b = jnp.broadcast_to(alpha, (ATTN_V_DIM, tq))
            acc_scr[mi, :, lo:] = alpha_b[:, lo:] * acc_scr[mi, :, lo:] + _dot(vb, p[:, lo:].astype(BF16))
            out += [m_new, l_new]
        return tuple(out)

    def pair(p, carry):
        qk(2 * p + 1, sb_scr)
        carry = process(2 * p, sa_scr, carry)
        qk(2 * p + 2, sa_scr)
        return process(2 * p + 1, sb_scr, carry)

    @pl.when(i == 0)
    def _():
        sb_scr[:, :, :tk] = jnp.zeros((2, tk, tk), F32)

    init = (jnp.full((1, tq), NEG, F32), jnp.zeros((1, tq), F32)) * 2
    qk(0, sa_scr)
    carry = lax.fori_loop(0, i, pair, init)
    qk(2 * i + 1, sb_scr, lo=tk)
    carry = process(2 * i, sa_scr, carry, diag=0)
    carry = process(2 * i + 1, sb_scr, carry, diag=1, lo=tk)

    lv = lam_ref[...]
    lam = (jnp.exp(jnp.sum(lv[0:1] * lv[1:2], keepdims=True))
           - jnp.exp(jnp.sum(lv[2:3] * lv[3:4], keepdims=True)) + lambda_init)
    o = acc_scr[0] / carry[1] - lam * (acc_scr[1] / carry[3])
    ms = jnp.mean(o * o, axis=0, keepdims=True)
    o = o * lax.rsqrt(ms + EPS) * sg_ref[...] * (1.0 - lambda_init)
    o_ref[...] = o.T.astype(BF16)


def _attention(qt, kk, vt, lambda_qk, subln_g, lambda_init, batch, seq, tq, tk):
    nq = seq // tq
    sg_b = jnp.broadcast_to(subln_g.astype(F32)[:, None], (ATTN_V_DIM, tq))
    return pl.pallas_call(
        functools.partial(_attn_body, tq=tq, tk=tk, lambda_init=lambda_init),
        grid=(batch, ATTN_HEADS, nq),
        in_specs=[pl.BlockSpec((4, ATTN_HEAD_DIM), lambda b, h, i: (0, 0)),
                  pl.BlockSpec((ATTN_V_DIM, tq), lambda b, h, i: (0, 0)),
                  pl.BlockSpec((1, 2, LANES, tq), lambda b, h, i: (b, h, 0, i)),
                  pl.BlockSpec((1, 1, seq, LANES), lambda b, h, i: (b, h, 0, 0)),
                  pl.BlockSpec((1, 1, seq // tk, LANES, tk), lambda b, h, i: (b, h, 0, 0, 0))],
        out_specs=pl.BlockSpec((tq, LANES), lambda b, h, i: (b * nq + i, h)),
        out_shape=jax.ShapeDtypeStruct((batch * seq, ATTN_HEADS * ATTN_V_DIM), BF16),
        scratch_shapes=[pltpu.VMEM((2, ATTN_V_DIM, tq), F32),
                        pltpu.VMEM((2, tk, tq), F32),
                        pltpu.VMEM((2, tk, tq), F32)],
        compiler_params=_params(("arbitrary", "arbitrary", "arbitrary")),
        name="attn",
    )(lambda_qk.astype(F32), sg_b, qt, kk, vt)


def _ssd_body(z0_ref, z1_ref, x0_ref, x1_ref, bc_ref, dt_ref, cw_ref, cb_ref, dtb_ref, alog_ref,
              dsk_ref, ng_ref, tri_ref, y_ref, ext_scr, st_scr):
    L = SSD_CHUNK
    halo = 8
    c = pl.program_id(1)

    @pl.when(c == 0)
    def _():
        ext_scr[0:halo, :] = jnp.zeros((halo, SSD_CONV_DIM), F32)
        st_scr[...] = jnp.zeros(st_scr.shape, F32)

    ext_scr[halo:halo + L, 0:1024] = x0_ref[...].astype(F32)
    ext_scr[halo:halo + L, 1024:2048] = x1_ref[...].astype(F32)
    ext_scr[halo:halo + L, 2048:3072] = bc_ref[...].astype(F32)
    acc = cb_ref[...]
    for k in range(SSD_CONV):
        off = halo - (SSD_CONV - 1) + k
        acc = acc + cw_ref[k:k + 1, :] * ext_scr[off:off + L, :]
    u = acc * _sigmoid(acc)
    ext_scr[0:halo, :] = ext_scr[L:L + halo, :]

    xdt_in = dt_ref[...] + dtb_ref[...]
    dtp = jnp.maximum(xdt_in, 0.0) + jnp.log(1.0 + jnp.exp(-jnp.abs(xdt_in)))
    a = dtp * (-jnp.exp(alog_ref[...]))
    tri = tri_ref[...]
    a_hi, a_mid, a_lo = _split3(a)
    acum = _dot(tri, a_hi) + _dot(tri, a_mid) + _dot(tri, a_lo)
    acum_t = acum.T
    lane = lax.broadcasted_iota(jnp.int32, (L, LANES), 1)
    first = lane < SSD_HEAD_DIM
    tril = lax.broadcasted_iota(jnp.int32, (L, L), 0) >= lax.broadcasted_iota(jnp.int32, (L, L), 1)
    gn = SSD_GROUPS * SSD_D_STATE
    gw = SSD_D_INNER // SSD_GROUPS

    def col(mat, h):
        return jnp.broadcast_to(mat[:, h:h + 1], (L, LANES))

    for g in range(SSD_GROUPS):
        bg = u[:, SSD_D_INNER + g * SSD_D_STATE:SSD_D_INNER + (g + 1) * SSD_D_STATE]
        cg = u[:, SSD_D_INNER + gn + g * SSD_D_STATE:SSD_D_INNER + gn + (g + 1) * SSD_D_STATE].astype(BF16)
        bgt = bg.T.astype(BF16)
        cb = _dot(cg, bgt)
        sprev = st_scr[g]
        yoff = _dot(cg, sprev.astype(BF16))
        y_parts, xdd_parts, dec_parts = [], [], []
        for p in range(4):
            h0 = 8 * g + 2 * p
            c0, c1 = col(acum, h0), col(acum, h0 + 1)
            ac_exp = jnp.where(first, c0, c1)
            dt_exp = jnp.where(first, col(dtp, h0), col(dtp, h0 + 1))
            xsl = u[:, h0 * SSD_HEAD_DIM:h0 * SSD_HEAD_DIM + LANES]
            xdt = xsl * dt_exp
            l0 = jnp.exp(jnp.where(tril, c0 - acum_t[h0:h0 + 1, :], NEG))
            l1 = jnp.exp(jnp.where(tril, c1 - acum_t[h0 + 1:h0 + 2, :], NEG))
            m2 = jnp.concatenate([(cb * l0).astype(BF16), (cb * l1).astype(BF16)], axis=1)
            xb = xdt.astype(BF16)
            zb = jnp.zeros_like(xb)
            rhs = jnp.concatenate([jnp.where(first, xb, zb), jnp.where(first, zb, xb)], axis=0)
            yd = _dot(m2, rhs)
            yo = yoff[:, p * LANES:(p + 1) * LANES] * jnp.exp(ac_exp)
            dsk = dsk_ref[:, h0 * SSD_HEAD_DIM:h0 * SSD_HEAD_DIM + LANES]
            y_parts.append(yd + yo + xsl * dsk)
            last = ac_exp[L - 1:L, :]
            xdd_parts.append((xdt * jnp.exp(last - ac_exp)).astype(BF16))
            dec_parts.append(jnp.exp(last))
        xdd = jnp.concatenate(xdd_parts, axis=1)
        dec = jnp.concatenate(dec_parts, axis=1)
        st_scr[g] = sprev * dec + _dot(bgt, xdd)
        yg = jnp.concatenate(y_parts, axis=1)
        zref = z0_ref if g < 2 else z1_ref
        zg = zref[:, (g % 2) * gw:(g % 2 + 1) * gw].astype(F32)
        yg = yg * (zg * _sigmoid(zg))
        ms = jnp.mean(yg * yg, axis=-1, keepdims=True)
        yg = yg * lax.rsqrt(ms + EPS) * ng_ref[:, g * gw:(g + 1) * gw]
        y_ref[:, g * gw:(g + 1) * gw] = yg.astype(BF16)


def _ssd(proj, dt, conv_w, conv_b, dt_bias, a_log, d_skip, norm_g, batch, seq):
    L = SSD_CHUNK
    nc = seq // L
    pad = LANES - SSD_HEADS
    dtb = jnp.pad(dt_bias.astype(F32), (0, pad)).reshape(1, LANES)
    alog = jnp.pad(a_log.astype(F32), (0, pad)).reshape(1, LANES)
    dsk = jnp.repeat(d_skip.astype(F32), SSD_HEAD_DIM).reshape(1, SSD_D_INNER)
    tri = jnp.tril(jnp.ones((L, L), F32)).astype(BF16)
    row = lambda b, c: (b * nc + c)
    const = lambda b, c: (0, 0)
    return pl.pallas_call(
        _ssd_body,
        grid=(batch, nc),
        in_specs=[pl.BlockSpec((L, 1024), lambda b, c: (row(b, c), 5)),
                  pl.BlockSpec((L, 1024), lambda b, c: (row(b, c), 6)),
                  pl.BlockSpec((L, 1024), lambda b, c: (row(b, c), 7)),
                  pl.BlockSpec((L, 1024), lambda b, c: (row(b, c), 8)),
                  pl.BlockSpec((L, 1024), lambda b, c: (row(b, c), 9)),
                  pl.BlockSpec((L, LANES), lambda b, c: (row(b, c), 0)),
                  pl.BlockSpec((SSD_CONV, SSD_CONV_DIM), const),
                  pl.BlockSpec((1, SSD_CONV_DIM), const),
                  pl.BlockSpec((1, LANES), const),
                  pl.BlockSpec((1, LANES), const),
                  pl.BlockSpec((1, SSD_D_INNER), const),
                  pl.BlockSpec((1, SSD_D_INNER), const),
                  pl.BlockSpec((L, L), const)],
        out_specs=pl.BlockSpec((L, SSD_D_INNER), lambda b, c: (row(b, c), 0)),
        out_shape=jax.ShapeDtypeStruct((batch * seq, SSD_D_INNER), BF16),
        scratch_shapes=[pltpu.VMEM((L + 8, SSD_CONV_DIM), F32),
                        pltpu.VMEM((SSD_GROUPS, SSD_D_STATE, SSD_D_INNER // SSD_GROUPS), F32)],
        compiler_params=_params(("arbitrary", "arbitrary")),
        name="ssd",
    )(proj, proj, proj, proj, proj, dt, conv_w.astype(F32), conv_b.astype(F32).reshape(1, -1), dtb, alog,
      dsk, norm_g.astype(F32).reshape(1, -1), tri)


def _outproj_body(ya_ref, yb_ref, ga_ref, gb_ref, x_ref, g1_ref, sc_ref, sh_ref, n2_ref,
                  wa_ref, wb_ref, wo_ref, wr_ref, x1_ref, h2_ref, lg_ref):
    pa = _dot(ya_ref[...], wa_ref[...])
    pb = _dot(yb_ref[...], wb_ref[...])
    merged = _sigmoid(ga_ref[...].astype(F32)) * pa + _sigmoid(gb_ref[...].astype(F32)) * pb
    x1 = x_ref[...] + g1_ref[0] * _dot(merged.astype(BF16), wo_ref[...])
    x1_ref[...] = x1
    ms = jnp.mean(x1 * x1, axis=-1, keepdims=True)
    h2 = x1 * lax.rsqrt(ms + EPS) * n2_ref[...]
    h2 = h2 * (1.0 + sc_ref[0]) + sh_ref[0]
    h2_ref[...] = h2
    lg_ref[...] = _dot(h2.astype(BF16), wr_ref[...])


def _outproj(ya, yb, proj, x2d, g1, sc2, sh2, norm2_g, wa, wb, wo, wr, seq):
    t, d = x2d.shape
    tm = min(512, seq)
    per_b = seq // tm
    const = lambda i: (0, 0)
    mod = lambda i: (i // per_b, 0, 0)
    return pl.pallas_call(
        _outproj_body,
        grid=(t // tm,),
        in_specs=[pl.BlockSpec((tm, d), lambda i: (i, 0)),
                  pl.BlockSpec((tm, SSD_D_INNER), lambda i: (i, 0)),
                  pl.BlockSpec((tm, d), lambda i: (i, 3)),
                  pl.BlockSpec((tm, d), lambda i: (i, 4)),
                  pl.BlockSpec((tm, d), lambda i: (i, 0)),
                  pl.BlockSpec((1, 1, d), mod),
                  pl.BlockSpec((1, 1, d), mod),
                  pl.BlockSpec((1, 1, d), mod),
                  pl.BlockSpec((1, d), const),
                  pl.BlockSpec((d, d), const),
                  pl.BlockSpec((SSD_D_INNER, d), const),
                  pl.BlockSpec((d, d), const),
                  pl.BlockSpec((d, LANES), const)],
        out_specs=[pl.BlockSpec((tm, d), lambda i: (i, 0)),
                   pl.BlockSpec((tm, d), lambda i: (i, 0)),
                   pl.BlockSpec((tm, LANES), lambda i: (i, 0))],
        out_shape=[jax.ShapeDtypeStruct((t, d), F32),
                   jax.ShapeDtypeStruct((t, d), F32),
                   jax.ShapeDtypeStruct((t, LANES), F32)],
        compiler_params=_params(("arbitrary",)),
        name="outproj",
    )(ya, yb, proj, proj, x2d, g1, sc2, sh2, norm2_g.astype(F32).reshape(1, d), wa, wb, wo, wr)


def _router_body(lg_ref, bias_ref, idx_ref, w_ref):
    tr = lg_ref.shape[0]
    per = N_EXPERTS // N_EXPERT_GROUPS
    lt = lg_ref[...].T
    scores = _sigmoid(lt[:N_EXPERTS])
    choice = scores + bias_ref[...]
    iota = lax.broadcasted_iota(jnp.int32, (per, tr), 0)
    grp = [choice[g * per:(g + 1) * per] for g in range(N_EXPERT_GROUPS)]
    sc_g = [scores[g * per:(g + 1) * per] for g in range(N_EXPERT_GROUPS)]

    gsc = jnp.zeros((N_EXPERT_GROUPS, tr), F32)
    for g in range(N_EXPERT_GROUPS):
        top1 = jnp.max(grp[g], axis=0, keepdims=True)
        idx1 = jnp.min(jnp.where(grp[g] == top1, iota, per), axis=0, keepdims=True)
        top2 = jnp.max(jnp.where(iota == idx1, NEG, grp[g]), axis=0, keepdims=True)
        gsc = jnp.where(iota == g, top1 + top2, gsc)

    gsel = jnp.zeros((N_EXPERT_GROUPS, tr), F32)
    cur = gsc
    for _ in range(TOPK_GROUPS):
        mx = jnp.max(cur, axis=0, keepdims=True)
        idx = jnp.min(jnp.where(cur == mx, iota, N_EXPERT_GROUPS), axis=0, keepdims=True)
        hit = iota == idx
        gsel = jnp.where(hit, 1.0, gsel)
        cur = jnp.where(hit, NEG, cur)

    cur_g = [jnp.where(gsel[g:g + 1] > 0.0, grp[g], NEG) for g in range(N_EXPERT_GROUPS)]
    idx_rows = jnp.zeros((TOP_K, tr), jnp.int32)
    w_rows = jnp.zeros((TOP_K, tr), F32)
    for kk in range(TOP_K):
        mx = cur_g[0]
        for g in range(1, N_EXPERT_GROUPS):
            mx = jnp.maximum(mx, cur_g[g])
        mx = jnp.max(mx, axis=0, keepdims=True)
        idx = jnp.where(cur_g[0] == mx, iota, N_EXPERTS)
        for g in range(1, N_EXPERT_GROUPS):
            idx = jnp.minimum(idx, jnp.where(cur_g[g] == mx, iota + g * per, N_EXPERTS))
        idx = jnp.min(idx, axis=0, keepdims=True)
        wk = jnp.zeros((per, tr), F32)
        for g in range(N_EXPERT_GROUPS):
            hit = (iota + g * per) == idx
            wk = wk + jnp.where(hit, sc_g[g], 0.0)
            cur_g[g] = jnp.where(hit, NEG, cur_g[g])
        idx_rows = jnp.where(iota == kk, idx, idx_rows)
        w_rows = jnp.where(iota == kk, jnp.sum(wk, axis=0, keepdims=True), w_rows)

    wsum = jnp.sum(w_rows, axis=0, keepdims=True)
    idx_ref[...] = idx_rows
    w_ref[...] = w_rows / wsum * ROUTED_SCALE


def _router(logits, router_bias, seq):
    t = logits.shape[0]
    tr = min(1024, seq)
    bias_b = jnp.broadcast_to(router_bias.astype(F32)[:, None], (N_EXPERTS, tr))
    return pl.pallas_call(
        _router_body,
        grid=(t // tr,),
        in_specs=[pl.BlockSpec((tr, LANES), lambda i: (i, 0)),
                  pl.BlockSpec((N_EXPERTS, tr), lambda i: (0, 0))],
        out_specs=[pl.BlockSpec((TOP_K, tr), lambda i: (0, i)),
                   pl.BlockSpec((TOP_K, tr), lambda i: (0, i))],
        out_shape=[jax.ShapeDtypeStruct((TOP_K, t), jnp.int32),
                   jax.ShapeDtypeStruct((TOP_K, t), F32)],
        compiler_params=_params(("arbitrary",)),
        name="router",
    )(logits, bias_b)


def _route_tables(idx_t, w_t, groups, tg, rows, ntiles):
    i32 = jnp.int32
    e = idx_t.reshape(TOP_K, groups, tg).transpose(1, 0, 2).reshape(groups, TOP_K * tg)
    w = w_t.reshape(TOP_K, groups, tg).transpose(1, 0, 2).reshape(groups, TOP_K * tg)
    tl = jnp.broadcast_to(jnp.arange(tg, dtype=i32)[None, None, :], (groups, TOP_K, tg)).reshape(groups, TOP_K * tg)
    skey, sw = lax.sort((e * tg + tl, w), dimension=1, num_keys=1)
    stok = skey % tg
    cnt = jnp.sum((e[:, :, None] == jnp.arange(N_EXPERTS, dtype=i32)[None, None, :]).astype(i32), axis=1)
    off = jnp.cumsum(cnt, axis=1) - cnt
    ntile = (cnt + rows - 1) // rows
    tend = jnp.cumsum(ntile, axis=1)
    tstart = tend - ntile
    n = jnp.arange(ntiles, dtype=i32)
    te = jnp.sum((n[None, :, None] >= tend[:, None, :]).astype(i32), axis=2)
    valid_tile = n[None, :] < tend[:, -1:]
    te = jnp.minimum(te, N_EXPERTS - 1)
    k0 = (n[None, :] - jnp.take_along_axis(tstart, te, axis=1)) * rows
    nvalid = jnp.where(valid_tile, jnp.clip(jnp.take_along_axis(cnt, te, axis=1) - k0, 0, rows), 0)
    r = jnp.arange(rows, dtype=i32)
    j = (jnp.take_along_axis(off, te, axis=1) + k0)[:, :, None] + r[None, None, :]
    rvalid = r[None, None, :] < nvalid[:, :, None]
    jc = jnp.clip(j, 0, TOP_K * tg - 1).reshape(groups, ntiles * rows)
    tok = jnp.take_along_axis(stok, jc, axis=1).reshape(groups, ntiles, rows)
    wv = jnp.take_along_axis(sw, jc, axis=1).reshape(groups, ntiles, rows)
    tok_g = jnp.where(rvalid, tok, 0)
    tok_s = jnp.where(rvalid, tok, tg)
    wv = jnp.where(rvalid, wv, 0.0)
    return te, nvalid, tok_g, tok_s, wv


SUB = 8
MOE_ROWS = 256


def _routed_body(te_ref, nv_ref, tokg_ref, toks_ref, wt_ref, h_ref, wg_ref, wu_ref, wd_ref, acc_ref,
                 xbuf, ybuf):
    g = pl.program_id(0)
    n = pl.program_id(1)
    rows = MOE_ROWS

    @pl.when(n == 0)
    def _():
        acc_ref[...] = jnp.zeros(acc_ref.shape, F32)

    @pl.when(nv_ref[g, n] > 0)
    def _():
        for r in range(rows):
            t = tokg_ref[0, 0, 0, r]
            xbuf[pl.ds(r * SUB, SUB), :] = h_ref[pl.ds(pl.multiple_of(t * SUB, SUB), SUB), :]
        x2d = jnp.concatenate([xbuf[pl.ds(s, rows, stride=SUB), :] for s in range(SUB)], axis=1).astype(BF16)
        a = _dot(x2d, wg_ref[0].astype(BF16))
        b = _dot(x2d, wu_ref[0].astype(BF16))
        act = (a * _sigmoid(a) * b).astype(BF16)
        y = _dot(act, wd_ref[0].astype(BF16))
        for s in range(SUB):
            ybuf[pl.ds(s, rows, stride=SUB), :] = y[:, s * LANES:(s + 1) * LANES]
        for c in range(rows // SUB):
            upd = []
            for r in range(c * SUB, (c + 1) * SUB):
                t8 = pl.multiple_of(toks_ref[0, 0, 0, r] * SUB, SUB)
                upd.append((t8, acc_ref[0, pl.ds(t8, SUB), :] + wt_ref[0, 0, 0, r] * ybuf[pl.ds(r * SUB, SUB), :]))
            for t8, v in upd:
                acc_ref[0, pl.ds(t8, SUB), :] = v


def _routed(h2, tables, w_eg, w_eu, w_ed, groups, tg, ntiles):
    te, nvalid, tok_g, tok_s, wv = tables
    d, f = D_MODEL, EXPERT_DIM
    rows = MOE_ROWS
    hv = h2.reshape(groups * tg * SUB, LANES)
    smem = functools.partial(pl.BlockSpec, (1, 1, 1, rows), lambda g, n, te, nv: (g, n, 0, 0),
                             memory_space=pltpu.SMEM)
    tok_g, tok_s, wv = (a.reshape(groups, ntiles, 1, rows) for a in (tok_g, tok_s, wv))
    grid_spec = pltpu.PrefetchScalarGridSpec(
        num_scalar_prefetch=2,
        grid=(groups, ntiles),
        in_specs=[smem(), smem(), smem(),
                  pl.BlockSpec((tg * SUB, LANES), lambda g, n, te, nv: (g, 0), pipeline_mode=pl.Buffered(1)),
                  pl.BlockSpec((1, d, f), lambda g, n, te, nv: (te[g, n], 0, 0)),
                  pl.BlockSpec((1, d, f), lambda g, n, te, nv: (te[g, n], 0, 0)),
                  pl.BlockSpec((1, f, d), lambda g, n, te, nv: (te[g, n], 0, 0))],
        out_specs=pl.BlockSpec((1, (tg + 1) * SUB, LANES), lambda g, n, te, nv: (g, 0, 0),
                               pipeline_mode=pl.Buffered(1)),
        scratch_shapes=[pltpu.VMEM((rows * SUB, LANES), F32), pltpu.VMEM((rows * SUB, LANES), F32)],
    )
    out = pl.pallas_call(
        _routed_body,
        grid_spec=grid_spec,
        out_shape=jax.ShapeDtypeStruct((groups, (tg + 1) * SUB, LANES), F32),
        compiler_params=_params(("arbitrary", "arbitrary")),
        name="routed",
    )(te, nvalid, tok_g, tok_s, wv, hv, w_eg, w_eu, w_ed)
    return out[:, :tg * SUB].reshape(groups * tg, d)


def _combine_body(h_ref, r_ref, x1_ref, g2_ref, sg_ref, su_ref, sd_ref, o_ref):
    hb = h_ref[...].astype(BF16)
    a = _dot(hb, sg_ref[...])
    b = _dot(hb, su_ref[...])
    shared = _dot((a * _sigmoid(a) * b).astype(BF16), sd_ref[...])
    o_ref[...] = x1_ref[...] + g2_ref[0] * (r_ref[...] + shared)


def _combine(h2, routed, x1, g2, w_sg, w_su, w_sd, seq):
    t, d = x1.shape
    tm = min(1024, seq)
    per_b = seq // tm
    f = w_sg.shape[1]
    const = lambda i: (0, 0)
    row = lambda i: (i, 0)
    return pl.pallas_call(
        _combine_body,
        grid=(t // tm,),
        in_specs=[pl.BlockSpec((tm, d), row),
                  pl.BlockSpec((tm, d), row),
                  pl.BlockSpec((tm, d), row),
                  pl.BlockSpec((1, 1, d), lambda i: (i // per_b, 0, 0)),
                  pl.BlockSpec((d, f), const),
                  pl.BlockSpec((d, f), const),
                  pl.BlockSpec((f, d), const)],
        out_specs=pl.BlockSpec((tm, d), row),
        out_shape=jax.ShapeDtypeStruct((t, d), F32),
        compiler_params=_params(("arbitrary",)),
        name="combine",
    )(h2, routed, x1, g2, w_sg, w_su, w_sd)


def kernel(x, c, positions, w_ada, b_ada, norm1_g, w_in, q_norm_g, k_norm_g, lambda_qk, subln_g, conv_w, conv_b, dt_bias, a_log, d_skip, ssd_norm_g, w_branch_a, w_branch_b, w_out, norm2_g, w_router, router_bias, w_exp_gate, w_exp_up, w_exp_down, w_sh_gate, w_sh_up, w_sh_down):
    batch, seq, d = x.shape
    t = batch * seq
    tq = min(1024, seq)
    tk = tq // 2
    xf = x.reshape(t, d)
    for layer in range(w_ada.shape[0]):
        lambda_init = 0.8 - 0.6 * math.exp(-0.3 * layer)
        c_pad = jnp.pad(c.astype(F32), ((0, 8 - batch), (0, 0)))
        mod = _ada(c_pad, w_ada[layer], b_ada[layer])[:batch].reshape(batch, 6, 1, d)
        sh1, sc1, g1, sh2, sc2, g2 = (mod[:, n] for n in range(6))

        w_l = w_in[layer]
        w_main = w_l[:, :MAIN_COLS].astype(BF16)
        wdt = jnp.pad(w_l[:, MAIN_COLS:], ((0, 0), (0, LANES - SSD_HEADS)))
        wdt_hi, wdt_lo = _split2(wdt)
        proj, dt = _inproj(xf, norm1_g[layer], sc1, sh1, w_main, wdt_hi, wdt_lo, seq)

        qt, kk, vt = _qkvprep(proj, positions, q_norm_g[layer], k_norm_g[layer], batch, seq, tk)
        ya = _attention(qt, kk, vt, lambda_qk[layer], subln_g[layer], lambda_init, batch, seq, tq, tk)
        yb = _ssd(proj, dt, conv_w[layer], conv_b[layer], dt_bias[layer], a_log[layer], d_skip[layer],
                  ssd_norm_g[layer], batch, seq)

        wr = jnp.pad(w_router[layer], ((0, 0), (0, LANES - N_EXPERTS))).astype(BF16)
        x1, h2, logits = _outproj(ya, yb, proj, xf, g1, sc2, sh2, norm2_g[layer],
                                  w_branch_a[layer].astype(BF16), w_branch_b[layer].astype(BF16),
                                  w_out[layer].astype(BF16), wr, seq)
        idx_t, w_t = _router(logits, router_bias[layer], seq)
        tg = min(4096, t)
        groups = t // tg
        ntiles = tg * TOP_K // MOE_ROWS + N_EXPERTS
        tables = _route_tables(idx_t, w_t, groups, tg, MOE_ROWS, ntiles)
        routed = _routed(h2, tables, w_exp_gate[layer], w_exp_up[layer], w_exp_down[layer], groups, tg, ntiles)
        xf = _combine(h2, routed, x1, g2, w_sh_gate[layer].astype(BF16), w_sh_up[layer].astype(BF16),
                      w_sh_down[layer].astype(BF16), seq)
    return xf.reshape(batch, seq, d)
```

```python
import functools
import math

import jax
import jax.numpy as jnp
from jax import lax
from jax.experimental import pallas as pl
from jax.experimental.pallas import tpu as pltpu

F32 = jnp.float32
BF16 = jnp.bfloat16

D_MODEL = 1024
ATTN_HEADS = 8
ATTN_HEAD_DIM = 64
ATTN_V_DIM = 128
ROPE_THETA = 10000.0
SSD_D_INNER = 2048
SSD_HEAD_DIM = 64
SSD_HEADS = 32
SSD_GROUPS = 4
SSD_D_STATE = 128
SSD_CONV = 4
SSD_CHUNK = 128
SSD_CONV_DIM = 3072
N_EXPERTS = 64
TOP_K = 8
N_EXPERT_GROUPS = 8
TOPK_GROUPS = 4
EXPERT_DIM = 256
ROUTED_SCALE = 2.5
EPS = 1e-6
NEG = -1e30
LOG2E = math.log2(math.e)
Q_SCALE = ATTN_HEAD_DIM ** -0.5 * LOG2E
SSD_HALO = 16

LANES = 128
SUB = 8
MOE_ROWS = 256
MAIN_COLS = 10240
VMEM_LIMIT = 56 * 1024 * 1024


def _dot(a, b):
    return jnp.dot(a, b, preferred_element_type=F32)


def _sigmoid(x):
    return 1.0 / (1.0 + jnp.exp(-x))


def _split2(a):
    hi = a.astype(BF16)
    lo = (a - hi.astype(F32)).astype(BF16)
    return hi, lo


def _split3(a):
    hi = a.astype(BF16)
    r = a - hi.astype(F32)
    mid = r.astype(BF16)
    lo = (r - mid.astype(F32)).astype(BF16)
    return hi, mid, lo


def _params(sem, vmem=VMEM_LIMIT):
    return pltpu.CompilerParams(dimension_semantics=sem, vmem_limit_bytes=vmem)


def _ada_body(c_ref, w_ref, b_ref, o_ref):
    c = c_ref[...]
    sc = c * _sigmoid(c)
    chi, clo = _split2(sc)
    whi, wlo = _split2(w_ref[...])
    o_ref[...] = _dot(chi, whi) + _dot(chi, wlo) + _dot(clo, whi) + b_ref[...]


def _ada(c_pad, w, b):
    d, n = w.shape
    tn = 1024
    return pl.pallas_call(
        _ada_body,
        grid=(n // tn,),
        in_specs=[pl.BlockSpec((8, d), lambda j: (0, 0)),
                  pl.BlockSpec((d, tn), lambda j: (0, j)),
                  pl.BlockSpec((1, tn), lambda j: (0, j))],
        out_specs=pl.BlockSpec((8, tn), lambda j: (0, j)),
        out_shape=jax.ShapeDtypeStruct((8, n), F32),
        compiler_params=_params(("arbitrary",)),
        name="ada",
    )(c_pad, w, b.reshape(1, n))


def _inproj_body(x_ref, g_ref, sc_ref, sh_ref, w_ref, wdh_ref, wdl_ref, o_ref, dt_ref, h_scr):
    @pl.when(pl.program_id(1) == 0)
    def _():
        x = x_ref[...]
        ms = jnp.mean(x * x, axis=-1, keepdims=True)
        h = x * lax.rsqrt(ms + EPS) * g_ref[...]
        h = h * (1.0 + sc_ref[0]) + sh_ref[0]
        hb, hl = _split2(h)
        h_scr[...] = hb
        dt_ref[...] = _dot(hb, wdh_ref[...]) + _dot(hb, wdl_ref[...]) + _dot(hl, wdh_ref[...])

    o_ref[...] = _dot(h_scr[...], w_ref[...]).astype(BF16)


def _inproj(x2d, g, sc, sh, w_main, wdt_hi, wdt_lo, seq):
    t, d = x2d.shape
    tm = min(1024, seq)
    tn = 1024
    per_b = seq // tm
    return pl.pallas_call(
        _inproj_body,
        grid=(t // tm, MAIN_COLS // tn),
        in_specs=[pl.BlockSpec((tm, d), lambda i, j: (i, 0)),
                  pl.BlockSpec((1, d), lambda i, j: (0, 0)),
                  pl.BlockSpec((1, 1, d), lambda i, j: (i // per_b, 0, 0)),
                  pl.BlockSpec((1, 1, d), lambda i, j: (i // per_b, 0, 0)),
                  pl.BlockSpec((d, tn), lambda i, j: (0, j)),
                  pl.BlockSpec((d, LANES), lambda i, j: (0, 0)),
                  pl.BlockSpec((d, LANES), lambda i, j: (0, 0))],
        out_specs=[pl.BlockSpec((tm, tn), lambda i, j: (i, j)),
                   pl.BlockSpec((tm, LANES), lambda i, j: (i, 0))],
        out_shape=[jax.ShapeDtypeStruct((t, MAIN_COLS), BF16),
                   jax.ShapeDtypeStruct((t, LANES), F32)],
        scratch_shapes=[pltpu.VMEM((tm, d), BF16)],
        compiler_params=_params(("arbitrary", "arbitrary")),
        name="inproj",
    )(x2d, g.reshape(1, d), sc, sh, w_main, wdt_hi, wdt_lo)


def _qkvprep_body(q_ref, k_ref, v_ref, pos_ref, invf_ref, gq_ref, gk_ref, qt_ref, ko_ref, vt_ref, *, tk):
    ts = q_ref.shape[0]
    half = ATTN_HEAD_DIM // 2
    ang = pos_ref[0].astype(F32) * invf_ref[...]
    cos = jnp.cos(ang)
    sin = jnp.sin(ang)
    zeros = jnp.zeros((ATTN_HEAD_DIM, ts), F32)

    def norm_rope(blk, g):
        ms = jnp.mean(blk * blk, axis=0, keepdims=True)
        r = blk * lax.rsqrt(ms + EPS) * g
        x1 = r[:half]
        x2 = r[half:]
        return jnp.concatenate([x1 * cos - x2 * sin, x2 * cos + x1 * sin], axis=0)

    for h in range(ATTN_HEADS):
        sl = slice(h * LANES, (h + 1) * LANES)
        qht = q_ref[:, sl].astype(F32).T
        q1 = norm_rope(qht[:ATTN_HEAD_DIM], gq_ref[...]) * Q_SCALE
        q2 = norm_rope(qht[ATTN_HEAD_DIM:], gq_ref[...]) * Q_SCALE
        qt_ref[0, 2 * h] = jnp.concatenate([q1, zeros], axis=0).astype(BF16)
        qt_ref[0, 2 * h + 1] = jnp.concatenate([zeros, q2], axis=0).astype(BF16)
        kht = k_ref[:, sl].astype(F32).T
        k1 = norm_rope(kht[:ATTN_HEAD_DIM], gk_ref[...])
        k2 = norm_rope(kht[ATTN_HEAD_DIM:], gk_ref[...])
        ko_ref[0, h] = jnp.concatenate([k1, k2], axis=0).T.astype(BF16)
        vht = v_ref[:, sl].astype(F32).T.astype(BF16)
        for cc in range(ts // tk):
            vt_ref[0, h, cc] = vht[:, cc * tk:(cc + 1) * tk]


def _qkvprep(proj, positions, q_norm_g, k_norm_g, batch, seq, tk):
    ts = min(512, seq)
    ns = seq // ts
    half = ATTN_HEAD_DIM // 2
    inv_freq = ROPE_THETA ** (-jnp.arange(half, dtype=F32) * 2.0 / ATTN_HEAD_DIM)
    invf_b = jnp.broadcast_to(inv_freq[:, None], (half, ts))
    gq_b = jnp.broadcast_to(q_norm_g.astype(F32)[:, None], (ATTN_HEAD_DIM, ts))
    gk_b = jnp.broadcast_to(k_norm_g.astype(F32)[:, None], (ATTN_HEAD_DIM, ts))
    pos3 = positions.reshape(batch, 1, seq)
    const = lambda b, i: (0, 0)
    return pl.pallas_call(
        functools.partial(_qkvprep_body, tk=tk),
        grid=(batch, ns),
        in_specs=[pl.BlockSpec((ts, D_MODEL), lambda b, i: (b * ns + i, 0)),
                  pl.BlockSpec((ts, D_MODEL), lambda b, i: (b * ns + i, 1)),
                  pl.BlockSpec((ts, D_MODEL), lambda b, i: (b * ns + i, 2)),
                  pl.BlockSpec((1, 1, ts), lambda b, i: (b, 0, i)),
                  pl.BlockSpec((half, ts), const),
                  pl.BlockSpec((ATTN_HEAD_DIM, ts), const),
                  pl.BlockSpec((ATTN_HEAD_DIM, ts), const)],
        out_specs=[pl.BlockSpec((1, 2 * ATTN_HEADS, LANES, ts), lambda b, i: (b, 0, 0, i)),
                   pl.BlockSpec((1, ATTN_HEADS, ts, LANES), lambda b, i: (b, 0, i, 0)),
                   pl.BlockSpec((1, ATTN_HEADS, ts // tk, LANES, tk), lambda b, i: (b, 0, i, 0, 0))],
        out_shape=[jax.ShapeDtypeStruct((batch, 2 * ATTN_HEADS, LANES, seq), BF16),
                   jax.ShapeDtypeStruct((batch, ATTN_HEADS, seq, LANES), BF16),
                   jax.ShapeDtypeStruct((batch, ATTN_HEADS, seq // tk, LANES, tk), BF16)],
        compiler_params=_params(("arbitrary", "arbitrary")),
        name="qkvprep",
    )(proj, proj, proj, pos3, invf_b, gq_b, gk_b)


def _attn_body(lam_ref, sg_ref, qt_ref, k_ref, vt_ref, o_ref, acc_scr, sa_scr, sb_scr, *, tq, tk, lambda_init):
    i = pl.program_id(2)
    q_maps = (qt_ref[0, 0], qt_ref[0, 1])
    acc_scr[...] = jnp.zeros(acc_scr.shape, F32)
    row = lax.broadcasted_iota(jnp.int32, (tk, tq), 0)
    colq = lax.broadcasted_iota(jnp.int32, (tk, tq), 1)

    def qk(j, s_scr, lo=0):
        kb = k_ref[0, 0, pl.ds(pl.multiple_of(j * tk, tk), tk), :]
        for mi in range(2):
            s_scr[mi, :, lo:] = _dot(kb, q_maps[mi][:, lo:])

    def process(j, s_scr, carry, diag=None, lo=0):
        vb = vt_ref[0, 0, j]
        out = []
        for mi in range(2):
            m_old, l_old = carry[2 * mi], carry[2 * mi + 1]
            s = s_scr[mi]
            if diag is not None:
                s = jnp.where(row + diag * tk <= colq, s, NEG)
            m_new = jnp.maximum(m_old, jnp.max(s, axis=0, keepdims=True))
            alpha = jnp.exp2(m_old - m_new)
            p = jnp.exp2(s - m_new)
            l_new = alpha * l_old + jnp.sum(p, axis=0, keepdims=True)
            alpha_b = jnp.broadcast_to(alpha, (ATTN_V_DIM, tq))
            acc_scr[mi, :, lo:] = alpha_b[:, lo:] * acc_scr[mi, :, lo:] + _dot(vb, p[:, lo:].astype(BF16))
            out += [m_new, l_new]
        return tuple(out)

    def pair(p, carry):
        qk(2 * p + 1, sb_scr)
        carry = process(2 * p, sa_scr, carry)
        qk(2 * p + 2, sa_scr)
        return process(2 * p + 1, sb_scr, carry)

    @pl.when(i == 0)
    def _():
        sb_scr[:, :, :tk] = jnp.zeros((2, tk, tk), F32)

    init = (jnp.full((1, tq), NEG, F32), jnp.zeros((1, tq), F32)) * 2
    qk(0, sa_scr)
    carry = lax.fori_loop(0, i, pair, init)
    qk(2 * i + 1, sb_scr, lo=tk)
    carry = process(2 * i, sa_scr, carry, diag=0)
    carry = process(2 * i + 1, sb_scr, carry, diag=1, lo=tk)

    lv = lam_ref[...]
    lam = (jnp.exp(jnp.sum(lv[0:1] * lv[1:2], keepdims=True))
           - jnp.exp(jnp.sum(lv[2:3] * lv[3:4], keepdims=True)) + lambda_init)
    o = acc_scr[0] / carry[1] - lam * (acc_scr[1] / carry[3])
    ms = jnp.mean(o * o, axis=0, keepdims=True)
    o = o * lax.rsqrt(ms + EPS) * sg_ref[...] * (1.0 - lambda_init)
    o_ref[...] = o.T.astype(BF16)


def _attention(qt, kk, vt, lambda_qk, subln_g, lambda_init, batch, seq, tq, tk):
    nq = seq // tq
    sg_b = jnp.broadcast_to(subln_g.astype(F32)[:, None], (ATTN_V_DIM, tq))
    return pl.pallas_call(
        functools.partial(_attn_body, tq=tq, tk=tk, lambda_init=lambda_init),
        grid=(batch, ATTN_HEADS, nq),
        in_specs=[pl.BlockSpec((4, ATTN_HEAD_DIM), lambda b, h, i: (0, 0)),
                  pl.BlockSpec((ATTN_V_DIM, tq), lambda b, h, i: (0, 0)),
                  pl.BlockSpec((1, 2, LANES, tq), lambda b, h, i: (b, h, 0, i)),
                  pl.BlockSpec((1, 1, seq, LANES), lambda b, h, i: (b, h, 0, 0)),
                  pl.BlockSpec((1, 1, seq // tk, LANES, tk), lambda b, h, i: (b, h, 0, 0, 0))],
        out_specs=pl.BlockSpec((tq, LANES), lambda b, h, i: (b * nq + i, h)),
        out_shape=jax.ShapeDtypeStruct((batch * seq, ATTN_HEADS * ATTN_V_DIM), BF16),
        scratch_shapes=[pltpu.VMEM((2, ATTN_V_DIM, tq), F32),
                        pltpu.VMEM((2, tk, tq), F32),
                        pltpu.VMEM((2, tk, tq), F32)],
        compiler_params=_params(("arbitrary", "arbitrary", "arbitrary")),
        name="attn",
    )(lambda_qk.astype(F32), sg_b, qt, kk, vt)


def _ssd_body(z0_ref, z1_ref, x0_ref, x1_ref, bc_ref, dt_ref, cw_ref, cb_ref, dtb_ref, alog_ref,
              dsk_ref, ng_ref, tri_ref, shift_ref, y_ref, ext_scr, st_scr):
    L = SSD_CHUNK
    halo = SSD_HALO
    c = pl.program_id(1)

    @pl.when(c == 0)
    def _():
        ext_scr[0:halo, :] = jnp.zeros((halo, SSD_CONV_DIM), BF16)
        st_scr[...] = jnp.zeros(st_scr.shape, F32)

    ext_scr[halo:halo + L, 0:1024] = x0_ref[...]
    ext_scr[halo:halo + L, 1024:2048] = x1_ref[...]
    ext_scr[halo:halo + L, 2048:3072] = bc_ref[...]
    sh = _dot(shift_ref[...], ext_scr[...])
    acc = cb_ref[...] + cw_ref[SSD_CONV - 1:SSD_CONV, :] * ext_scr[halo:halo + L, :].astype(F32)
    for k in range(SSD_CONV - 1):
        acc = acc + cw_ref[k:k + 1, :] * sh[k * L:(k + 1) * L]
    u = acc * _sigmoid(acc)
    ext_scr[0:halo, :] = ext_scr[L:L + halo, :]

    xdt_in = dt_ref[...] + dtb_ref[...]
    dtp = jnp.maximum(xdt_in, 0.0) + jnp.log(1.0 + jnp.exp(-jnp.abs(xdt_in)))
    a = dtp * (-jnp.exp(alog_ref[...]) * LOG2E)
    tri = tri_ref[...]
    a_hi, a_mid, a_lo = _split3(a)
    acum = _dot(tri, a_hi) + _dot(tri, a_mid) + _dot(tri, a_lo)
    acum_t = acum.T
    lane = lax.broadcasted_iota(jnp.int32, (L, LANES), 1)
    first = lane < SSD_HEAD_DIM
    tril = lax.broadcasted_iota(jnp.int32, (L, L), 0) >= lax.broadcasted_iota(jnp.int32, (L, L), 1)
    gn = SSD_GROUPS * SSD_D_STATE
    gw = SSD_D_INNER // SSD_GROUPS

    def col(mat, h):
        return jnp.broadcast_to(mat[:, h:h + 1], (L, LANES))

    for g in range(SSD_GROUPS):
        bg = u[:, SSD_D_INNER + g * SSD_D_STATE:SSD_D_INNER + (g + 1) * SSD_D_STATE]
        cg = u[:, SSD_D_INNER + gn + g * SSD_D_STATE:SSD_D_INNER + gn + (g + 1) * SSD_D_STATE].astype(BF16)
        bgt = bg.T.astype(BF16)
        cb = _dot(cg, bgt)
        sprev = st_scr[g]
        yoff = _dot(cg, sprev.astype(BF16))
        y_parts, xdd_parts, dec_parts = [], [], []
        for p in range(4):
            h0 = 8 * g + 2 * p
            c0, c1 = col(acum, h0), col(acum, h0 + 1)
            ac_exp = jnp.where(first, c0, c1)
            dt_exp = jnp.where(first, col(dtp, h0), col(dtp, h0 + 1))
            xsl = u[:, h0 * SSD_HEAD_DIM:h0 * SSD_HEAD_DIM + LANES]
            xdt = xsl * dt_exp
            l0 = jnp.exp2(jnp.where(tril, c0 - acum_t[h0:h0 + 1, :], NEG))
            l1 = jnp.exp2(jnp.where(tril, c1 - acum_t[h0 + 1:h0 + 2, :], NEG))
            m2 = jnp.concatenate([(cb * l0).astype(BF16), (cb * l1).astype(BF16)], axis=1)
            xb = xdt.astype(BF16)
            zb = jnp.zeros_like(xb)
            rhs = jnp.concatenate([jnp.where(first, xb, zb), jnp.where(first, zb, xb)], axis=0)
            yd = _dot(m2, rhs)
            yo = yoff[:, p * LANES:(p + 1) * LANES] * jnp.exp2(ac_exp)
            dsk = dsk_ref[:, h0 * SSD_HEAD_DIM:h0 * SSD_HEAD_DIM + LANES]
            y_parts.append(yd + yo + xsl * dsk)
            last = ac_exp[L - 1:L, :]
            xdd_parts.append((xdt * jnp.exp2(last - ac_exp)).astype(BF16))
            dec_parts.append(jnp.exp2(last))
        xdd = jnp.concatenate(xdd_parts, axis=1)
        dec = jnp.concatenate(dec_parts, axis=1)
        st_scr[g] = sprev * dec + _dot(bgt, xdd)
        yg = jnp.concatenate(y_parts, axis=1)
        zref = z0_ref if g < 2 else z1_ref
        zg = zref[:, (g % 2) * gw:(g % 2 + 1) * gw].astype(F32)
        yg = yg * (zg * _sigmoid(zg))
        ms = jnp.mean(yg * yg, axis=-1, keepdims=True)
        yg = yg * lax.rsqrt(ms + EPS) * ng_ref[:, g * gw:(g + 1) * gw]
        y_ref[:, g * gw:(g + 1) * gw] = yg.astype(BF16)


def _ssd(proj, dt, conv_w, conv_b, dt_bias, a_log, d_skip, norm_g, batch, seq):
    L = SSD_CHUNK
    nc = seq // L
    pad = LANES - SSD_HEADS
    dtb = jnp.pad(dt_bias.astype(F32), (0, pad)).reshape(1, LANES)
    alog = jnp.pad(a_log.astype(F32), (0, pad)).reshape(1, LANES)
    dsk = jnp.repeat(d_skip.astype(F32), SSD_HEAD_DIM).reshape(1, SSD_D_INNER)
    tri = jnp.tril(jnp.ones((L, L), F32)).astype(BF16)
    tt = jnp.arange(L)
    shift = jnp.concatenate(
        [jax.nn.one_hot(SSD_HALO + tt - (SSD_CONV - 1) + k, L + SSD_HALO, dtype=BF16) for k in range(SSD_CONV - 1)],
        axis=0)
    row = lambda b, c: (b * nc + c)
    const = lambda b, c: (0, 0)
    return pl.pallas_call(
        _ssd_body,
        grid=(batch, nc),
        in_specs=[pl.BlockSpec((L, 1024), lambda b, c: (row(b, c), 5)),
                  pl.BlockSpec((L, 1024), lambda b, c: (row(b, c), 6)),
                  pl.BlockSpec((L, 1024), lambda b, c: (row(b, c), 7)),
                  pl.BlockSpec((L, 1024), lambda b, c: (row(b, c), 8)),
                  pl.BlockSpec((L, 1024), lambda b, c: (row(b, c), 9)),
                  pl.BlockSpec((L, LANES), lambda b, c: (row(b, c), 0)),
                  pl.BlockSpec((SSD_CONV, SSD_CONV_DIM), const),
                  pl.BlockSpec((1, SSD_CONV_DIM), const),
                  pl.BlockSpec((1, LANES), const),
                  pl.BlockSpec((1, LANES), const),
                  pl.BlockSpec((1, SSD_D_INNER), const),
                  pl.BlockSpec((1, SSD_D_INNER), const),
                  pl.BlockSpec((L, L), const),
                  pl.BlockSpec(((SSD_CONV - 1) * L, L + SSD_HALO), const)],
        out_specs=pl.BlockSpec((L, SSD_D_INNER), lambda b, c: (row(b, c), 0)),
        out_shape=jax.ShapeDtypeStruct((batch * seq, SSD_D_INNER), BF16),
        scratch_shapes=[pltpu.VMEM((L + SSD_HALO, SSD_CONV_DIM), BF16),
                        pltpu.VMEM((SSD_GROUPS, SSD_D_STATE, SSD_D_INNER // SSD_GROUPS), F32)],
        compiler_params=_params(("arbitrary", "arbitrary")),
        name="ssd",
    )(proj, proj, proj, proj, proj, dt, conv_w.astype(F32), conv_b.astype(F32).reshape(1, -1), dtb, alog,
      dsk, norm_g.astype(F32).reshape(1, -1), tri, shift)


def _outproj_body(ya_ref, yb_ref, ga_ref, gb_ref, x_ref, g1_ref, sc_ref, sh_ref, n2_ref,
                  wa_ref, wb_ref, wo_ref, wr_ref, x1_ref, h2_ref, lg_ref):
    pa = _dot(ya_ref[...], wa_ref[...])
    pb = _dot(yb_ref[...], wb_ref[...])
    merged = _sigmoid(ga_ref[...].astype(F32)) * pa + _sigmoid(gb_ref[...].astype(F32)) * pb
    x1 = x_ref[...] + g1_ref[0] * _dot(merged.astype(BF16), wo_ref[...])
    x1_ref[...] = x1
    ms = jnp.mean(x1 * x1, axis=-1, keepdims=True)
    h2 = x1 * lax.rsqrt(ms + EPS) * n2_ref[...]
    h2 = h2 * (1.0 + sc_ref[0]) + sh_ref[0]
    tm = h2.shape[0]
    for s in range(SUB):
        h2_ref[pl.ds(s, tm, stride=SUB), :] = h2[:, s * LANES:(s + 1) * LANES]
    lg_ref[...] = _dot(h2.astype(BF16), wr_ref[...])


def _outproj(ya, yb, proj, x2d, g1, sc2, sh2, norm2_g, wa, wb, wo, wr, seq):
    t, d = x2d.shape
    tm = min(512, seq)
    per_b = seq // tm
    const = lambda i: (0, 0)
    mod = lambda i: (i // per_b, 0, 0)
    return pl.pallas_call(
        _outproj_body,
        grid=(t // tm,),
        in_specs=[pl.BlockSpec((tm, d), lambda i: (i, 0)),
                  pl.BlockSpec((tm, SSD_D_INNER), lambda i: (i, 0)),
                  pl.BlockSpec((tm, d), lambda i: (i, 3)),
                  pl.BlockSpec((tm, d), lambda i: (i, 4)),
                  pl.BlockSpec((tm, d), lambda i: (i, 0)),
                  pl.BlockSpec((1, 1, d), mod),
                  pl.BlockSpec((1, 1, d), mod),
                  pl.BlockSpec((1, 1, d), mod),
                  pl.BlockSpec((1, d), const),
                  pl.BlockSpec((d, d), const),
                  pl.BlockSpec((SSD_D_INNER, d), const),
                  pl.BlockSpec((d, d), const),
                  pl.BlockSpec((d, LANES), const)],
        out_specs=[pl.BlockSpec((tm, d), lambda i: (i, 0)),
                   pl.BlockSpec((tm * SUB, LANES), lambda i: (i, 0)),
                   pl.BlockSpec((tm, LANES), lambda i: (i, 0))],
        out_shape=[jax.ShapeDtypeStruct((t, d), F32),
                   jax.ShapeDtypeStruct((t * SUB, LANES), F32),
                   jax.ShapeDtypeStruct((t, LANES), F32)],
        compiler_params=_params(("arbitrary",)),
        name="outproj",
    )(ya, yb, proj, proj, x2d, g1, sc2, sh2, norm2_g.astype(F32).reshape(1, d), wa, wb, wo, wr)


def _router_body(lg_ref, bias_ref, idx_ref, w_ref):
    tr = lg_ref.shape[0]
    per = N_EXPERTS // N_EXPERT_GROUPS
    lt = lg_ref[...].T
    scores = _sigmoid(lt[:N_EXPERTS])
    choice = scores + bias_ref[...]
    iota = lax.broadcasted_iota(jnp.int32, (per, tr), 0)
    grp = [choice[g * per:(g + 1) * per] for g in range(N_EXPERT_GROUPS)]
    sc_g = [scores[g * per:(g + 1) * per] for g in range(N_EXPERT_GROUPS)]

    gsc = jnp.zeros((N_EXPERT_GROUPS, tr), F32)
    for g in range(N_EXPERT_GROUPS):
        top1 = jnp.max(grp[g], axis=0, keepdims=True)
        idx1 = jnp.min(jnp.where(grp[g] == top1, iota, per), axis=0, keepdims=True)
        top2 = jnp.max(jnp.where(iota == idx1, NEG, grp[g]), axis=0, keepdims=True)
        gsc = jnp.where(iota == g, top1 + top2, gsc)

    gsel = jnp.zeros((N_EXPERT_GROUPS, tr), F32)
    cur = gsc
    for _ in range(TOPK_GROUPS):
        mx = jnp.max(cur, axis=0, keepdims=True)
        idx = jnp.min(jnp.where(cur == mx, iota, N_EXPERT_GROUPS), axis=0, keepdims=True)
        hit = iota == idx
        gsel = jnp.where(hit, 1.0, gsel)
        cur = jnp.where(hit, NEG, cur)

    cur_g = [jnp.where(gsel[g:g + 1] > 0.0, grp[g], NEG) for g in range(N_EXPERT_GROUPS)]
    idx_rows = jnp.zeros((TOP_K, tr), jnp.int32)
    w_rows = jnp.zeros((TOP_K, tr), F32)
    for kk in range(TOP_K):
        mx = cur_g[0]
        for g in range(1, N_EXPERT_GROUPS):
            mx = jnp.maximum(mx, cur_g[g])
        mx = jnp.max(mx, axis=0, keepdims=True)
        idx = jnp.where(cur_g[0] == mx, iota, N_EXPERTS)
        for g in range(1, N_EXPERT_GROUPS):
            idx = jnp.minimum(idx, jnp.where(cur_g[g] == mx, iota + g * per, N_EXPERTS))
        idx = jnp.min(idx, axis=0, keepdims=True)
        wk = jnp.zeros((per, tr), F32)
        for g in range(N_EXPERT_GROUPS):
            hit = (iota + g * per) == idx
            wk = wk + jnp.where(hit, sc_g[g], 0.0)
            cur_g[g] = jnp.where(hit, NEG, cur_g[g])
        idx_rows = jnp.where(iota == kk, idx, idx_rows)
        w_rows = jnp.where(iota == kk, jnp.sum(wk, axis=0, keepdims=True), w_rows)

    wsum = jnp.sum(w_rows, axis=0, keepdims=True)
    idx_ref[...] = idx_rows
    w_ref[...] = w_rows / wsum * ROUTED_SCALE


def _router(logits, router_bias, seq):
    t = logits.shape[0]
    tr = min(1024, seq)
    bias_b = jnp.broadcast_to(router_bias.astype(F32)[:, None], (N_EXPERTS, tr))
    return pl.pallas_call(
        _router_body,
        grid=(t // tr,),
        in_specs=[pl.BlockSpec((tr, LANES), lambda i: (i, 0)),
                  pl.BlockSpec((N_EXPERTS, tr), lambda i: (0, 0))],
        out_specs=[pl.BlockSpec((TOP_K, tr), lambda i: (0, i)),
                   pl.BlockSpec((TOP_K, tr), lambda i: (0, i))],
        out_shape=[jax.ShapeDtypeStruct((TOP_K, t), jnp.int32),
                   jax.ShapeDtypeStruct((TOP_K, t), F32)],
        compiler_params=_params(("arbitrary",)),
        name="router",
    )(logits, bias_b)


def _route_tables(idx_t, w_t, groups, tg, rows, ntiles):
    i32 = jnp.int32
    e = idx_t.reshape(TOP_K, groups, tg).transpose(1, 0, 2).reshape(groups, TOP_K * tg)
    w = w_t.reshape(TOP_K, groups, tg).transpose(1, 0, 2).reshape(groups, TOP_K * tg)
    tl = jnp.broadcast_to(jnp.arange(tg, dtype=i32)[None, None, :], (groups, TOP_K, tg)).reshape(groups, TOP_K * tg)
    skey, sw = lax.sort((e * tg + tl, w), dimension=1, num_keys=1)
    stok = skey % tg
    cnt = jnp.sum((e[:, :, None] == jnp.arange(N_EXPERTS, dtype=i32)[None, None, :]).astype(i32), axis=1)
    off = jnp.cumsum(cnt, axis=1) - cnt
    ntile = (cnt + rows - 1) // rows
    tend = jnp.cumsum(ntile, axis=1)
    tstart = tend - ntile
    n = jnp.arange(ntiles, dtype=i32)
    te = jnp.sum((n[None, :, None] >= tend[:, None, :]).astype(i32), axis=2)
    valid_tile = n[None, :] < tend[:, -1:]
    te = jnp.minimum(te, N_EXPERTS - 1)
    k0 = (n[None, :] - jnp.take_along_axis(tstart, te, axis=1)) * rows
    nvalid = jnp.where(valid_tile, jnp.clip(jnp.take_along_axis(cnt, te, axis=1) - k0, 0, rows), 0)
    r = jnp.arange(rows, dtype=i32)
    j = (jnp.take_along_axis(off, te, axis=1) + k0)[:, :, None] + r[None, None, :]
    rvalid = r[None, None, :] < nvalid[:, :, None]
    jc = jnp.clip(j, 0, TOP_K * tg - 1).reshape(groups, ntiles * rows)
    tok = jnp.take_along_axis(stok, jc, axis=1).reshape(groups, ntiles, rows)
    wv = jnp.take_along_axis(sw, jc, axis=1).reshape(groups, ntiles, rows)
    tok_g = jnp.where(rvalid, tok, 0)
    tok_s = jnp.where(rvalid, tok, tg)
    wv = jnp.where(rvalid, wv, 0.0)
    return te, nvalid, tok_g, tok_s, wv


def _routed_body(te_ref, nv_ref, tokg_ref, toks_ref, wt_ref, h_ref, wg_ref, wu_ref, wd_ref, acc_ref,
                 xbuf, ybuf):
    g = pl.program_id(0)
    n = pl.program_id(1)
    rows = MOE_ROWS
    slot = n % 2
    has_cur = nv_ref[g, n + 1] > 0
    has_prev = nv_ref[g, n] > 0

    @pl.when(n == 0)
    def _():
        acc_ref[...] = jnp.zeros(acc_ref.shape, F32)

    def compute():
        for r in range(rows):
            t8 = pl.multiple_of(tokg_ref[0, 0, 0, r], SUB)
            xbuf[pl.ds(r * SUB, SUB), :] = h_ref[pl.ds(t8, SUB), :]
        x2d = jnp.concatenate([xbuf[pl.ds(s, rows, stride=SUB), :] for s in range(SUB)], axis=1).astype(BF16)
        a = _dot(x2d, wg_ref[0].astype(BF16))
        b = _dot(x2d, wu_ref[0].astype(BF16))
        act = (a * _sigmoid(a) * b).astype(BF16)
        y = _dot(act, wd_ref[0].astype(BF16))
        for s in range(SUB):
            ybuf[slot, pl.ds(s, rows, stride=SUB), :] = y[:, s * LANES:(s + 1) * LANES]

    def scatter_prev():
        for c in range(rows // SUB):
            upd = []
            for r in range(c * SUB, (c + 1) * SUB):
                t8 = pl.multiple_of(toks_ref[0, 0, 0, r], SUB)
                upd.append((t8, acc_ref[0, pl.ds(t8, SUB), :]
                            + wt_ref[0, 0, 0, r] * ybuf[1 - slot, pl.ds(r * SUB, SUB), :]))
            for t8, v in upd:
                acc_ref[0, pl.ds(t8, SUB), :] = v

    @pl.when(jnp.logical_and(has_cur, has_prev))
    def _():
        scatter_prev()
        compute()

    @pl.when(jnp.logical_and(has_cur, jnp.logical_not(has_prev)))
    def _():
        compute()

    @pl.when(jnp.logical_and(jnp.logical_not(has_cur), has_prev))
    def _():
        scatter_prev()


def _routed(hv, tables, w_eg, w_eu, w_ed, groups, tg, ntiles):
    te, nvalid, tok_g, tok_s, wv = tables
    d, f = D_MODEL, EXPERT_DIM
    rows = MOE_ROWS
    i32 = jnp.int32
    te = jnp.concatenate([te, te[:, -1:]], axis=1)
    zero = jnp.zeros((groups, 1), i32)
    nvp = jnp.concatenate([zero, nvalid, zero], axis=1)
    tok_g, tok_s = (a * SUB for a in (tok_g, tok_s))
    tok_g, tok_s, wv = (a.reshape(groups, ntiles, 1, rows) for a in (tok_g, tok_s, wv))
    cur = lambda g, n, te, nv: (g, jnp.minimum(n, ntiles - 1), 0, 0)
    prev = lambda g, n, te, nv: (g, jnp.maximum(n - 1, 0), 0, 0)
    smem = functools.partial(pl.BlockSpec, (1, 1, 1, rows), memory_space=pltpu.SMEM)
    grid_spec = pltpu.PrefetchScalarGridSpec(
        num_scalar_prefetch=2,
        grid=(groups, ntiles + 1),
        in_specs=[smem(index_map=cur), smem(index_map=prev), smem(index_map=prev),
                  pl.BlockSpec((tg * SUB, LANES), lambda g, n, te, nv: (g, 0), pipeline_mode=pl.Buffered(1)),
                  pl.BlockSpec((1, d, f), lambda g, n, te, nv: (te[g, n], 0, 0)),
                  pl.BlockSpec((1, d, f), lambda g, n, te, nv: (te[g, n], 0, 0)),
                  pl.BlockSpec((1, f, d), lambda g, n, te, nv: (te[g, n], 0, 0))],
        out_specs=pl.BlockSpec((1, (tg + 1) * SUB, LANES), lambda g, n, te, nv: (g, 0, 0),
                               pipeline_mode=pl.Buffered(1)),
        scratch_shapes=[pltpu.VMEM((rows * SUB, LANES), F32), pltpu.VMEM((2, rows * SUB, LANES), F32)],
    )
    return pl.pallas_call(
        _routed_body,
        grid_spec=grid_spec,
        out_shape=jax.ShapeDtypeStruct((groups, (tg + 1) * SUB, LANES), F32),
        compiler_params=_params(("arbitrary", "arbitrary")),
        name="routed",
    )(te, nvp, tok_g, tok_s, wv, hv, w_eg, w_eu, w_ed)


def _combine_body(h_ref, r_ref, x1_ref, g2_ref, sg_ref, su_ref, sd_ref, o_ref):
    tm = x1_ref.shape[0]
    h = jnp.concatenate([h_ref[pl.ds(s, tm, stride=SUB), :] for s in range(SUB)], axis=1)
    routed = jnp.concatenate([r_ref[0, pl.ds(s, tm, stride=SUB), :] for s in range(SUB)], axis=1)
    hb = h.astype(BF16)
    a = _dot(hb, sg_ref[...])
    b = _dot(hb, su_ref[...])
    shared = _dot((a * _sigmoid(a) * b).astype(BF16), sd_ref[...])
    o_ref[...] = x1_ref[...] + g2_ref[0] * (routed + shared)


def _combine(hv, routed, x1, g2, w_sg, w_su, w_sd, seq, tg):
    t, d = x1.shape
    tm = min(1024, seq, tg)
    per_b = seq // tm
    per_g = tg // tm
    f = w_sg.shape[1]
    const = lambda i: (0, 0)
    row = lambda i: (i, 0)
    return pl.pallas_call(
        _combine_body,
        grid=(t // tm,),
        in_specs=[pl.BlockSpec((tm * SUB, LANES), row),
                  pl.BlockSpec((1, tm * SUB, LANES), lambda i: (i // per_g, i % per_g, 0)),
                  pl.BlockSpec((tm, d), row),
                  pl.BlockSpec((1, 1, d), lambda i: (i // per_b, 0, 0)),
                  pl.BlockSpec((d, f), const),
                  pl.BlockSpec((d, f), const),
                  pl.BlockSpec((f, d), const)],
        out_specs=pl.BlockSpec((tm, d), row),
        out_shape=jax.ShapeDtypeStruct((t, d), F32),
        compiler_params=_params(("arbitrary",)),
        name="combine",
    )(hv, routed, x1, g2, w_sg, w_su, w_sd)


def kernel(x, c, positions, w_ada, b_ada, norm1_g, w_in, q_norm_g, k_norm_g, lambda_qk, subln_g, conv_w, conv_b, dt_bias, a_log, d_skip, ssd_norm_g, w_branch_a, w_branch_b, w_out, norm2_g, w_router, router_bias, w_exp_gate, w_exp_up, w_exp_down, w_sh_gate, w_sh_up, w_sh_down):
    batch, seq, d = x.shape
    t = batch * seq
    tq = min(1024, seq)
    tk = tq // 2
    xf = x.reshape(t, d)
    for layer in range(w_ada.shape[0]):
        lambda_init = 0.8 - 0.6 * math.exp(-0.3 * layer)
        c_pad = jnp.pad(c.astype(F32), ((0, 8 - batch), (0, 0)))
        mod = _ada(c_pad, w_ada[layer], b_ada[layer])[:batch].reshape(batch, 6, 1, d)
        sh1, sc1, g1, sh2, sc2, g2 = (mod[:, n] for n in range(6))

        w_l = w_in[layer]
        w_main = w_l[:, :MAIN_COLS].astype(BF16)
        wdt = jnp.pad(w_l[:, MAIN_COLS:], ((0, 0), (0, LANES - SSD_HEADS)))
        wdt_hi, wdt_lo = _split2(wdt)
        proj, dt = _inproj(xf, norm1_g[layer], sc1, sh1, w_main, wdt_hi, wdt_lo, seq)

        qt, kk, vt = _qkvprep(proj, positions, q_norm_g[layer], k_norm_g[layer], batch, seq, tk)
        ya = _attention(qt, kk, vt, lambda_qk[layer], subln_g[layer], lambda_init, batch, seq, tq, tk)
        yb = _ssd(proj, dt, conv_w[layer], conv_b[layer], dt_bias[layer], a_log[layer], d_skip[layer],
                  ssd_norm_g[layer], batch, seq)

        wr = jnp.pad(w_router[layer], ((0, 0), (0, LANES - N_EXPERTS))).astype(BF16)
        x1, h2, logits = _outproj(ya, yb, proj, xf, g1, sc2, sh2, norm2_g[layer],
                                  w_branch_a[layer].astype(BF16), w_branch_b[layer].astype(BF16),
                                  w_out[layer].astype(BF16), wr, seq)
        idx_t, w_t = _router(logits, router_bias[layer], seq)
        tg = min(4096, t)
        groups = t // tg
        ntiles = tg * TOP_K // MOE_ROWS + N_EXPERTS
        tables = _route_tables(idx_t, w_t, groups, tg, MOE_ROWS, ntiles)
        routed = _routed(h2, tables, w_exp_gate[layer], w_exp_up[layer], w_exp_down[layer], groups, tg, ntiles)
        xf = _combine(h2, routed, x1, g2, w_sh_gate[layer].astype(BF16), w_sh_up[layer].astype(BF16),
                      w_sh_down[layer].astype(BF16), seq, tg)
    return xf.reshape(batch, seq, d)
```

```python
import functools
import math

import jax
import jax.numpy as jnp
from jax import lax
from jax.experimental import pallas as pl
from jax.experimental.pallas import tpu as pltpu

F32 = jnp.float32
BF16 = jnp.bfloat16

D_MODEL = 1024
ATTN_HEADS = 8
ATTN_HEAD_DIM = 64
ATTN_V_DIM = 128
ROPE_THETA = 10000.0
SSD_D_INNER = 2048
SSD_HEAD_DIM = 64
SSD_HEADS = 32
SSD_GROUPS = 4
SSD_D_STATE = 128
SSD_CONV = 4
SSD_CHUNK = 128
SSD_CONV_DIM = 3072
N_EXPERTS = 64
TOP_K = 8
N_EXPERT_GROUPS = 8
TOPK_GROUPS = 4
EXPERT_DIM = 256
ROUTED_SCALE = 2.5
EPS = 1e-6
NEG = -1e30
LOG2E = math.log2(math.e)
Q_SCALE = ATTN_HEAD_DIM ** -0.5 * LOG2E
SSD_HALO = 16

LANES = 128
SUB = 8
MOE_ROWS = 576
MAIN_COLS = 10240
VMEM_LIMIT = 56 * 1024 * 1024


def _dot(a, b):
    return jnp.dot(a, b, preferred_element_type=F32)


def _sigmoid(x):
    return 1.0 / (1.0 + jnp.exp(-x))


def _split2(a):
    hi = a.astype(BF16)
    lo = (a - hi.astype(F32)).astype(BF16)
    return hi, lo


def _split3(a):
    hi = a.astype(BF16)
    r = a - hi.astype(F32)
    mid = r.astype(BF16)
    lo = (r - mid.astype(F32)).astype(BF16)
    return hi, mid, lo


def _params(sem, vmem=VMEM_LIMIT):
    return pltpu.CompilerParams(dimension_semantics=sem, vmem_limit_bytes=vmem)


def _ada_body(c_ref, w_ref, b_ref, o_ref):
    c = c_ref[...]
    sc = c * _sigmoid(c)
    chi, clo = _split2(sc)
    whi, wlo = _split2(w_ref[...])
    o_ref[...] = _dot(chi, whi) + _dot(chi, wlo) + _dot(clo, whi) + b_ref[...]


def _ada(c_pad, w, b):
    d, n = w.shape
    tn = 1024
    return pl.pallas_call(
        _ada_body,
        grid=(n // tn,),
        in_specs=[pl.BlockSpec((8, d), lambda j: (0, 0)),
                  pl.BlockSpec((d, tn), lambda j: (0, j)),
                  pl.BlockSpec((1, tn), lambda j: (0, j))],
        out_specs=pl.BlockSpec((8, tn), lambda j: (0, j)),
        out_shape=jax.ShapeDtypeStruct((8, n), F32),
        compiler_params=_params(("arbitrary",)),
        name="ada",
    )(c_pad, w, b.reshape(1, n))


def _inproj_body(x_ref, g_ref, sc_ref, sh_ref, w_ref, wdh_ref, wdl_ref, o_ref, dt_ref, h_scr):
    @pl.when(pl.program_id(1) == 0)
    def _():
        x = x_ref[...]
        ms = jnp.mean(x * x, axis=-1, keepdims=True)
        h = x * lax.rsqrt(ms + EPS) * g_ref[...]
        h = h * (1.0 + sc_ref[0]) + sh_ref[0]
        hb, hl = _split2(h)
        h_scr[...] = hb
        dt_ref[...] = _dot(hb, wdh_ref[...]) + _dot(hb, wdl_ref[...]) + _dot(hl, wdh_ref[...])

    o_ref[...] = _dot(h_scr[...], w_ref[...].astype(BF16)).astype(BF16)


def _inproj(x2d, g, sc, sh, w_in, layer, wdt_hi, wdt_lo, seq):
    t, d = x2d.shape
    tm = min(1024, seq)
    tn = 1024
    per_b = seq // tm
    return pl.pallas_call(
        _inproj_body,
        grid=(t // tm, MAIN_COLS // tn),
        in_specs=[pl.BlockSpec((tm, d), lambda i, j: (i, 0)),
                  pl.BlockSpec((1, d), lambda i, j: (0, 0)),
                  pl.BlockSpec((1, 1, d), lambda i, j: (i // per_b, 0, 0)),
                  pl.BlockSpec((1, 1, d), lambda i, j: (i // per_b, 0, 0)),
                  pl.BlockSpec((None, d, tn), lambda i, j: (layer, 0, j)),
                  pl.BlockSpec((d, LANES), lambda i, j: (0, 0)),
                  pl.BlockSpec((d, LANES), lambda i, j: (0, 0))],
        out_specs=[pl.BlockSpec((tm, tn), lambda i, j: (i, j)),
                   pl.BlockSpec((tm, LANES), lambda i, j: (i, 0))],
        out_shape=[jax.ShapeDtypeStruct((t, MAIN_COLS), BF16),
                   jax.ShapeDtypeStruct((t, LANES), F32)],
        scratch_shapes=[pltpu.VMEM((tm, d), BF16)],
        compiler_params=_params(("arbitrary", "arbitrary")),
        name="inproj",
    )(x2d, g.reshape(1, d), sc, sh, w_in, wdt_hi, wdt_lo)


def _qkvprep_body(q_ref, k_ref, v_ref, pos_ref, invf_ref, gq_ref, gk_ref, qt_ref, ko_ref, vt_ref, *, tk):
    ts = q_ref.shape[0]
    half = ATTN_HEAD_DIM // 2
    ang = pos_ref[0].astype(F32) * invf_ref[...]
    cos = jnp.cos(ang)
    sin = jnp.sin(ang)
    zeros = jnp.zeros((ATTN_HEAD_DIM, ts), F32)

    def norm_rope(blk, g):
        ms = jnp.mean(blk * blk, axis=0, keepdims=True)
        r = blk * lax.rsqrt(ms + EPS) * g
        x1 = r[:half]
        x2 = r[half:]
        return jnp.concatenate([x1 * cos - x2 * sin, x2 * cos + x1 * sin], axis=0)

    for h in range(ATTN_HEADS):
        sl = slice(h * LANES, (h + 1) * LANES)
        qht = q_ref[:, sl].astype(F32).T
        q1 = norm_rope(qht[:ATTN_HEAD_DIM], gq_ref[...]) * Q_SCALE
        q2 = norm_rope(qht[ATTN_HEAD_DIM:], gq_ref[...]) * Q_SCALE
        qt_ref[0, 2 * h] = jnp.concatenate([q1, zeros], axis=0).astype(BF16)
        qt_ref[0, 2 * h + 1] = jnp.concatenate([zeros, q2], axis=0).astype(BF16)
        kht = k_ref[:, sl].astype(F32).T
        k1 = norm_rope(kht[:ATTN_HEAD_DIM], gk_ref[...])
        k2 = norm_rope(kht[ATTN_HEAD_DIM:], gk_ref[...])
        ko_ref[0, h] = jnp.concatenate([k1, k2], axis=0).T.astype(BF16)
        vht = v_ref[:, sl].astype(F32).T.astype(BF16)
        for cc in range(ts // tk):
            vt_ref[0, h, cc] = vht[:, cc * tk:(cc + 1) * tk]


def _qkvprep(proj, positions, q_norm_g, k_norm_g, batch, seq, tk):
    ts = min(512, seq)
    ns = seq // ts
    half = ATTN_HEAD_DIM // 2
    inv_freq = ROPE_THETA ** (-jnp.arange(half, dtype=F32) * 2.0 / ATTN_HEAD_DIM)
    invf_b = jnp.broadcast_to(inv_freq[:, None], (half, ts))
    gq_b = jnp.broadcast_to(q_norm_g.astype(F32)[:, None], (ATTN_HEAD_DIM, ts))
    gk_b = jnp.broadcast_to(k_norm_g.astype(F32)[:, None], (ATTN_HEAD_DIM, ts))
    pos3 = positions.reshape(batch, 1, seq)
    const = lambda b, i: (0, 0)
    return pl.pallas_call(
        functools.partial(_qkvprep_body, tk=tk),
        grid=(batch, ns),
        in_specs=[pl.BlockSpec((ts, D_MODEL), lambda b, i: (b * ns + i, 0)),
                  pl.BlockSpec((ts, D_MODEL), lambda b, i: (b * ns + i, 1)),
                  pl.BlockSpec((ts, D_MODEL), lambda b, i: (b * ns + i, 2)),
                  pl.BlockSpec((1, 1, ts), lambda b, i: (b, 0, i)),
                  pl.BlockSpec((half, ts), const),
                  pl.BlockSpec((ATTN_HEAD_DIM, ts), const),
                  pl.BlockSpec((ATTN_HEAD_DIM, ts), const)],
        out_specs=[pl.BlockSpec((1, 2 * ATTN_HEADS, LANES, ts), lambda b, i: (b, 0, 0, i)),
                   pl.BlockSpec((1, ATTN_HEADS, ts, LANES), lambda b, i: (b, 0, i, 0)),
                   pl.BlockSpec((1, ATTN_HEADS, ts // tk, LANES, tk), lambda b, i: (b, 0, i, 0, 0))],
        out_shape=[jax.ShapeDtypeStruct((batch, 2 * ATTN_HEADS, LANES, seq), BF16),
                   jax.ShapeDtypeStruct((batch, ATTN_HEADS, seq, LANES), BF16),
                   jax.ShapeDtypeStruct((batch, ATTN_HEADS, seq // tk, LANES, tk), BF16)],
        compiler_params=_params(("arbitrary", "arbitrary")),
        name="qkvprep",
    )(proj, proj, proj, pos3, invf_b, gq_b, gk_b)


def _attn_body(lam_ref, sg_ref, qt_ref, k_ref, vt_ref, o_ref, acc_scr, sa_scr, sb_scr, *, tq, tk, lambda_init):
    i = pl.program_id(2)
    q_maps = (qt_ref[0, 0], qt_ref[0, 1])
    acc_scr[...] = jnp.zeros(acc_scr.shape, F32)
    row = lax.broadcasted_iota(jnp.int32, (tk, tq), 0)
    colq = lax.broadcasted_iota(jnp.int32, (tk, tq), 1)

    def qk(j, s_scr, lo=0):
        kb = k_ref[0, 0, pl.ds(pl.multiple_of(j * tk, tk), tk), :]
        for mi in range(2):
            s_scr[mi, :, lo:] = _dot(kb, q_maps[mi][:, lo:])

    def process(j, s_scr, carry, diag=None, lo=0):
        vb = vt_ref[0, 0, j]
        out = []
        for mi in range(2):
            m_old, l_old = carry[2 * mi], carry[2 * mi + 1]
            s = s_scr[mi]
            if diag is not None:
                s = jnp.where(row + diag * tk <= colq, s, NEG)
            m_new = jnp.maximum(m_old, jnp.max(s, axis=0, keepdims=True))
            alpha = jnp.exp2(m_old - m_new)
            p = jnp.exp2(s - m_new)
            l_new = alpha * l_old + jnp.sum(p, axis=0, keepdims=True)
            alpha_b = jnp.broadcast_to(alpha, (ATTN_V_DIM, tq))
            acc_scr[mi, :, lo:] = alpha_b[:, lo:] * acc_scr[mi, :, lo:] + _dot(vb, p[:, lo:].astype(BF16))
            out += [m_new, l_new]
        return tuple(out)

    def pair(p, carry):
        qk(2 * p + 1, sb_scr)
        carry = process(2 * p, sa_scr, carry)
        qk(2 * p + 2, sa_scr)
        return process(2 * p + 1, sb_scr, carry)

    @pl.when(i == 0)
    def _():
        sb_scr[:, :, :tk] = jnp.zeros((2, tk, tk), F32)

    init = (jnp.full((1, tq), NEG, F32), jnp.zeros((1, tq), F32)) * 2
    qk(0, sa_scr)
    carry = lax.fori_loop(0, i, pair, init)
    qk(2 * i + 1, sb_scr, lo=tk)
    carry = process(2 * i, sa_scr, carry, diag=0)
    carry = process(2 * i + 1, sb_scr, carry, diag=1, lo=tk)

    lv = lam_ref[...]
    lam = (jnp.exp(jnp.sum(lv[0:1] * lv[1:2], keepdims=True))
           - jnp.exp(jnp.sum(lv[2:3] * lv[3:4], keepdims=True)) + lambda_init)
    o = acc_scr[0] / carry[1] - lam * (acc_scr[1] / carry[3])
    ms = jnp.mean(o * o, axis=0, keepdims=True)
    o = o * lax.rsqrt(ms + EPS) * sg_ref[...] * (1.0 - lambda_init)
    o_ref[...] = o.T.astype(BF16)


def _attention(qt, kk, vt, lambda_qk, subln_g, lambda_init, batch, seq, tq, tk):
    nq = seq // tq
    sg_b = jnp.broadcast_to(subln_g.astype(F32)[:, None], (ATTN_V_DIM, tq))
    return pl.pallas_call(
        functools.partial(_attn_body, tq=tq, tk=tk, lambda_init=lambda_init),
        grid=(batch, ATTN_HEADS, nq),
        in_specs=[pl.BlockSpec((4, ATTN_HEAD_DIM), lambda b, h, i: (0, 0)),
                  pl.BlockSpec((ATTN_V_DIM, tq), lambda b, h, i: (0, 0)),
                  pl.BlockSpec((1, 2, LANES, tq), lambda b, h, i: (b, h, 0, i)),
                  pl.BlockSpec((1, 1, seq, LANES), lambda b, h, i: (b, h, 0, 0)),
                  pl.BlockSpec((1, 1, seq // tk, LANES, tk), lambda b, h, i: (b, h, 0, 0, 0))],
        out_specs=pl.BlockSpec((tq, LANES), lambda b, h, i: (b * nq + i, h)),
        out_shape=jax.ShapeDtypeStruct((batch * seq, ATTN_HEADS * ATTN_V_DIM), BF16),
        scratch_shapes=[pltpu.VMEM((2, ATTN_V_DIM, tq), F32),
                        pltpu.VMEM((2, tk, tq), F32),
                        pltpu.VMEM((2, tk, tq), F32)],
        compiler_params=_params(("arbitrary", "arbitrary", "arbitrary")),
        name="attn",
    )(lambda_qk.astype(F32), sg_b, qt, kk, vt)


def _ssd_body(z0_ref, z1_ref, x0_ref, x1_ref, bc_ref, dt_ref, cw_ref, cb_ref, dtb_ref, alog_ref,
              dsk_ref, ng_ref, tri_ref, shift_ref, y_ref, ext_scr, st_scr):
    L = SSD_CHUNK
    halo = SSD_HALO
    c = pl.program_id(1)

    @pl.when(c == 0)
    def _():
        ext_scr[0:halo, :] = jnp.zeros((halo, SSD_CONV_DIM), BF16)
        st_scr[...] = jnp.zeros(st_scr.shape, F32)

    ext_scr[halo:halo + L, 0:1024] = x0_ref[...]
    ext_scr[halo:halo + L, 1024:2048] = x1_ref[...]
    ext_scr[halo:halo + L, 2048:3072] = bc_ref[...]
    sh = _dot(shift_ref[...], ext_scr[...])
    acc = cb_ref[...] + cw_ref[SSD_CONV - 1:SSD_CONV, :] * ext_scr[halo:halo + L, :].astype(F32)
    for k in range(SSD_CONV - 1):
        acc = acc + cw_ref[k:k + 1, :] * sh[k * L:(k + 1) * L]
    u = acc * _sigmoid(acc)
    ext_scr[0:halo, :] = ext_scr[L:L + halo, :]

    xdt_in = dt_ref[...] + dtb_ref[...]
    dtp = jnp.maximum(xdt_in, 0.0) + jnp.log(1.0 + jnp.exp(-jnp.abs(xdt_in)))
    a = dtp * (-jnp.exp(alog_ref[...]) * LOG2E)
    tri = tri_ref[...]
    a_hi, a_mid, a_lo = _split3(a)
    acum = _dot(tri, a_hi) + _dot(tri, a_mid) + _dot(tri, a_lo)
    acum_t = acum.T
    lane = lax.broadcasted_iota(jnp.int32, (L, LANES), 1)
    first = lane < SSD_HEAD_DIM
    tril = lax.broadcasted_iota(jnp.int32, (L, L), 0) >= lax.broadcasted_iota(jnp.int32, (L, L), 1)
    gn = SSD_GROUPS * SSD_D_STATE
    gw = SSD_D_INNER // SSD_GROUPS

    def col(mat, h):
        return jnp.broadcast_to(mat[:, h:h + 1], (L, LANES))

    for g in range(SSD_GROUPS):
        bg = u[:, SSD_D_INNER + g * SSD_D_STATE:SSD_D_INNER + (g + 1) * SSD_D_STATE]
        cg = u[:, SSD_D_INNER + gn + g * SSD_D_STATE:SSD_D_INNER + gn + (g + 1) * SSD_D_STATE].astype(BF16)
        bgt = bg.T.astype(BF16)
        cb = _dot(cg, bgt)
        sprev = st_scr[g]
        yoff = _dot(cg, sprev.astype(BF16))
        y_parts, xdd_parts, dec_parts = [], [], []
        for p in range(4):
            h0 = 8 * g + 2 * p
            c0, c1 = col(acum, h0), col(acum, h0 + 1)
            ac_exp = jnp.where(first, c0, c1)
            dt_exp = jnp.where(first, col(dtp, h0), col(dtp, h0 + 1))
            xsl = u[:, h0 * SSD_HEAD_DIM:h0 * SSD_HEAD_DIM + LANES]
            xdt = xsl * dt_exp
            l0 = jnp.exp2(jnp.where(tril, c0 - acum_t[h0:h0 + 1, :], NEG))
            l1 = jnp.exp2(jnp.where(tril, c1 - acum_t[h0 + 1:h0 + 2, :], NEG))
            m2 = jnp.concatenate([(cb * l0).astype(BF16), (cb * l1).astype(BF16)], axis=1)
            xb = xdt.astype(BF16)
            zb = jnp.zeros_like(xb)
            rhs = jnp.concatenate([jnp.where(first, xb, zb), jnp.where(first, zb, xb)], axis=0)
            yd = _dot(m2, rhs)
            yo = yoff[:, p * LANES:(p + 1) * LANES] * jnp.exp2(ac_exp)
            dsk = dsk_ref[:, h0 * SSD_HEAD_DIM:h0 * SSD_HEAD_DIM + LANES]
            y_parts.append(yd + yo + xsl * dsk)
            last = ac_exp[L - 1:L, :]
            xdd_parts.append((xdt * jnp.exp2(last - ac_exp)).astype(BF16))
            dec_parts.append(jnp.exp2(last))
        xdd = jnp.concatenate(xdd_parts, axis=1)
        dec = jnp.concatenate(dec_parts, axis=1)
        st_scr[g] = sprev * dec + _dot(bgt, xdd)
        yg = jnp.concatenate(y_parts, axis=1)
        zref = z0_ref if g < 2 else z1_ref
        zg = zref[:, (g % 2) * gw:(g % 2 + 1) * gw].astype(F32)
        yg = yg * (zg * _sigmoid(zg))
        ms = jnp.mean(yg * yg, axis=-1, keepdims=True)
        yg = yg * lax.rsqrt(ms + EPS) * ng_ref[:, g * gw:(g + 1) * gw]
        y_ref[:, g * gw:(g + 1) * gw] = yg.astype(BF16)


def _ssd(proj, dt, conv_w, conv_b, dt_bias, a_log, d_skip, norm_g, batch, seq):
    L = SSD_CHUNK
    nc = seq // L
    pad = LANES - SSD_HEADS
    dtb = jnp.pad(dt_bias.astype(F32), (0, pad)).reshape(1, LANES)
    alog = jnp.pad(a_log.astype(F32), (0, pad)).reshape(1, LANES)
    dsk = jnp.repeat(d_skip.astype(F32), SSD_HEAD_DIM).reshape(1, SSD_D_INNER)
    tri = jnp.tril(jnp.ones((L, L), F32)).astype(BF16)
    tt = jnp.arange(L)
    shift = jnp.concatenate(
        [jax.nn.one_hot(SSD_HALO + tt - (SSD_CONV - 1) + k, L + SSD_HALO, dtype=BF16) for k in range(SSD_CONV - 1)],
        axis=0)
    row = lambda b, c: (b * nc + c)
    const = lambda b, c: (0, 0)
    return pl.pallas_call(
        _ssd_body,
        grid=(batch, nc),
        in_specs=[pl.BlockSpec((L, 1024), lambda b, c: (row(b, c), 5)),
                  pl.BlockSpec((L, 1024), lambda b, c: (row(b, c), 6)),
                  pl.BlockSpec((L, 1024), lambda b, c: (row(b, c), 7)),
                  pl.BlockSpec((L, 1024), lambda b, c: (row(b, c), 8)),
                  pl.BlockSpec((L, 1024), lambda b, c: (row(b, c), 9)),
                  pl.BlockSpec((L, LANES), lambda b, c: (row(b, c), 0)),
                  pl.BlockSpec((SSD_CONV, SSD_CONV_DIM), const),
                  pl.BlockSpec((1, SSD_CONV_DIM), const),
                  pl.BlockSpec((1, LANES), const),
                  pl.BlockSpec((1, LANES), const),
                  pl.BlockSpec((1, SSD_D_INNER), const),
                  pl.BlockSpec((1, SSD_D_INNER), const),
                  pl.BlockSpec((L, L), const),
                  pl.BlockSpec(((SSD_CONV - 1) * L, L + SSD_HALO), const)],
        out_specs=pl.BlockSpec((L, SSD_D_INNER), lambda b, c: (row(b, c), 0)),
        out_shape=jax.ShapeDtypeStruct((batch * seq, SSD_D_INNER), BF16),
        scratch_shapes=[pltpu.VMEM((L + SSD_HALO, SSD_CONV_DIM), BF16),
                        pltpu.VMEM((SSD_GROUPS, SSD_D_STATE, SSD_D_INNER // SSD_GROUPS), F32)],
        compiler_params=_params(("arbitrary", "arbitrary")),
        name="ssd",
    )(proj, proj, proj, proj, proj, dt, conv_w.astype(F32), conv_b.astype(F32).reshape(1, -1), dtb, alog,
      dsk, norm_g.astype(F32).reshape(1, -1), tri, shift)


def _outproj_body(ya_ref, yb_ref, ga_ref, gb_ref, x_ref, g1_ref, sc_ref, sh_ref, n2_ref,
                  wa_ref, wb_ref, wo_ref, wr_ref, x1_ref, h2_ref, lg_ref):
    pa = _dot(ya_ref[...], wa_ref[...])
    pb = _dot(yb_ref[...], wb_ref[...])
    merged = _sigmoid(ga_ref[...].astype(F32)) * pa + _sigmoid(gb_ref[...].astype(F32)) * pb
    x1 = x_ref[...] + g1_ref[0] * _dot(merged.astype(BF16), wo_ref[...])
    x1_ref[...] = x1
    ms = jnp.mean(x1 * x1, axis=-1, keepdims=True)
    h2 = x1 * lax.rsqrt(ms + EPS) * n2_ref[...]
    h2 = h2 * (1.0 + sc_ref[0]) + sh_ref[0]
    tm = h2.shape[0]
    for s in range(SUB):
        h2_ref[pl.ds(s, tm, stride=SUB), :] = h2[:, s * LANES:(s + 1) * LANES]
    lg_ref[...] = _dot(h2.astype(BF16), wr_ref[...])


def _outproj(ya, yb, proj, x2d, g1, sc2, sh2, norm2_g, wa, wb, wo, wr, seq):
    t, d = x2d.shape
    tm = min(512, seq)
    per_b = seq // tm
    const = lambda i: (0, 0)
    mod = lambda i: (i // per_b, 0, 0)
    return pl.pallas_call(
        _outproj_body,
        grid=(t // tm,),
        in_specs=[pl.BlockSpec((tm, d), lambda i: (i, 0)),
                  pl.BlockSpec((tm, SSD_D_INNER), lambda i: (i, 0)),
                  pl.BlockSpec((tm, d), lambda i: (i, 3)),
                  pl.BlockSpec((tm, d), lambda i: (i, 4)),
                  pl.BlockSpec((tm, d), lambda i: (i, 0)),
                  pl.BlockSpec((1, 1, d), mod),
                  pl.BlockSpec((1, 1, d), mod),
                  pl.BlockSpec((1, 1, d), mod),
                  pl.BlockSpec((1, d), const),
                  pl.BlockSpec((d, d), const),
                  pl.BlockSpec((SSD_D_INNER, d), const),
                  pl.BlockSpec((d, d), const),
                  pl.BlockSpec((d, LANES), const)],
        out_specs=[pl.BlockSpec((tm, d), lambda i: (i, 0)),
                   pl.BlockSpec((tm * SUB, LANES), lambda i: (i, 0)),
                   pl.BlockSpec((tm, LANES), lambda i: (i, 0))],
        out_shape=[jax.ShapeDtypeStruct((t, d), F32),
                   jax.ShapeDtypeStruct((t * SUB, LANES), F32),
                   jax.ShapeDtypeStruct((t, LANES), F32)],
        compiler_params=_params(("arbitrary",)),
        name="outproj",
    )(ya, yb, proj, proj, x2d, g1, sc2, sh2, norm2_g.astype(F32).reshape(1, d), wa, wb, wo, wr)


def _router_body(lg_ref, bias_ref, idx_ref, w_ref):
    tr = lg_ref.shape[0]
    per = N_EXPERTS // N_EXPERT_GROUPS
    lt = lg_ref[...].T
    scores = _sigmoid(lt[:N_EXPERTS])
    choice = scores + bias_ref[...]
    iota = lax.broadcasted_iota(jnp.int32, (per, tr), 0)
    grp = [choice[g * per:(g + 1) * per] for g in range(N_EXPERT_GROUPS)]
    sc_g = [scores[g * per:(g + 1) * per] for g in range(N_EXPERT_GROUPS)]

    gsc = jnp.zeros((N_EXPERT_GROUPS, tr), F32)
    for g in range(N_EXPERT_GROUPS):
        top1 = jnp.max(grp[g], axis=0, keepdims=True)
        idx1 = jnp.min(jnp.where(grp[g] == top1, iota, per), axis=0, keepdims=True)
        top2 = jnp.max(jnp.where(iota == idx1, NEG, grp[g]), axis=0, keepdims=True)
        gsc = jnp.where(iota == g, top1 + top2, gsc)

    gsel = jnp.zeros((N_EXPERT_GROUPS, tr), F32)
    cur = gsc
    for _ in range(TOPK_GROUPS):
        mx = jnp.max(cur, axis=0, keepdims=True)
        idx = jnp.min(jnp.where(cur == mx, iota, N_EXPERT_GROUPS), axis=0, keepdims=True)
        hit = iota == idx
        gsel = jnp.where(hit, 1.0, gsel)
        cur = jnp.where(hit, NEG, cur)

    cur_g = [jnp.where(gsel[g:g + 1] > 0.0, grp[g], NEG) for g in range(N_EXPERT_GROUPS)]
    idx_rows = jnp.zeros((TOP_K, tr), jnp.int32)
    w_rows = jnp.zeros((TOP_K, tr), F32)
    for kk in range(TOP_K):
        mx = cur_g[0]
        for g in range(1, N_EXPERT_GROUPS):
            mx = jnp.maximum(mx, cur_g[g])
        mx = jnp.max(mx, axis=0, keepdims=True)
        idx = jnp.where(cur_g[0] == mx, iota, N_EXPERTS)
        for g in range(1, N_EXPERT_GROUPS):
            idx = jnp.minimum(idx, jnp.where(cur_g[g] == mx, iota + g * per, N_EXPERTS))
        idx = jnp.min(idx, axis=0, keepdims=True)
        wk = jnp.zeros((per, tr), F32)
        for g in range(N_EXPERT_GROUPS):
            hit = (iota + g * per) == idx
            wk = wk + jnp.where(hit, sc_g[g], 0.0)
            cur_g[g] = jnp.where(hit, NEG, cur_g[g])
        idx_rows = jnp.where(iota == kk, idx, idx_rows)
        w_rows = jnp.where(iota == kk, jnp.sum(wk, axis=0, keepdims=True), w_rows)

    wsum = jnp.sum(w_rows, axis=0, keepdims=True)
    idx_ref[...] = idx_rows
    w_ref[...] = w_rows / wsum * ROUTED_SCALE


def _router(logits, router_bias, seq):
    t = logits.shape[0]
    tr = min(1024, seq)
    bias_b = jnp.broadcast_to(router_bias.astype(F32)[:, None], (N_EXPERTS, tr))
    return pl.pallas_call(
        _router_body,
        grid=(t // tr,),
        in_specs=[pl.BlockSpec((tr, LANES), lambda i: (i, 0)),
                  pl.BlockSpec((N_EXPERTS, tr), lambda i: (0, 0))],
        out_specs=[pl.BlockSpec((TOP_K, tr), lambda i: (0, i)),
                   pl.BlockSpec((TOP_K, tr), lambda i: (0, i))],
        out_shape=[jax.ShapeDtypeStruct((TOP_K, t), jnp.int32),
                   jax.ShapeDtypeStruct((TOP_K, t), F32)],
        compiler_params=_params(("arbitrary",)),
        name="router",
    )(logits, bias_b)


def _route_tables(idx_t, w_t, groups, tg, rows, ntiles):
    i32 = jnp.int32
    e = idx_t.reshape(TOP_K, groups, tg).transpose(1, 0, 2).reshape(groups, TOP_K * tg)
    w = w_t.reshape(TOP_K, groups, tg).transpose(1, 0, 2).reshape(groups, TOP_K * tg)
    tl = jnp.broadcast_to(jnp.arange(tg, dtype=i32)[None, None, :], (groups, TOP_K, tg)).reshape(groups, TOP_K * tg)
    skey, sw = lax.sort((e * tg + tl, w), dimension=1, num_keys=1)
    stok = skey % tg
    cnt = jnp.sum((e[:, :, None] == jnp.arange(N_EXPERTS, dtype=i32)[None, None, :]).astype(i32), axis=1)
    off = jnp.cumsum(cnt, axis=1) - cnt
    ntile = (cnt + rows - 1) // rows
    tend = jnp.cumsum(ntile, axis=1)
    tstart = tend - ntile
    n = jnp.arange(ntiles, dtype=i32)
    te = jnp.sum((n[None, :, None] >= tend[:, None, :]).astype(i32), axis=2)
    valid_tile = n[None, :] < tend[:, -1:]
    te = jnp.minimum(te, N_EXPERTS - 1)
    k0 = (n[None, :] - jnp.take_along_axis(tstart, te, axis=1)) * rows
    nvalid = jnp.where(valid_tile, jnp.clip(jnp.take_along_axis(cnt, te, axis=1) - k0, 0, rows), 0)
    r = jnp.arange(rows, dtype=i32)
    j = (jnp.take_along_axis(off, te, axis=1) + k0)[:, :, None] + r[None, None, :]
    rvalid = r[None, None, :] < nvalid[:, :, None]
    jc = jnp.clip(j, 0, TOP_K * tg - 1).reshape(groups, ntiles * rows, 1)
    pair = jnp.stack([stok, lax.bitcast_convert_type(sw, i32)], axis=-1)
    picked = jnp.take_along_axis(pair, jc, axis=1).reshape(groups, ntiles, rows, 2)
    tok = picked[..., 0]
    wv = lax.bitcast_convert_type(picked[..., 1], F32)
    tok_g = jnp.where(rvalid, tok, 0)
    tok_s = jnp.where(rvalid, tok, tg)
    wv = jnp.where(rvalid, wv, 0.0)
    return te, nvalid, tok_g, tok_s, wv


def _routed_body(te_ref, nv_ref, tokg_ref, toks_ref, wt_ref, h_ref, wg_ref, wu_ref, wd_ref, acc_ref,
                 xbuf, ybuf):
    g = pl.program_id(0)
    n = pl.program_id(1)
    rows = MOE_ROWS
    slot = n % 2
    has_cur = nv_ref[g, n + 1] > 0
    has_prev = nv_ref[g, n] > 0

    @pl.when(n == 0)
    def _():
        acc_ref[...] = jnp.zeros(acc_ref.shape, F32)

    def compute():
        for r in range(rows):
            t8 = pl.multiple_of(tokg_ref[0, 0, 0, r], SUB)
            xbuf[pl.ds(r * SUB, SUB), :] = h_ref[pl.ds(t8, SUB), :]
        x2d = jnp.concatenate([xbuf[pl.ds(s, rows, stride=SUB), :] for s in range(SUB)], axis=1).astype(BF16)
        a = _dot(x2d, wg_ref[0].astype(BF16))
        b = _dot(x2d, wu_ref[0].astype(BF16))
        act = (a * _sigmoid(a) * b).astype(BF16)
        y = _dot(act, wd_ref[0].astype(BF16))
        for s in range(SUB):
            ybuf[slot, pl.ds(s, rows, stride=SUB), :] = y[:, s * LANES:(s + 1) * LANES]

    def scatter_prev():
        for c in range(rows // SUB):
            upd = []
            for r in range(c * SUB, (c + 1) * SUB):
                t8 = pl.multiple_of(toks_ref[0, 0, 0, r], SUB)
                upd.append((t8, acc_ref[0, pl.ds(t8, SUB), :]
                            + wt_ref[0, 0, 0, r] * ybuf[1 - slot, pl.ds(r * SUB, SUB), :]))
            for t8, v in upd:
                acc_ref[0, pl.ds(t8, SUB), :] = v

    @pl.when(jnp.logical_and(has_cur, has_prev))
    def _():
        scatter_prev()
        compute()

    @pl.when(jnp.logical_and(has_cur, jnp.logical_not(has_prev)))
    def _():
        compute()

    @pl.when(jnp.logical_and(jnp.logical_not(has_cur), has_prev))
    def _():
        scatter_prev()


def _routed(hv, tables, w_eg, w_eu, w_ed, groups, tg, ntiles):
    te, nvalid, tok_g, tok_s, wv = tables
    d, f = D_MODEL, EXPERT_DIM
    rows = MOE_ROWS
    i32 = jnp.int32
    te = jnp.concatenate([te, te[:, -1:]], axis=1)
    zero = jnp.zeros((groups, 1), i32)
    nvp = jnp.concatenate([zero, nvalid, zero], axis=1)
    tok_g, tok_s = (a * SUB for a in (tok_g, tok_s))
    tok_g, tok_s, wv = (a.reshape(groups, ntiles, 1, rows) for a in (tok_g, tok_s, wv))
    cur = lambda g, n, te, nv: (g, jnp.minimum(n, ntiles - 1), 0, 0)
    prev = lambda g, n, te, nv: (g, jnp.maximum(n - 1, 0), 0, 0)
    smem = functools.partial(pl.BlockSpec, (1, 1, 1, rows), memory_space=pltpu.SMEM)
    grid_spec = pltpu.PrefetchScalarGridSpec(
        num_scalar_prefetch=2,
        grid=(groups, ntiles + 1),
        in_specs=[smem(index_map=cur), smem(index_map=prev), smem(index_map=prev),
                  pl.BlockSpec((tg * SUB, LANES), lambda g, n, te, nv: (g, 0), pipeline_mode=pl.Buffered(1)),
                  pl.BlockSpec((1, d, f), lambda g, n, te, nv: (te[g, n], 0, 0)),
                  pl.BlockSpec((1, d, f), lambda g, n, te, nv: (te[g, n], 0, 0)),
                  pl.BlockSpec((1, f, d), lambda g, n, te, nv: (te[g, n], 0, 0))],
        out_specs=pl.BlockSpec((1, (tg + 1) * SUB, LANES), lambda g, n, te, nv: (g, 0, 0),
                               pipeline_mode=pl.Buffered(1)),
        scratch_shapes=[pltpu.VMEM((rows * SUB, LANES), F32), pltpu.VMEM((2, rows * SUB, LANES), F32)],
    )
    return pl.pallas_call(
        _routed_body,
        grid_spec=grid_spec,
        out_shape=jax.ShapeDtypeStruct((groups, (tg + 1) * SUB, LANES), F32),
        compiler_params=_params(("arbitrary", "arbitrary")),
        name="routed",
    )(te, nvp, tok_g, tok_s, wv, hv, w_eg, w_eu, w_ed)


def _combine_body(h_ref, r_ref, x1_ref, g2_ref, sg_ref, su_ref, sd_ref, o_ref):
    tm = x1_ref.shape[0]
    h = jnp.concatenate([h_ref[pl.ds(s, tm, stride=SUB), :] for s in range(SUB)], axis=1)
    routed = jnp.concatenate([r_ref[0, pl.ds(s, tm, stride=SUB), :] for s in range(SUB)], axis=1)
    hb = h.astype(BF16)
    a = _dot(hb, sg_ref[...])
    b = _dot(hb, su_ref[...])
    shared = _dot((a * _sigmoid(a) * b).astype(BF16), sd_ref[...])
    o_ref[...] = x1_ref[...] + g2_ref[0] * (routed + shared)


def _combine(hv, routed, x1, g2, w_sg, w_su, w_sd, seq, tg):
    t, d = x1.shape
    tm = min(1024, seq, tg)
    per_b = seq // tm
    per_g = tg // tm
    f = w_sg.shape[1]
    const = lambda i: (0, 0)
    row = lambda i: (i, 0)
    return pl.pallas_call(
        _combine_body,
        grid=(t // tm,),
        in_specs=[pl.BlockSpec((tm * SUB, LANES), row),
                  pl.BlockSpec((1, tm * SUB, LANES), lambda i: (i // per_g, i % per_g, 0)),
                  pl.BlockSpec((tm, d), row),
                  pl.BlockSpec((1, 1, d), lambda i: (i // per_b, 0, 0)),
                  pl.BlockSpec((d, f), const),
                  pl.BlockSpec((d, f), const),
                  pl.BlockSpec((f, d), const)],
        out_specs=pl.BlockSpec((tm, d), row),
        out_shape=jax.ShapeDtypeStruct((t, d), F32),
        compiler_params=_params(("arbitrary",)),
        name="combine",
    )(hv, routed, x1, g2, w_sg, w_su, w_sd)


def kernel(x, c, positions, w_ada, b_ada, norm1_g, w_in, q_norm_g, k_norm_g, lambda_qk, subln_g, conv_w, conv_b, dt_bias, a_log, d_skip, ssd_norm_g, w_branch_a, w_branch_b, w_out, norm2_g, w_router, router_bias, w_exp_gate, w_exp_up, w_exp_down, w_sh_gate, w_sh_up, w_sh_down):
    batch, seq, d = x.shape
    t = batch * seq
    tq = min(1024, seq)
    tk = tq // 2
    xf = x.reshape(t, d)
    for layer in range(w_ada.shape[0]):
        lambda_init = 0.8 - 0.6 * math.exp(-0.3 * layer)
        c_pad = jnp.pad(c.astype(F32), ((0, 8 - batch), (0, 0)))
        mod = _ada(c_pad, w_ada[layer], b_ada[layer])[:batch].reshape(batch, 6, 1, d)
        sh1, sc1, g1, sh2, sc2, g2 = (mod[:, n] for n in range(6))

        w_l = w_in[layer]
        wdt = jnp.pad(w_l[:, MAIN_COLS:], ((0, 0), (0, LANES - SSD_HEADS)))
        wdt_hi, wdt_lo = _split2(wdt)
        proj, dt = _inproj(xf, norm1_g[layer], sc1, sh1, w_in, layer, wdt_hi, wdt_lo, seq)

        qt, kk, vt = _qkvprep(proj, positions, q_norm_g[layer], k_norm_g[layer], batch, seq, tk)
        ya = _attention(qt, kk, vt, lambda_qk[layer], subln_g[layer], lambda_init, batch, seq, tq, tk)
        yb = _ssd(proj, dt, conv_w[layer], conv_b[layer], dt_bias[layer], a_log[layer], d_skip[layer],
                  ssd_norm_g[layer], batch, seq)

        wr = jnp.pad(w_router[layer], ((0, 0), (0, LANES - N_EXPERTS))).astype(BF16)
        x1, h2, logits = _outproj(ya, yb, proj, xf, g1, sc2, sh2, norm2_g[layer],
                                  w_branch_a[layer].astype(BF16), w_branch_b[layer].astype(BF16),
                                  w_out[layer].astype(BF16), wr, seq)
        idx_t, w_t = _router(logits, router_bias[layer], seq)
        tg = min(4096, t)
        groups = t // tg
        ntiles = -(-tg * TOP_K // MOE_ROWS) + N_EXPERTS
        tables = _route_tables(idx_t, w_t, groups, tg, MOE_ROWS, ntiles)
        routed = _routed(h2, tables, w_exp_gate[layer], w_exp_up[layer], w_exp_down[layer], groups, tg, ntiles)
        xf = _combine(h2, routed, x1, g2, w_sh_gate[layer].astype(BF16), w_sh_up[layer].astype(BF16),
                      w_sh_down[layer].astype(BF16), seq, tg)
    return xf.reshape(batch, seq, d)
```

```python
import functools
import math

import jax
import jax.numpy as jnp
from jax import lax
from jax.experimental import pallas as pl
from jax.experimental.pallas import tpu as pltpu

F32 = jnp.float32
BF16 = jnp.bfloat16

D_MODEL = 1024
ATTN_HEADS = 8
ATTN_HEAD_DIM = 64
ATTN_V_DIM = 128
ROPE_THETA = 10000.0
SSD_D_INNER = 2048
SSD_HEAD_DIM = 64
SSD_HEADS = 32
SSD_GROUPS = 4
SSD_D_STATE = 128
SSD_CONV = 4
SSD_CHUNK = 128
SSD_CONV_DIM = 3072
N_EXPERTS = 64
TOP_K = 8
N_EXPERT_GROUPS = 8
TOPK_GROUPS = 4
EXPERT_DIM = 256
ROUTED_SCALE = 2.5
EPS = 1e-6
NEG = -1e30
LOG2E = math.log2(math.e)
Q_SCALE = ATTN_HEAD_DIM ** -0.5 * LOG2E
SSD_HALO = 16

LANES = 128
SUB = 8
MOE_ROWS = 576
MAIN_COLS = 10240
VMEM_LIMIT = 56 * 1024 * 1024


def _dot(a, b):
    return jnp.dot(a, b, preferred_element_type=F32)


def _sigmoid(x):
    return 1.0 / (1.0 + jnp.exp(-x))


def _split2(a):
    hi = a.astype(BF16)
    lo = (a - hi.astype(F32)).astype(BF16)
    return hi, lo


def _split3(a):
    hi = a.astype(BF16)
    r = a - hi.astype(F32)
    mid = r.astype(BF16)
    lo = (r - mid.astype(F32)).astype(BF16)
    return hi, mid, lo


def _params(sem, vmem=VMEM_LIMIT):
    return pltpu.CompilerParams(dimension_semantics=sem, vmem_limit_bytes=vmem)


def _ada_body(c_ref, w_ref, b_ref, o_ref):
    c = c_ref[...]
    sc = c * _sigmoid(c)
    chi, clo = _split2(sc)
    whi, wlo = _split2(w_ref[...])
    o_ref[...] = _dot(chi, whi) + _dot(chi, wlo) + _dot(clo, whi) + b_ref[...]


def _ada(c_pad, w, b):
    d, n = w.shape
    tn = 1024
    return pl.pallas_call(
        _ada_body,
        grid=(n // tn,),
        in_specs=[pl.BlockSpec((8, d), lambda j: (0, 0)),
                  pl.BlockSpec((d, tn), lambda j: (0, j)),
                  pl.BlockSpec((1, tn), lambda j: (0, j))],
        out_specs=pl.BlockSpec((8, tn), lambda j: (0, j)),
        out_shape=jax.ShapeDtypeStruct((8, n), F32),
        compiler_params=_params(("arbitrary",)),
        name="ada",
    )(c_pad, w, b.reshape(1, n))


def _inproj_body(x_ref, g_ref, sc_ref, sh_ref, w_ref, wdh_ref, wdl_ref, o_ref, dt_ref, h_scr):
    @pl.when(pl.program_id(1) == 0)
    def _():
        x = x_ref[...]
        ms = jnp.mean(x * x, axis=-1, keepdims=True)
        h = x * lax.rsqrt(ms + EPS) * g_ref[...]
        h = h * (1.0 + sc_ref[0]) + sh_ref[0]
        hb, hl = _split2(h)
        h_scr[...] = hb
        dt_ref[...] = _dot(hb, wdh_ref[...]) + _dot(hb, wdl_ref[...]) + _dot(hl, wdh_ref[...])

    o_ref[...] = _dot(h_scr[...], w_ref[...]).astype(BF16)


def _inproj(x2d, g, sc, sh, w_main, wdt_hi, wdt_lo, seq):
    t, d = x2d.shape
    tm = min(1024, seq)
    tn = 1024
    per_b = seq // tm
    return pl.pallas_call(
        _inproj_body,
        grid=(t // tm, MAIN_COLS // tn),
        in_specs=[pl.BlockSpec((tm, d), lambda i, j: (i, 0)),
                  pl.BlockSpec((1, d), lambda i, j: (0, 0)),
                  pl.BlockSpec((1, 1, d), lambda i, j: (i // per_b, 0, 0)),
                  pl.BlockSpec((1, 1, d), lambda i, j: (i // per_b, 0, 0)),
                  pl.BlockSpec((d, tn), lambda i, j: (0, j)),
                  pl.BlockSpec((d, LANES), lambda i, j: (0, 0)),
                  pl.BlockSpec((d, LANES), lambda i, j: (0, 0))],
        out_specs=[pl.BlockSpec((tm, tn), lambda i, j: (i, j)),
                   pl.BlockSpec((tm, LANES), lambda i, j: (i, 0))],
        out_shape=[jax.ShapeDtypeStruct((t, MAIN_COLS), BF16),
                   jax.ShapeDtypeStruct((t, LANES), F32)],
        scratch_shapes=[pltpu.VMEM((tm, d), BF16)],
        compiler_params=_params(("arbitrary", "arbitrary")),
        name="inproj",
    )(x2d, g.reshape(1, d), sc, sh, w_main, wdt_hi, wdt_lo)


def _qkvprep_body(q_ref, k_ref, v_ref, pos_ref, invf_ref, gq_ref, gk_ref, qt_ref, ko_ref, vt_ref, *, tk):
    ts = q_ref.shape[0]
    half = ATTN_HEAD_DIM // 2
    ang = pos_ref[0].astype(F32) * invf_ref[...]
    cos = jnp.cos(ang)
    sin = jnp.sin(ang)
    zeros = jnp.zeros((ATTN_HEAD_DIM, ts), F32)

    def norm_rope(blk, g):
        ms = jnp.mean(blk * blk, axis=0, keepdims=True)
        r = blk * lax.rsqrt(ms + EPS) * g
        x1 = r[:half]
        x2 = r[half:]
        return jnp.concatenate([x1 * cos - x2 * sin, x2 * cos + x1 * sin], axis=0)

    for h in range(ATTN_HEADS):
        sl = slice(h * LANES, (h + 1) * LANES)
        qht = q_ref[:, sl].astype(F32).T
        q1 = norm_rope(qht[:ATTN_HEAD_DIM], gq_ref[...]) * Q_SCALE
        q2 = norm_rope(qht[ATTN_HEAD_DIM:], gq_ref[...]) * Q_SCALE
        qt_ref[0, 2 * h] = jnp.concatenate([q1, zeros], axis=0).astype(BF16)
        qt_ref[0, 2 * h + 1] = jnp.concatenate([zeros, q2], axis=0).astype(BF16)
        kht = k_ref[:, sl].astype(F32).T
        k1 = norm_rope(kht[:ATTN_HEAD_DIM], gk_ref[...])
        k2 = norm_rope(kht[ATTN_HEAD_DIM:], gk_ref[...])
        ko_ref[0, h] = jnp.concatenate([k1, k2], axis=0).T.astype(BF16)
        vht = v_ref[:, sl].astype(F32).T.astype(BF16)
        for cc in range(ts // tk):
            vt_ref[0, h, cc] = vht[:, cc * tk:(cc + 1) * tk]


def _qkvprep(proj, positions, q_norm_g, k_norm_g, batch, seq, tk):
    ts = min(512, seq)
    ns = seq // ts
    half = ATTN_HEAD_DIM // 2
    inv_freq = ROPE_THETA ** (-jnp.arange(half, dtype=F32) * 2.0 / ATTN_HEAD_DIM)
    invf_b = jnp.broadcast_to(inv_freq[:, None], (half, ts))
    gq_b = jnp.broadcast_to(q_norm_g.astype(F32)[:, None], (ATTN_HEAD_DIM, ts))
    gk_b = jnp.broadcast_to(k_norm_g.astype(F32)[:, None], (ATTN_HEAD_DIM, ts))
    pos3 = positions.reshape(batch, 1, seq)
    const = lambda b, i: (0, 0)
    return pl.pallas_call(
        functools.partial(_qkvprep_body, tk=tk),
        grid=(batch, ns),
        in_specs=[pl.BlockSpec((ts, D_MODEL), lambda b, i: (b * ns + i, 0)),
                  pl.BlockSpec((ts, D_MODEL), lambda b, i: (b * ns + i, 1)),
                  pl.BlockSpec((ts, D_MODEL), lambda b, i: (b * ns + i, 2)),
                  pl.BlockSpec((1, 1, ts), lambda b, i: (b, 0, i)),
                  pl.BlockSpec((half, ts), const),
                  pl.BlockSpec((ATTN_HEAD_DIM, ts), const),
                  pl.BlockSpec((ATTN_HEAD_DIM, ts), const)],
        out_specs=[pl.BlockSpec((1, 2 * ATTN_HEADS, LANES, ts), lambda b, i: (b, 0, 0, i)),
                   pl.BlockSpec((1, ATTN_HEADS, ts, LANES), lambda b, i: (b, 0, i, 0)),
                   pl.BlockSpec((1, ATTN_HEADS, ts // tk, LANES, tk), lambda b, i: (b, 0, i, 0, 0))],
        out_shape=[jax.ShapeDtypeStruct((batch, 2 * ATTN_HEADS, LANES, seq), BF16),
                   jax.ShapeDtypeStruct((batch, ATTN_HEADS, seq, LANES), BF16),
                   jax.ShapeDtypeStruct((batch, ATTN_HEADS, seq // tk, LANES, tk), BF16)],
        compiler_params=_params(("arbitrary", "arbitrary")),
        name="qkvprep",
    )(proj, proj, proj, pos3, invf_b, gq_b, gk_b)


def _attn_body(lam_ref, sg_ref, qt_ref, k_ref, vt_ref, o_ref, acc_scr, sa_scr, sb_scr, *, tq, tk, lambda_init):
    i = pl.program_id(2)
    q_maps = (qt_ref[0, 0], qt_ref[0, 1])
    acc_scr[...] = jnp.zeros(acc_scr.shape, F32)
    tri = lax.broadcasted_iota(jnp.int32, (tk, tk), 0) <= lax.broadcasted_iota(jnp.int32, (tk, tk), 1)

    def qk(j, s_scr, lo=0):
        kb = k_ref[0, 0, pl.ds(pl.multiple_of(j * tk, tk), tk), :]
        for mi in range(2):
            s_scr[mi, :, lo:] = _dot(kb, q_maps[mi][:, lo:])

    def process(j, s_scr, carry, diag=None, lo=0):
        vb = vt_ref[0, 0, j]
        w = tq - lo
        out = []
        for mi in range(2):
            m_old, l_old = carry[2 * mi][:, lo:], carry[2 * mi + 1][:, lo:]
            if diag is None:
                s = s_scr[mi]
            elif diag == 0:
                s = jnp.concatenate([jnp.where(tri, s_scr[mi, :, :tk], NEG), s_scr[mi, :, tk:]], axis=1)
            else:
                s = jnp.where(tri, s_scr[mi, :, lo:], NEG)
            m_new = jnp.maximum(m_old, jnp.max(s, axis=0, keepdims=True))
            alpha = jnp.exp2(m_old - m_new)
            p = jnp.exp2(s - m_new[0:1])
            l_new = alpha * l_old + jnp.sum(p, axis=0, keepdims=True)
            acc_scr[mi, :, lo:] = (jnp.broadcast_to(alpha[0:1], (ATTN_V_DIM, w)) * acc_scr[mi, :, lo:]
                                   + _dot(vb, p.astype(BF16)))
            if lo:
                m_new = jnp.concatenate([carry[2 * mi][:, :lo], m_new], axis=1)
                l_new = jnp.concatenate([carry[2 * mi + 1][:, :lo], l_new], axis=1)
            out += [m_new, l_new]
        return tuple(out)

    def pair(p, carry):
        qk(2 * p + 1, sb_scr)
        carry = process(2 * p, sa_scr, carry)
        qk(2 * p + 2, sa_scr)
        return process(2 * p + 1, sb_scr, carry)

    init = (jnp.full((SUB, tq), NEG, F32), jnp.zeros((SUB, tq), F32)) * 2
    qk(0, sa_scr)
    carry = lax.fori_loop(0, i, pair, init)
    qk(2 * i + 1, sb_scr, lo=tk)
    carry = process(2 * i, sa_scr, carry, diag=0)
    carry = process(2 * i + 1, sb_scr, carry, diag=1, lo=tk)

    lv = lam_ref[...]
    lam = (jnp.exp(jnp.sum(lv[0:1] * lv[1:2], keepdims=True))
           - jnp.exp(jnp.sum(lv[2:3] * lv[3:4], keepdims=True)) + lambda_init)
    o = acc_scr[0] / carry[1][0:1] - lam * (acc_scr[1] / carry[3][0:1])
    ms = jnp.mean(o * o, axis=0, keepdims=True)
    o = o * lax.rsqrt(ms + EPS) * sg_ref[...] * (1.0 - lambda_init)
    o_ref[...] = o.T.astype(BF16)


def _attention(qt, kk, vt, lambda_qk, subln_g, lambda_init, batch, seq, tq, tk):
    nq = seq // tq
    sg_b = jnp.broadcast_to(subln_g.astype(F32)[:, None], (ATTN_V_DIM, tq))
    return pl.pallas_call(
        functools.partial(_attn_body, tq=tq, tk=tk, lambda_init=lambda_init),
        grid=(batch, ATTN_HEADS, nq),
        in_specs=[pl.BlockSpec((4, ATTN_HEAD_DIM), lambda b, h, i: (0, 0)),
                  pl.BlockSpec((ATTN_V_DIM, tq), lambda b, h, i: (0, 0)),
                  pl.BlockSpec((1, 2, LANES, tq), lambda b, h, i: (b, h, 0, i)),
                  pl.BlockSpec((1, 1, seq, LANES), lambda b, h, i: (b, h, 0, 0)),
                  pl.BlockSpec((1, 1, seq // tk, LANES, tk), lambda b, h, i: (b, h, 0, 0, 0))],
        out_specs=pl.BlockSpec((tq, LANES), lambda b, h, i: (b * nq + i, h)),
        out_shape=jax.ShapeDtypeStruct((batch * seq, ATTN_HEADS * ATTN_V_DIM), BF16),
        scratch_shapes=[pltpu.VMEM((2, ATTN_V_DIM, tq), F32),
                        pltpu.VMEM((2, tk, tq), F32),
                        pltpu.VMEM((2, tk, tq), F32)],
        compiler_params=_params(("arbitrary", "arbitrary", "arbitrary")),
        name="attn",
    )(lambda_qk.astype(F32), sg_b, qt, kk, vt)


def _ssd_body(z0_ref, z1_ref, x0_ref, x1_ref, bc_ref, dt_ref, cw_ref, cb_ref, dtb_ref, alog_ref,
              dsk_ref, ng_ref, tri_ref, shift_ref, y_ref, ext_scr, st_scr):
    L = SSD_CHUNK
    halo = SSD_HALO
    c = pl.program_id(1)

    @pl.when(c == 0)
    def _():
        ext_scr[0:halo, :] = jnp.zeros((halo, SSD_CONV_DIM), BF16)
        st_scr[...] = jnp.zeros(st_scr.shape, F32)

    ext_scr[halo:halo + L, 0:1024] = x0_ref[...]
    ext_scr[halo:halo + L, 1024:2048] = x1_ref[...]
    ext_scr[halo:halo + L, 2048:3072] = bc_ref[...]
    sh = _dot(shift_ref[...], ext_scr[...])
    acc = cb_ref[...] + cw_ref[SSD_CONV - 1:SSD_CONV, :] * ext_scr[halo:halo + L, :].astype(F32)
    for k in range(SSD_CONV - 1):
        acc = acc + cw_ref[k:k + 1, :] * sh[k * L:(k + 1) * L]
    u = acc * _sigmoid(acc)
    ext_scr[0:halo, :] = ext_scr[L:L + halo, :]

    xdt_in = dt_ref[...] + dtb_ref[...]
    dtp = jnp.maximum(xdt_in, 0.0) + jnp.log(1.0 + jnp.exp(-jnp.abs(xdt_in)))
    a = dtp * (-jnp.exp(alog_ref[...]) * LOG2E)
    tri = tri_ref[...]
    a_hi, a_mid, a_lo = _split3(a)
    acum = _dot(tri, a_hi) + _dot(tri, a_mid) + _dot(tri, a_lo)
    acum_t = acum.T
    lane = lax.broadcasted_iota(jnp.int32, (L, LANES), 1)
    first = lane < SSD_HEAD_DIM
    tril = lax.broadcasted_iota(jnp.int32, (L, L), 0) >= lax.broadcasted_iota(jnp.int32, (L, L), 1)
    gn = SSD_GROUPS * SSD_D_STATE
    gw = SSD_D_INNER // SSD_GROUPS

    def col(mat, h):
        return jnp.broadcast_to(mat[:, h:h + 1], (L, LANES))

    for g in range(SSD_GROUPS):
        bg = u[:, SSD_D_INNER + g * SSD_D_STATE:SSD_D_INNER + (g + 1) * SSD_D_STATE]
        cg = u[:, SSD_D_INNER + gn + g * SSD_D_STATE:SSD_D_INNER + gn + (g + 1) * SSD_D_STATE].astype(BF16)
        bgt = bg.T.astype(BF16)
        cb = _dot(cg, bgt)
        sprev = st_scr[g]
        yoff = _dot(cg, sprev.astype(BF16))
        y_parts, xdd_parts, dec_parts = [], [], []
        for p in range(4):
            h0 = 8 * g + 2 * p
            c0, c1 = col(acum, h0), col(acum, h0 + 1)
            ac_exp = jnp.where(first, c0, c1)
            dt_exp = jnp.where(first, col(dtp, h0), col(dtp, h0 + 1))
            xsl = u[:, h0 * SSD_HEAD_DIM:h0 * SSD_HEAD_DIM + LANES]
            xdt = xsl * dt_exp
            l0 = jnp.exp2(jnp.where(tril, c0 - acum_t[h0:h0 + 1, :], NEG))
            l1 = jnp.exp2(jnp.where(tril, c1 - acum_t[h0 + 1:h0 + 2, :], NEG))
            m2 = jnp.concatenate([(cb * l0).astype(BF16), (cb * l1).astype(BF16)], axis=1)
            xb = xdt.astype(BF16)
            zb = jnp.zeros_like(xb)
            rhs = jnp.concatenate([jnp.where(first, xb, zb), jnp.where(first, zb, xb)], axis=0)
            yd = _dot(m2, rhs)
            yo = yoff[:, p * LANES:(p + 1) * LANES] * jnp.exp2(ac_exp)
            dsk = dsk_ref[:, h0 * SSD_HEAD_DIM:h0 * SSD_HEAD_DIM + LANES]
            y_parts.append(yd + yo + xsl * dsk)
            last = ac_exp[L - 1:L, :]
            xdd_parts.append((xdt * jnp.exp2(last - ac_exp)).astype(BF16))
            dec_parts.append(jnp.exp2(last))
        xdd = jnp.concatenate(xdd_parts, axis=1)
        dec = jnp.concatenate(dec_parts, axis=1)
        st_scr[g] = sprev * dec + _dot(bgt, xdd)
        yg = jnp.concatenate(y_parts, axis=1)
        zref = z0_ref if g < 2 else z1_ref
        zg = zref[:, (g % 2) * gw:(g % 2 + 1) * gw].astype(F32)
        yg = yg * (zg * _sigmoid(zg))
        ms = jnp.mean(yg * yg, axis=-1, keepdims=True)
        yg = yg * lax.rsqrt(ms + EPS) * ng_ref[:, g * gw:(g + 1) * gw]
        y_ref[:, g * gw:(g + 1) * gw] = yg.astype(BF16)


def _ssd(proj, dt, conv_w, conv_b, dt_bias, a_log, d_skip, norm_g, batch, seq):
    L = SSD_CHUNK
    nc = seq // L
    pad = LANES - SSD_HEADS
    dtb = jnp.pad(dt_bias.astype(F32), (0, pad)).reshape(1, LANES)
    alog = jnp.pad(a_log.astype(F32), (0, pad)).reshape(1, LANES)
    dsk = jnp.repeat(d_skip.astype(F32), SSD_HEAD_DIM).reshape(1, SSD_D_INNER)
    tri = jnp.tril(jnp.ones((L, L), F32)).astype(BF16)
    tt = jnp.arange(L)
    shift = jnp.concatenate(
        [jax.nn.one_hot(SSD_HALO + tt - (SSD_CONV - 1) + k, L + SSD_HALO, dtype=BF16) for k in range(SSD_CONV - 1)],
        axis=0)
    row = lambda b, c: (b * nc + c)
    const = lambda b, c: (0, 0)
    return pl.pallas_call(
        _ssd_body,
        grid=(batch, nc),
        in_specs=[pl.BlockSpec((L, 1024), lambda b, c: (row(b, c), 5)),
                  pl.BlockSpec((L, 1024), lambda b, c: (row(b, c), 6)),
                  pl.BlockSpec((L, 1024), lambda b, c: (row(b, c), 7)),
                  pl.BlockSpec((L, 1024), lambda b, c: (row(b, c), 8)),
                  pl.BlockSpec((L, 1024), lambda b, c: (row(b, c), 9)),
                  pl.BlockSpec((L, LANES), lambda b, c: (row(b, c), 0)),
                  pl.BlockSpec((SSD_CONV, SSD_CONV_DIM), const),
                  pl.BlockSpec((1, SSD_CONV_DIM), const),
                  pl.BlockSpec((1, LANES), const),
                  pl.BlockSpec((1, LANES), const),
                  pl.BlockSpec((1, SSD_D_INNER), const),
                  pl.BlockSpec((1, SSD_D_INNER), const),
                  pl.BlockSpec((L, L), const),
                  pl.BlockSpec(((SSD_CONV - 1) * L, L + SSD_HALO), const)],
        out_specs=pl.BlockSpec((L, SSD_D_INNER), lambda b, c: (row(b, c), 0)),
        out_shape=jax.ShapeDtypeStruct((batch * seq, SSD_D_INNER), BF16),
        scratch_shapes=[pltpu.VMEM((L + SSD_HALO, SSD_CONV_DIM), BF16),
                        pltpu.VMEM((SSD_GROUPS, SSD_D_STATE, SSD_D_INNER // SSD_GROUPS), F32)],
        compiler_params=_params(("arbitrary", "arbitrary")),
        name="ssd",
    )(proj, proj, proj, proj, proj, dt, conv_w.astype(F32), conv_b.astype(F32).reshape(1, -1), dtb, alog,
      dsk, norm_g.astype(F32).reshape(1, -1), tri, shift)


def _outproj_body(ya_ref, yb_ref, ga_ref, gb_ref, x_ref, g1_ref, sc_ref, sh_ref, n2_ref,
                  wa_ref, wb_ref, wo_ref, wr_ref, x1_ref, h2_ref, lg_ref):
    pa = _dot(ya_ref[...], wa_ref[...])
    pb = _dot(yb_ref[...], wb_ref[...])
    merged = _sigmoid(ga_ref[...].astype(F32)) * pa + _sigmoid(gb_ref[...].astype(F32)) * pb
    x1 = x_ref[...] + g1_ref[0] * _dot(merged.astype(BF16), wo_ref[...])
    x1_ref[...] = x1
    ms = jnp.mean(x1 * x1, axis=-1, keepdims=True)
    h2 = x1 * lax.rsqrt(ms + EPS) * n2_ref[...]
    h2 = h2 * (1.0 + sc_ref[0]) + sh_ref[0]
    tm = h2.shape[0]
    for s in range(SUB):
        h2_ref[pl.ds(s, tm, stride=SUB), :] = h2[:, s * LANES:(s + 1) * LANES]
    lg_ref[...] = _dot(h2.astype(BF16), wr_ref[...])


def _outproj(ya, yb, proj, x2d, g1, sc2, sh2, norm2_g, wa, wb, wo, wr, seq):
    t, d = x2d.shape
    tm = min(512, seq)
    per_b = seq // tm
    const = lambda i: (0, 0)
    mod = lambda i: (i // per_b, 0, 0)
    return pl.pallas_call(
        _outproj_body,
        grid=(t // tm,),
        in_specs=[pl.BlockSpec((tm, d), lambda i: (i, 0)),
                  pl.BlockSpec((tm, SSD_D_INNER), lambda i: (i, 0)),
                  pl.BlockSpec((tm, d), lambda i: (i, 3)),
                  pl.BlockSpec((tm, d), lambda i: (i, 4)),
                  pl.BlockSpec((tm, d), lambda i: (i, 0)),
                  pl.BlockSpec((1, 1, d), mod),
                  pl.BlockSpec((1, 1, d), mod),
                  pl.BlockSpec((1, 1, d), mod),
                  pl.BlockSpec((1, d), const),
                  pl.BlockSpec((d, d), const),
                  pl.BlockSpec((SSD_D_INNER, d), const),
                  pl.BlockSpec((d, d), const),
                  pl.BlockSpec((d, LANES), const)],
        out_specs=[pl.BlockSpec((tm, d), lambda i: (i, 0)),
                   pl.BlockSpec((tm * SUB, LANES), lambda i: (i, 0)),
                   pl.BlockSpec((tm, LANES), lambda i: (i, 0))],
        out_shape=[jax.ShapeDtypeStruct((t, d), F32),
                   jax.ShapeDtypeStruct((t * SUB, LANES), F32),
                   jax.ShapeDtypeStruct((t, LANES), F32)],
        compiler_params=_params(("arbitrary",)),
        name="outproj",
    )(ya, yb, proj, proj, x2d, g1, sc2, sh2, norm2_g.astype(F32).reshape(1, d), wa, wb, wo, wr)


def _router_body(lg_ref, bias_ref, idx_ref, w_ref):
    tr = lg_ref.shape[0]
    per = N_EXPERTS // N_EXPERT_GROUPS
    lt = lg_ref[...].T
    scores = _sigmoid(lt[:N_EXPERTS])
    choice = scores + bias_ref[...]
    iota = lax.broadcasted_iota(jnp.int32, (per, tr), 0)
    grp = [choice[g * per:(g + 1) * per] for g in range(N_EXPERT_GROUPS)]
    sc_g = [scores[g * per:(g + 1) * per] for g in range(N_EXPERT_GROUPS)]

    gsc = jnp.zeros((N_EXPERT_GROUPS, tr), F32)
    for g in range(N_EXPERT_GROUPS):
        top1 = jnp.max(grp[g], axis=0, keepdims=True)
        idx1 = jnp.min(jnp.where(grp[g] == top1, iota, per), axis=0, keepdims=True)
        top2 = jnp.max(jnp.where(iota == idx1, NEG, grp[g]), axis=0, keepdims=True)
        gsc = jnp.where(iota == g, top1 + top2, gsc)

    gsel = jnp.zeros((N_EXPERT_GROUPS, tr), F32)
    cur = gsc
    for _ in range(TOPK_GROUPS):
        mx = jnp.max(cur, axis=0, keepdims=True)
        idx = jnp.min(jnp.where(cur == mx, iota, N_EXPERT_GROUPS), axis=0, keepdims=True)
        hit = iota == idx
        gsel = jnp.where(hit, 1.0, gsel)
        cur = jnp.where(hit, NEG, cur)

    cur_g = [jnp.where(gsel[g:g + 1] > 0.0, grp[g], NEG) for g in range(N_EXPERT_GROUPS)]
    idx_rows = jnp.zeros((TOP_K, tr), jnp.int32)
    w_rows = jnp.zeros((TOP_K, tr), F32)
    for kk in range(TOP_K):
        mx = cur_g[0]
        for g in range(1, N_EXPERT_GROUPS):
            mx = jnp.maximum(mx, cur_g[g])
        mx = jnp.max(mx, axis=0, keepdims=True)
        idx = jnp.where(cur_g[0] == mx, iota, N_EXPERTS)
        for g in range(1, N_EXPERT_GROUPS):
            idx = jnp.minimum(idx, jnp.where(cur_g[g] == mx, iota + g * per, N_EXPERTS))
        idx = jnp.min(idx, axis=0, keepdims=True)
        wk = jnp.zeros((per, tr), F32)
        for g in range(N_EXPERT_GROUPS):
            hit = (iota + g * per) == idx
            wk = wk + jnp.where(hit, sc_g[g], 0.0)
            cur_g[g] = jnp.where(hit, NEG, cur_g[g])
        idx_rows = jnp.where(iota == kk, idx, idx_rows)
        w_rows = jnp.where(iota == kk, jnp.sum(wk, axis=0, keepdims=True), w_rows)

    wsum = jnp.sum(w_rows, axis=0, keepdims=True)
    idx_ref[...] = idx_rows
    w_ref[...] = w_rows / wsum * ROUTED_SCALE


def _router(logits, router_bias, seq):
    t = logits.shape[0]
    tr = min(1024, seq)
    bias_b = jnp.broadcast_to(router_bias.astype(F32)[:, None], (N_EXPERTS, tr))
    return pl.pallas_call(
        _router_body,
        grid=(t // tr,),
        in_specs=[pl.BlockSpec((tr, LANES), lambda i: (i, 0)),
                  pl.BlockSpec((N_EXPERTS, tr), lambda i: (0, 0))],
        out_specs=[pl.BlockSpec((TOP_K, tr), lambda i: (0, i)),
                   pl.BlockSpec((TOP_K, tr), lambda i: (0, i))],
        out_shape=[jax.ShapeDtypeStruct((TOP_K, t), jnp.int32),
                   jax.ShapeDtypeStruct((TOP_K, t), F32)],
        compiler_params=_params(("arbitrary",)),
        name="router",
    )(logits, bias_b)


def _route_tables(idx_t, w_t, groups, tg, rows, ntiles):
    i32 = jnp.int32
    e = idx_t.reshape(TOP_K, groups, tg).transpose(1, 0, 2).reshape(groups, TOP_K * tg)
    w = w_t.reshape(TOP_K, groups, tg).transpose(1, 0, 2).reshape(groups, TOP_K * tg)
    tl = jnp.broadcast_to(jnp.arange(tg, dtype=i32)[None, None, :], (groups, TOP_K, tg)).reshape(groups, TOP_K * tg)
    skey, sw = lax.sort((e * tg + tl, w), dimension=1, num_keys=1)
    stok = skey % tg
    cnt = jnp.sum((e[:, :, None] == jnp.arange(N_EXPERTS, dtype=i32)[None, None, :]).astype(i32), axis=1)
    off = jnp.cumsum(cnt, axis=1) - cnt
    ntile = (cnt + rows - 1) // rows
    tend = jnp.cumsum(ntile, axis=1)
    tstart = tend - ntile
    n = jnp.arange(ntiles, dtype=i32)
    te = jnp.sum((n[None, :, None] >= tend[:, None, :]).astype(i32), axis=2)
    valid_tile = n[None, :] < tend[:, -1:]
    te = jnp.minimum(te, N_EXPERTS - 1)
    k0 = (n[None, :] - jnp.take_along_axis(tstart, te, axis=1)) * rows
    nvalid = jnp.where(valid_tile, jnp.clip(jnp.take_along_axis(cnt, te, axis=1) - k0, 0, rows), 0)
    r = jnp.arange(rows, dtype=i32)
    j = (jnp.take_along_axis(off, te, axis=1) + k0)[:, :, None] + r[None, None, :]
    rvalid = r[None, None, :] < nvalid[:, :, None]
    jc = jnp.clip(j, 0, TOP_K * tg - 1).reshape(groups, ntiles * rows)
    tok = jnp.take_along_axis(stok, jc, axis=1).reshape(groups, ntiles, rows)
    wv = jnp.take_along_axis(sw, jc, axis=1).reshape(groups, ntiles, rows)
    tok_g = jnp.where(rvalid, tok, 0)
    tok_s = jnp.where(rvalid, tok, tg)
    wv = jnp.where(rvalid, wv, 0.0)
    return te, nvalid, tok_g, tok_s, wv


def _routed_body(te_ref, nv_ref, tokg_ref, toks_ref, wt_ref, h_ref, wg_ref, wu_ref, wd_ref, acc_ref,
                 xbuf, ybuf):
    g = pl.program_id(0)
    n = pl.program_id(1)
    rows = MOE_ROWS
    slot = n % 2
    has_cur = nv_ref[g, n + 1] > 0
    has_prev = nv_ref[g, n] > 0

    @pl.when(n == 0)
    def _():
        acc_ref[...] = jnp.zeros(acc_ref.shape, F32)

    def compute():
        for r in range(rows):
            t8 = pl.multiple_of(tokg_ref[0, 0, 0, r], SUB)
            xbuf[pl.ds(r * SUB, SUB), :] = h_ref[pl.ds(t8, SUB), :]
        x2d = jnp.concatenate([xbuf[pl.ds(s, rows, stride=SUB), :] for s in range(SUB)], axis=1).astype(BF16)
        a = _dot(x2d, wg_ref[0].astype(BF16))
        b = _dot(x2d, wu_ref[0].astype(BF16))
        act = (a * _sigmoid(a) * b).astype(BF16)
        y = _dot(act, wd_ref[0].astype(BF16))
        for s in range(SUB):
            ybuf[slot, pl.ds(s, rows, stride=SUB), :] = y[:, s * LANES:(s + 1) * LANES]

    def scatter_prev():
        for c in range(rows // SUB):
            upd = []
            for r in range(c * SUB, (c + 1) * SUB):
                t8 = pl.multiple_of(toks_ref[0, 0, 0, r], SUB)
                upd.append((t8, acc_ref[0, pl.ds(t8, SUB), :]
                            + wt_ref[0, 0, 0, r] * ybuf[1 - slot, pl.ds(r * SUB, SUB), :]))
            for t8, v in upd:
                acc_ref[0, pl.ds(t8, SUB), :] = v

    @pl.when(jnp.logical_and(has_cur, has_prev))
    def _():
        scatter_prev()
        compute()

    @pl.when(jnp.logical_and(has_cur, jnp.logical_not(has_prev)))
    def _():
        compute()

    @pl.when(jnp.logical_and(jnp.logical_not(has_cur), has_prev))
    def _():
        scatter_prev()


def _routed(hv, tables, w_eg, w_eu, w_ed, groups, tg, ntiles):
    te, nvalid, tok_g, tok_s, wv = tables
    d, f = D_MODEL, EXPERT_DIM
    rows = MOE_ROWS
    i32 = jnp.int32
    te = jnp.concatenate([te, te[:, -1:]], axis=1)
    zero = jnp.zeros((groups, 1), i32)
    nvp = jnp.concatenate([zero, nvalid, zero], axis=1)
    tok_g, tok_s = (a * SUB for a in (tok_g, tok_s))
    tok_g, tok_s, wv = (a.reshape(groups, ntiles, 1, rows) for a in (tok_g, tok_s, wv))
    cur = lambda g, n, te, nv: (g, jnp.minimum(n, ntiles - 1), 0, 0)
    prev = lambda g, n, te, nv: (g, jnp.maximum(n - 1, 0), 0, 0)
    smem = functools.partial(pl.BlockSpec, (1, 1, 1, rows), memory_space=pltpu.SMEM)
    grid_spec = pltpu.PrefetchScalarGridSpec(
        num_scalar_prefetch=2,
        grid=(groups, ntiles + 1),
        in_specs=[smem(index_map=cur), smem(index_map=prev), smem(index_map=prev),
                  pl.BlockSpec((tg * SUB, LANES), lambda g, n, te, nv: (g, 0), pipeline_mode=pl.Buffered(1)),
                  pl.BlockSpec((1, d, f), lambda g, n, te, nv: (te[g, n], 0, 0)),
                  pl.BlockSpec((1, d, f), lambda g, n, te, nv: (te[g, n], 0, 0)),
                  pl.BlockSpec((1, f, d), lambda g, n, te, nv: (te[g, n], 0, 0))],
        out_specs=pl.BlockSpec((1, (tg + 1) * SUB, LANES), lambda g, n, te, nv: (g, 0, 0),
                               pipeline_mode=pl.Buffered(1)),
        scratch_shapes=[pltpu.VMEM((rows * SUB, LANES), F32), pltpu.VMEM((2, rows * SUB, LANES), F32)],
    )
    return pl.pallas_call(
        _routed_body,
        grid_spec=grid_spec,
        out_shape=jax.ShapeDtypeStruct((groups, (tg + 1) * SUB, LANES), F32),
        compiler_params=_params(("arbitrary", "arbitrary")),
        name="routed",
    )(te, nvp, tok_g, tok_s, wv, hv, w_eg, w_eu, w_ed)


def _combine_body(h_ref, r_ref, x1_ref, g2_ref, sg_ref, su_ref, sd_ref, o_ref):
    tm = x1_ref.shape[0]
    h = jnp.concatenate([h_ref[pl.ds(s, tm, stride=SUB), :] for s in range(SUB)], axis=1)
    routed = jnp.concatenate([r_ref[0, pl.ds(s, tm, stride=SUB), :] for s in range(SUB)], axis=1)
    hb = h.astype(BF16)
    a = _dot(hb, sg_ref[...])
    b = _dot(hb, su_ref[...])
    shared = _dot((a * _sigmoid(a) * b).astype(BF16), sd_ref[...])
    o_ref[...] = x1_ref[...] + g2_ref[0] * (routed + shared)


def _combine(hv, routed, x1, g2, w_sg, w_su, w_sd, seq, tg):
    t, d = x1.shape
    tm = min(1024, seq, tg)
    per_b = seq // tm
    per_g = tg // tm
    f = w_sg.shape[1]
    const = lambda i: (0, 0)
    row = lambda i: (i, 0)
    return pl.pallas_call(
        _combine_body,
        grid=(t // tm,),
        in_specs=[pl.BlockSpec((tm * SUB, LANES), row),
                  pl.BlockSpec((1, tm * SUB, LANES), lambda i: (i // per_g, i % per_g, 0)),
                  pl.BlockSpec((tm, d), row),
                  pl.BlockSpec((1, 1, d), lambda i: (i // per_b, 0, 0)),
                  pl.BlockSpec((d, f), const),
                  pl.BlockSpec((d, f), const),
                  pl.BlockSpec((f, d), const)],
        out_specs=pl.BlockSpec((tm, d), row),
        out_shape=jax.ShapeDtypeStruct((t, d), F32),
        compiler_params=_params(("arbitrary",)),
        name="combine",
    )(hv, routed, x1, g2, w_sg, w_su, w_sd)


def kernel(x, c, positions, w_ada, b_ada, norm1_g, w_in, q_norm_g, k_norm_g, lambda_qk, subln_g, conv_w, conv_b, dt_bias, a_log, d_skip, ssd_norm_g, w_branch_a, w_branch_b, w_out, norm2_g, w_router, router_bias, w_exp_gate, w_exp_up, w_exp_down, w_sh_gate, w_sh_up, w_sh_down):
    batch, seq, d = x.shape
    t = batch * seq
    tq = min(1024, seq)
    tk = tq // 2
    xf = x.reshape(t, d)
    for layer in range(w_ada.shape[0]):
        lambda_init = 0.8 - 0.6 * math.exp(-0.3 * layer)
        c_pad = jnp.pad(c.astype(F32), ((0, 8 - batch), (0, 0)))
        mod = _ada(c_pad, w_ada[layer], b_ada[layer])[:batch].reshape(batch, 6, 1, d)
        sh1, sc1, g1, sh2, sc2, g2 = (mod[:, n] for n in range(6))

        w_l = w_in[layer]
        wdt = jnp.pad(w_l[:, MAIN_COLS:], ((0, 0), (0, LANES - SSD_HEADS)))
        wdt_hi, wdt_lo = _split2(wdt)
        proj, dt = _inproj(xf, norm1_g[layer], sc1, sh1, w_l[:, :MAIN_COLS].astype(BF16), wdt_hi, wdt_lo, seq)

        qt, kk, vt = _qkvprep(proj, positions, q_norm_g[layer], k_norm_g[layer], batch, seq, tk)
        ya = _attention(qt, kk, vt, lambda_qk[layer], subln_g[layer], lambda_init, batch, seq, tq, tk)
        yb = _ssd(proj, dt, conv_w[layer], conv_b[layer], dt_bias[layer], a_log[layer], d_skip[layer],
                  ssd_norm_g[layer], batch, seq)

        wr = jnp.pad(w_router[layer], ((0, 0), (0, LANES - N_EXPERTS))).astype(BF16)
        x1, h2, logits = _outproj(ya, yb, proj, xf, g1, sc2, sh2, norm2_g[layer],
                                  w_branch_a[layer].astype(BF16), w_branch_b[layer].astype(BF16),
                                  w_out[layer].astype(BF16), wr, seq)
        idx_t, w_t = _router(logits, router_bias[layer], seq)
        tg = min(4096, t)
        groups = t // tg
        ntiles = -(-tg * TOP_K // MOE_ROWS) + N_EXPERTS
        tables = _route_tables(idx_t, w_t, groups, tg, MOE_ROWS, ntiles)
        routed = _routed(h2, tables, w_exp_gate[layer], w_exp_up[layer], w_exp_down[layer], groups, tg, ntiles)
        xf = _combine(h2, routed, x1, g2, w_sh_gate[layer].astype(BF16), w_sh_up[layer].astype(BF16),
                      w_sh_down[layer].astype(BF16), seq, tg)
    return xf.reshape(batch, seq, d)
```

```python
import functools
import math

import jax
import jax.numpy as jnp
from jax import lax
from jax.experimental import pallas as pl
from jax.experimental.pallas import tpu as pltpu

F32 = jnp.float32
BF16 = jnp.bfloat16

D_MODEL = 1024
ATTN_HEADS = 8
ATTN_HEAD_DIM = 64
ATTN_V_DIM = 128
ROPE_THETA = 10000.0
SSD_D_INNER = 2048
SSD_HEAD_DIM = 64
SSD_HEADS = 32
SSD_GROUPS = 4
SSD_D_STATE = 128
SSD_CONV = 4
SSD_CHUNK = 128
SSD_CONV_DIM = 3072
N_EXPERTS = 64
TOP_K = 8
N_EXPERT_GROUPS = 8
TOPK_GROUPS = 4
EXPERT_DIM = 256
ROUTED_SCALE = 2.5
EPS = 1e-6
NEG = -1e30
LOG2E = math.log2(math.e)
Q_SCALE = ATTN_HEAD_DIM ** -0.5 * LOG2E
SSD_HALO = 16

LANES = 128
SUB = 8
MOE_ROWS = 576
MAIN_COLS = 10240
VMEM_LIMIT = 56 * 1024 * 1024


def _dot(a, b):
    return jnp.dot(a, b, preferred_element_type=F32)


def _sigmoid(x):
    return 1.0 / (1.0 + jnp.exp(-x))


def _split2(a):
    hi = a.astype(BF16)
    lo = (a - hi.astype(F32)).astype(BF16)
    return hi, lo


def _split3(a):
    hi = a.astype(BF16)
    r = a - hi.astype(F32)
    mid = r.astype(BF16)
    lo = (r - mid.astype(F32)).astype(BF16)
    return hi, mid, lo


def _params(sem, vmem=VMEM_LIMIT):
    return pltpu.CompilerParams(dimension_semantics=sem, vmem_limit_bytes=vmem)


def _ada_body(c_ref, w_ref, b_ref, o_ref):
    c = c_ref[...]
    sc = c * _sigmoid(c)
    chi, clo = _split2(sc)
    whi, wlo = _split2(w_ref[...])
    o_ref[...] = _dot(chi, whi) + _dot(chi, wlo) + _dot(clo, whi) + b_ref[...]


def _ada(c_pad, w, b):
    d, n = w.shape
    tn = 1024
    return pl.pallas_call(
        _ada_body,
        grid=(n // tn,),
        in_specs=[pl.BlockSpec((8, d), lambda j: (0, 0)),
                  pl.BlockSpec((d, tn), lambda j: (0, j)),
                  pl.BlockSpec((1, tn), lambda j: (0, j))],
        out_specs=pl.BlockSpec((8, tn), lambda j: (0, j)),
        out_shape=jax.ShapeDtypeStruct((8, n), F32),
        compiler_params=_params(("arbitrary",)),
        name="ada",
    )(c_pad, w, b.reshape(1, n))


def _inproj_body(x_ref, g_ref, sc_ref, sh_ref, w_ref, wdh_ref, wdl_ref, o_ref, dt_ref, h_scr):
    @pl.when(pl.program_id(1) == 0)
    def _():
        x = x_ref[...]
        ms = jnp.mean(x * x, axis=-1, keepdims=True)
        h = x * lax.rsqrt(ms + EPS) * g_ref[...]
        h = h * (1.0 + sc_ref[0]) + sh_ref[0]
        hb, hl = _split2(h)
        h_scr[...] = hb
        dt_ref[...] = _dot(hb, wdh_ref[...]) + _dot(hb, wdl_ref[...]) + _dot(hl, wdh_ref[...])

    o_ref[...] = _dot(h_scr[...], w_ref[...]).astype(BF16)


def _inproj(x2d, g, sc, sh, w_main, wdt_hi, wdt_lo, seq):
    t, d = x2d.shape
    tm = min(1024, seq)
    tn = 1024
    per_b = seq // tm
    return pl.pallas_call(
        _inproj_body,
        grid=(t // tm, MAIN_COLS // tn),
        in_specs=[pl.BlockSpec((tm, d), lambda i, j: (i, 0)),
                  pl.BlockSpec((1, d), lambda i, j: (0, 0)),
                  pl.BlockSpec((1, 1, d), lambda i, j: (i // per_b, 0, 0)),
                  pl.BlockSpec((1, 1, d), lambda i, j: (i // per_b, 0, 0)),
                  pl.BlockSpec((d, tn), lambda i, j: (0, j)),
                  pl.BlockSpec((d, LANES), lambda i, j: (0, 0)),
                  pl.BlockSpec((d, LANES), lambda i, j: (0, 0))],
        out_specs=[pl.BlockSpec((tm, tn), lambda i, j: (i, j)),
                   pl.BlockSpec((tm, LANES), lambda i, j: (i, 0))],
        out_shape=[jax.ShapeDtypeStruct((t, MAIN_COLS), BF16),
                   jax.ShapeDtypeStruct((t, LANES), F32)],
        scratch_shapes=[pltpu.VMEM((tm, d), BF16)],
        compiler_params=_params(("arbitrary", "arbitrary")),
        name="inproj",
    )(x2d, g.reshape(1, d), sc, sh, w_main, wdt_hi, wdt_lo)


def _qkvprep_body(q_ref, k_ref, v_ref, pos_ref, invf_ref, gq_ref, gk_ref, qt_ref, ko_ref, vt_ref, *, tk):
    ts = q_ref.shape[0]
    half = ATTN_HEAD_DIM // 2
    ang = pos_ref[0].astype(F32) * invf_ref[...]
    cos = jnp.cos(ang)
    sin = jnp.sin(ang)
    zeros = jnp.zeros((ATTN_HEAD_DIM, ts), F32)

    def norm_rope(blk, g):
        ms = jnp.mean(blk * blk, axis=0, keepdims=True)
        r = blk * lax.rsqrt(ms + EPS) * g
        x1 = r[:half]
        x2 = r[half:]
        return jnp.concatenate([x1 * cos - x2 * sin, x2 * cos + x1 * sin], axis=0)

    for h in range(ATTN_HEADS):
        sl = slice(h * LANES, (h + 1) * LANES)
        qht = q_ref[:, sl].astype(F32).T
        q1 = norm_rope(qht[:ATTN_HEAD_DIM], gq_ref[...]) * Q_SCALE
        q2 = norm_rope(qht[ATTN_HEAD_DIM:], gq_ref[...]) * Q_SCALE
        qt_ref[0, 2 * h] = jnp.concatenate([q1, zeros], axis=0).astype(BF16)
        qt_ref[0, 2 * h + 1] = jnp.concatenate([zeros, q2], axis=0).astype(BF16)
        kht = k_ref[:, sl].astype(F32).T
        k1 = norm_rope(kht[:ATTN_HEAD_DIM], gk_ref[...])
        k2 = norm_rope(kht[ATTN_HEAD_DIM:], gk_ref[...])
        ko_ref[0, h] = jnp.concatenate([k1, k2], axis=0).T.astype(BF16)
        vht = v_ref[:, sl].astype(F32).T.astype(BF16)
        for cc in range(ts // tk):
            vt_ref[0, h, cc] = vht[:, cc * tk:(cc + 1) * tk]


def _qkvprep(proj, positions, q_norm_g, k_norm_g, batch, seq, tk):
    ts = min(512, seq)
    ns = seq // ts
    half = ATTN_HEAD_DIM // 2
    inv_freq = ROPE_THETA ** (-jnp.arange(half, dtype=F32) * 2.0 / ATTN_HEAD_DIM)
    invf_b = jnp.broadcast_to(inv_freq[:, None], (half, ts))
    gq_b = jnp.broadcast_to(q_norm_g.astype(F32)[:, None], (ATTN_HEAD_DIM, ts))
    gk_b = jnp.broadcast_to(k_norm_g.astype(F32)[:, None], (ATTN_HEAD_DIM, ts))
    pos3 = positions.reshape(batch, 1, seq)
    const = lambda b, i: (0, 0)
    return pl.pallas_call(
        functools.partial(_qkvprep_body, tk=tk),
        grid=(batch, ns),
        in_specs=[pl.BlockSpec((ts, D_MODEL), lambda b, i: (b * ns + i, 0)),
                  pl.BlockSpec((ts, D_MODEL), lambda b, i: (b * ns + i, 1)),
                  pl.BlockSpec((ts, D_MODEL), lambda b, i: (b * ns + i, 2)),
                  pl.BlockSpec((1, 1, ts), lambda b, i: (b, 0, i)),
                  pl.BlockSpec((half, ts), const),
                  pl.BlockSpec((ATTN_HEAD_DIM, ts), const),
                  pl.BlockSpec((ATTN_HEAD_DIM, ts), const)],
        out_specs=[pl.BlockSpec((1, 2 * ATTN_HEADS, LANES, ts), lambda b, i: (b, 0, 0, i)),
                   pl.BlockSpec((1, ATTN_HEADS, ts, LANES), lambda b, i: (b, 0, i, 0)),
                   pl.BlockSpec((1, ATTN_HEADS, ts // tk, LANES, tk), lambda b, i: (b, 0, i, 0, 0))],
        out_shape=[jax.ShapeDtypeStruct((batch, 2 * ATTN_HEADS, LANES, seq), BF16),
                   jax.ShapeDtypeStruct((batch, ATTN_HEADS, seq, LANES), BF16),
                   jax.ShapeDtypeStruct((batch, ATTN_HEADS, seq // tk, LANES, tk), BF16)],
        compiler_params=_params(("arbitrary", "arbitrary")),
        name="qkvprep",
    )(proj, proj, proj, pos3, invf_b, gq_b, gk_b)


def _attn_body(lam_ref, sg_ref, qt_ref, k_ref, vt_ref, o_ref, acc_scr, sa_scr, sb_scr, *, tq, tk, lambda_init):
    i = pl.program_id(2)
    q_maps = (qt_ref[0, 0], qt_ref[0, 1])
    acc_scr[...] = jnp.zeros(acc_scr.shape, F32)
    tri = lax.broadcasted_iota(jnp.int32, (tk, tk), 0) <= lax.broadcasted_iota(jnp.int32, (tk, tk), 1)

    def qk(j, s_scr, lo=0):
        kb = k_ref[0, 0, pl.ds(pl.multiple_of(j * tk, tk), tk), :]
        for mi in range(2):
            s_scr[mi, :, lo:] = _dot(kb, q_maps[mi][:, lo:])

    def process(j, s_scr, carry, diag=None, lo=0):
        vb = vt_ref[0, 0, j]
        w = tq - lo
        out = []
        for mi in range(2):
            m_old, l_old = carry[2 * mi][:, lo:], carry[2 * mi + 1][:, lo:]
            if diag is None:
                s = s_scr[mi]
            elif diag == 0:
                s = jnp.concatenate([jnp.where(tri, s_scr[mi, :, :tk], NEG), s_scr[mi, :, tk:]], axis=1)
            else:
                s = jnp.where(tri, s_scr[mi, :, lo:], NEG)
            m_new = jnp.maximum(m_old, jnp.max(s, axis=0, keepdims=True))
            alpha = jnp.exp2(m_old - m_new)
            p = jnp.exp2(s - m_new[0:1])
            l_new = alpha * l_old + jnp.sum(p, axis=0, keepdims=True)
            acc_scr[mi, :, lo:] = (jnp.broadcast_to(alpha[0:1], (ATTN_V_DIM, w)) * acc_scr[mi, :, lo:]
                                   + _dot(vb, p.astype(BF16)))
            if lo:
                m_new = jnp.concatenate([carry[2 * mi][:, :lo], m_new], axis=1)
                l_new = jnp.concatenate([carry[2 * mi + 1][:, :lo], l_new], axis=1)
            out += [m_new, l_new]
        return tuple(out)

    def pair(p, carry):
        qk(2 * p + 1, sb_scr)
        carry = process(2 * p, sa_scr, carry)
        qk(2 * p + 2, sa_scr)
        return process(2 * p + 1, sb_scr, carry)

    init = (jnp.full((SUB, tq), NEG, F32), jnp.zeros((SUB, tq), F32)) * 2
    qk(0, sa_scr)
    carry = lax.fori_loop(0, i, pair, init)
    qk(2 * i + 1, sb_scr, lo=tk)
    carry = process(2 * i, sa_scr, carry, diag=0)
    carry = process(2 * i + 1, sb_scr, carry, diag=1, lo=tk)

    lv = lam_ref[...]
    lam = (jnp.exp(jnp.sum(lv[0:1] * lv[1:2], keepdims=True))
           - jnp.exp(jnp.sum(lv[2:3] * lv[3:4], keepdims=True)) + lambda_init)
    o = acc_scr[0] / carry[1][0:1] - lam * (acc_scr[1] / carry[3][0:1])
    ms = jnp.mean(o * o, axis=0, keepdims=True)
    o = o * lax.rsqrt(ms + EPS) * sg_ref[...] * (1.0 - lambda_init)
    o_ref[...] = o.T.astype(BF16)


def _attention(qt, kk, vt, lambda_qk, subln_g, lambda_init, batch, seq, tq, tk):
    nq = seq // tq
    sg_b = jnp.broadcast_to(subln_g.astype(F32)[:, None], (ATTN_V_DIM, tq))
    return pl.pallas_call(
        functools.partial(_attn_body, tq=tq, tk=tk, lambda_init=lambda_init),
        grid=(batch, ATTN_HEADS, nq),
        in_specs=[pl.BlockSpec((4, ATTN_HEAD_DIM), lambda b, h, i: (0, 0)),
                  pl.BlockSpec((ATTN_V_DIM, tq), lambda b, h, i: (0, 0)),
                  pl.BlockSpec((1, 2, LANES, tq), lambda b, h, i: (b, h, 0, i)),
                  pl.BlockSpec((1, 1, seq, LANES), lambda b, h, i: (b, h, 0, 0)),
                  pl.BlockSpec((1, 1, seq // tk, LANES, tk), lambda b, h, i: (b, h, 0, 0, 0))],
        out_specs=pl.BlockSpec((tq, LANES), lambda b, h, i: (b * nq + i, h)),
        out_shape=jax.ShapeDtypeStruct((batch * seq, ATTN_HEADS * ATTN_V_DIM), BF16),
        scratch_shapes=[pltpu.VMEM((2, ATTN_V_DIM, tq), F32),
                        pltpu.VMEM((2, tk, tq), F32),
                        pltpu.VMEM((2, tk, tq), F32)],
        compiler_params=_params(("arbitrary", "arbitrary", "arbitrary")),
        name="attn",
    )(lambda_qk.astype(F32), sg_b, qt, kk, vt)


def _ssd_body(z0_ref, z1_ref, x0_ref, x1_ref, bc_ref, dt_ref, cw_ref, cb_ref, dtb_ref, alog_ref,
              dsk_ref, ng_ref, tri_ref, shift_ref, y_ref, ext_scr, st_scr):
    L = SSD_CHUNK
    halo = SSD_HALO
    c = pl.program_id(1)

    @pl.when(c == 0)
    def _():
        ext_scr[0:halo, :] = jnp.zeros((halo, SSD_CONV_DIM), BF16)
        st_scr[...] = jnp.zeros(st_scr.shape, F32)

    ext_scr[halo:halo + L, 0:1024] = x0_ref[...]
    ext_scr[halo:halo + L, 1024:2048] = x1_ref[...]
    ext_scr[halo:halo + L, 2048:3072] = bc_ref[...]
    sh = _dot(shift_ref[...], ext_scr[...])
    acc = cb_ref[...] + cw_ref[SSD_CONV - 1:SSD_CONV, :] * ext_scr[halo:halo + L, :].astype(F32)
    for k in range(SSD_CONV - 1):
        acc = acc + cw_ref[k:k + 1, :] * sh[k * L:(k + 1) * L]
    u = acc * _sigmoid(acc)
    ext_scr[0:halo, :] = ext_scr[L:L + halo, :]

    xdt_in = dt_ref[...] + dtb_ref[...]
    dtp = jnp.maximum(xdt_in, 0.0) + jnp.log(1.0 + jnp.exp(-jnp.abs(xdt_in)))
    a = dtp * (-jnp.exp(alog_ref[...]) * LOG2E)
    tri = tri_ref[...]
    a_hi, a_mid, a_lo = _split3(a)
    acum = _dot(tri, a_hi) + _dot(tri, a_mid) + _dot(tri, a_lo)
    acum_t = acum.T
    lane = lax.broadcasted_iota(jnp.int32, (L, LANES), 1)
    first = lane < SSD_HEAD_DIM
    tril = lax.broadcasted_iota(jnp.int32, (L, L), 0) >= lax.broadcasted_iota(jnp.int32, (L, L), 1)
    gn = SSD_GROUPS * SSD_D_STATE
    gw = SSD_D_INNER // SSD_GROUPS

    def col(mat, h):
        return jnp.broadcast_to(mat[:, h:h + 1], (L, LANES))

    for g in range(SSD_GROUPS):
        bg = u[:, SSD_D_INNER + g * SSD_D_STATE:SSD_D_INNER + (g + 1) * SSD_D_STATE]
        cg = u[:, SSD_D_INNER + gn + g * SSD_D_STATE:SSD_D_INNER + gn + (g + 1) * SSD_D_STATE].astype(BF16)
        bgt = bg.T.astype(BF16)
        cb = _dot(cg, bgt)
        sprev = st_scr[g]
        yoff = _dot(cg, sprev.astype(BF16))
        y_parts, xdd_parts, dec_parts = [], [], []
        for p in range(4):
            h0 = 8 * g + 2 * p
            c0, c1 = col(acum, h0), col(acum, h0 + 1)
            ac_exp = jnp.where(first, c0, c1)
            dt_exp = jnp.where(first, col(dtp, h0), col(dtp, h0 + 1))
            xsl = u[:, h0 * SSD_HEAD_DIM:h0 * SSD_HEAD_DIM + LANES]
            xdt = xsl * dt_exp
            l0 = jnp.exp2(jnp.where(tril, c0 - acum_t[h0:h0 + 1, :], NEG))
            l1 = jnp.exp2(jnp.where(tril, c1 - acum_t[h0 + 1:h0 + 2, :], NEG))
            m2 = jnp.concatenate([(cb * l0).astype(BF16), (cb * l1).astype(BF16)], axis=1)
            xb = xdt.astype(BF16)
            zb = jnp.zeros_like(xb)
            rhs = jnp.concatenate([jnp.where(first, xb, zb), jnp.where(first, zb, xb)], axis=0)
            yd = _dot(m2, rhs)
            yo = yoff[:, p * LANES:(p + 1) * LANES] * jnp.exp2(ac_exp)
            dsk = dsk_ref[:, h0 * SSD_HEAD_DIM:h0 * SSD_HEAD_DIM + LANES]
            y_parts.append(yd + yo + xsl * dsk)
            last = ac_exp[L - 1:L, :]
            xdd_parts.append((xdt * jnp.exp2(last - ac_exp)).astype(BF16))
            dec_parts.append(jnp.exp2(last))
        xdd = jnp.concatenate(xdd_parts, axis=1)
        dec = jnp.concatenate(dec_parts, axis=1)
        st_scr[g] = sprev * dec + _dot(bgt, xdd)
        yg = jnp.concatenate(y_parts, axis=1)
        zref = z0_ref if g < 2 else z1_ref
        zg = zref[:, (g % 2) * gw:(g % 2 + 1) * gw].astype(F32)
        yg = yg * (zg * _sigmoid(zg))
        ms = jnp.mean(yg * yg, axis=-1, keepdims=True)
        yg = yg * lax.rsqrt(ms + EPS) * ng_ref[:, g * gw:(g + 1) * gw]
        y_ref[:, g * gw:(g + 1) * gw] = yg.astype(BF16)


def _ssd(proj, dt, conv_w, conv_b, dt_bias, a_log, d_skip, norm_g, batch, seq):
    L = SSD_CHUNK
    nc = seq // L
    pad = LANES - SSD_HEADS
    dtb = jnp.pad(dt_bias.astype(F32), (0, pad)).reshape(1, LANES)
    alog = jnp.pad(a_log.astype(F32), (0, pad)).reshape(1, LANES)
    dsk = jnp.repeat(d_skip.astype(F32), SSD_HEAD_DIM).reshape(1, SSD_D_INNER)
    tri = jnp.tril(jnp.ones((L, L), F32)).astype(BF16)
    tt = jnp.arange(L)
    shift = jnp.concatenate(
        [jax.nn.one_hot(SSD_HALO + tt - (SSD_CONV - 1) + k, L + SSD_HALO, dtype=BF16) for k in range(SSD_CONV - 1)],
        axis=0)
    row = lambda b, c: (b * nc + c)
    const = lambda b, c: (0, 0)
    return pl.pallas_call(
        _ssd_body,
        grid=(batch, nc),
        in_specs=[pl.BlockSpec((L, 1024), lambda b, c: (row(b, c), 5)),
                  pl.BlockSpec((L, 1024), lambda b, c: (row(b, c), 6)),
                  pl.BlockSpec((L, 1024), lambda b, c: (row(b, c), 7)),
                  pl.BlockSpec((L, 1024), lambda b, c: (row(b, c), 8)),
                  pl.BlockSpec((L, 1024), lambda b, c: (row(b, c), 9)),
                  pl.BlockSpec((L, LANES), lambda b, c: (row(b, c), 0)),
                  pl.BlockSpec((SSD_CONV, SSD_CONV_DIM), const),
                  pl.BlockSpec((1, SSD_CONV_DIM), const),
                  pl.BlockSpec((1, LANES), const),
                  pl.BlockSpec((1, LANES), const),
                  pl.BlockSpec((1, SSD_D_INNER), const),
                  pl.BlockSpec((1, SSD_D_INNER), const),
                  pl.BlockSpec((L, L), const),
                  pl.BlockSpec(((SSD_CONV - 1) * L, L + SSD_HALO), const)],
        out_specs=pl.BlockSpec((L, SSD_D_INNER), lambda b, c: (row(b, c), 0)),
        out_shape=jax.ShapeDtypeStruct((batch * seq, SSD_D_INNER), BF16),
        scratch_shapes=[pltpu.VMEM((L + SSD_HALO, SSD_CONV_DIM), BF16),
                        pltpu.VMEM((SSD_GROUPS, SSD_D_STATE, SSD_D_INNER // SSD_GROUPS), F32)],
        compiler_params=_params(("arbitrary", "arbitrary")),
        name="ssd",
    )(proj, proj, proj, proj, proj, dt, conv_w.astype(F32), conv_b.astype(F32).reshape(1, -1), dtb, alog,
      dsk, norm_g.astype(F32).reshape(1, -1), tri, shift)


def _outproj_body(ya_ref, yb_ref, ga_ref, gb_ref, x_ref, g1_ref, sc_ref, sh_ref, n2_ref,
                  wa_ref, wb_ref, wo_ref, wr_ref, x1_ref, h2_ref, lg_ref):
    pa = _dot(ya_ref[...], wa_ref[...])
    pb = _dot(yb_ref[...], wb_ref[...])
    merged = _sigmoid(ga_ref[...].astype(F32)) * pa + _sigmoid(gb_ref[...].astype(F32)) * pb
    x1 = x_ref[...] + g1_ref[0] * _dot(merged.astype(BF16), wo_ref[...])
    x1_ref[...] = x1
    ms = jnp.mean(x1 * x1, axis=-1, keepdims=True)
    h2 = x1 * lax.rsqrt(ms + EPS) * n2_ref[...]
    h2 = h2 * (1.0 + sc_ref[0]) + sh_ref[0]
    tm = h2.shape[0]
    for s in range(SUB):
        h2_ref[pl.ds(s, tm, stride=SUB), :] = h2[:, s * LANES:(s + 1) * LANES]
    lg_ref[...] = _dot(h2.astype(BF16), wr_ref[...])


def _outproj(ya, yb, proj, x2d, g1, sc2, sh2, norm2_g, wa, wb, wo, wr, seq):
    t, d = x2d.shape
    tm = min(512, seq)
    per_b = seq // tm
    const = lambda i: (0, 0)
    mod = lambda i: (i // per_b, 0, 0)
    return pl.pallas_call(
        _outproj_body,
        grid=(t // tm,),
        in_specs=[pl.BlockSpec((tm, d), lambda i: (i, 0)),
                  pl.BlockSpec((tm, SSD_D_INNER), lambda i: (i, 0)),
                  pl.BlockSpec((tm, d), lambda i: (i, 3)),
                  pl.BlockSpec((tm, d), lambda i: (i, 4)),
                  pl.BlockSpec((tm, d), lambda i: (i, 0)),
                  pl.BlockSpec((1, 1, d), mod),
                  pl.BlockSpec((1, 1, d), mod),
                  pl.BlockSpec((1, 1, d), mod),
                  pl.BlockSpec((1, d), const),
                  pl.BlockSpec((d, d), const),
                  pl.BlockSpec((SSD_D_INNER, d), const),
                  pl.BlockSpec((d, d), const),
                  pl.BlockSpec((d, LANES), const)],
        out_specs=[pl.BlockSpec((tm, d), lambda i: (i, 0)),
                   pl.BlockSpec((tm * SUB, LANES), lambda i: (i, 0)),
                   pl.BlockSpec((tm, LANES), lambda i: (i, 0))],
        out_shape=[jax.ShapeDtypeStruct((t, d), F32),
                   jax.ShapeDtypeStruct((t * SUB, LANES), F32),
                   jax.ShapeDtypeStruct((t, LANES), F32)],
        compiler_params=_params(("arbitrary",)),
        name="outproj",
    )(ya, yb, proj, proj, x2d, g1, sc2, sh2, norm2_g.astype(F32).reshape(1, d), wa, wb, wo, wr)


def _router_body(lg_ref, bias_ref, idx_ref, w_ref):
    tr = lg_ref.shape[0]
    per = N_EXPERTS // N_EXPERT_GROUPS
    lt = lg_ref[...].T
    scores = _sigmoid(lt[:N_EXPERTS])
    choice = scores + bias_ref[...]
    iota = lax.broadcasted_iota(jnp.int32, (per, tr), 0)
    grp = [choice[g * per:(g + 1) * per] for g in range(N_EXPERT_GROUPS)]
    sc_g = [scores[g * per:(g + 1) * per] for g in range(N_EXPERT_GROUPS)]

    gsc = jnp.zeros((N_EXPERT_GROUPS, tr), F32)
    for g in range(N_EXPERT_GROUPS):
        top1 = jnp.max(grp[g], axis=0, keepdims=True)
        idx1 = jnp.min(jnp.where(grp[g] == top1, iota, per), axis=0, keepdims=True)
        top2 = jnp.max(jnp.where(iota == idx1, NEG, grp[g]), axis=0, keepdims=True)
        gsc = jnp.where(iota == g, top1 + top2, gsc)

    gsel = jnp.zeros((N_EXPERT_GROUPS, tr), F32)
    cur = gsc
    for _ in range(TOPK_GROUPS):
        mx = jnp.max(cur, axis=0, keepdims=True)
        idx = jnp.min(jnp.where(cur == mx, iota, N_EXPERT_GROUPS), axis=0, keepdims=True)
        hit = iota == idx
        gsel = jnp.where(hit, 1.0, gsel)
        cur = jnp.where(hit, NEG, cur)

    cur_g = [jnp.where(gsel[g:g + 1] > 0.0, grp[g], NEG) for g in range(N_EXPERT_GROUPS)]
    idx_rows = jnp.zeros((TOP_K, tr), jnp.int32)
    w_rows = jnp.zeros((TOP_K, tr), F32)
    for kk in range(TOP_K):
        mx = cur_g[0]
        for g in range(1, N_EXPERT_GROUPS):
            mx = jnp.maximum(mx, cur_g[g])
        mx = jnp.max(mx, axis=0, keepdims=True)
        idx = jnp.where(cur_g[0] == mx, iota, N_EXPERTS)
        for g in range(1, N_EXPERT_GROUPS):
            idx = jnp.minimum(idx, jnp.where(cur_g[g] == mx, iota + g * per, N_EXPERTS))
        idx = jnp.min(idx, axis=0, keepdims=True)
        wk = jnp.zeros((per, tr), F32)
        for g in range(N_EXPERT_GROUPS):
            hit = (iota + g * per) == idx
            wk = wk + jnp.where(hit, sc_g[g], 0.0)
            cur_g[g] = jnp.where(hit, NEG, cur_g[g])
        idx_rows = jnp.where(iota == kk, idx, idx_rows)
        w_rows = jnp.where(iota == kk, jnp.sum(wk, axis=0, keepdims=True), w_rows)

    wsum = jnp.sum(w_rows, axis=0, keepdims=True)
    idx_ref[...] = idx_rows
    w_ref[...] = w_rows / wsum * ROUTED_SCALE


def _router(logits, router_bias, seq):
    t = logits.shape[0]
    tr = min(1024, seq)
    bias_b = jnp.broadcast_to(router_bias.astype(F32)[:, None], (N_EXPERTS, tr))
    return pl.pallas_call(
        _router_body,
        grid=(t // tr,),
        in_specs=[pl.BlockSpec((tr, LANES), lambda i: (i, 0)),
                  pl.BlockSpec((N_EXPERTS, tr), lambda i: (0, 0))],
        out_specs=[pl.BlockSpec((TOP_K, tr), lambda i: (0, i)),
                   pl.BlockSpec((TOP_K, tr), lambda i: (0, i))],
        out_shape=[jax.ShapeDtypeStruct((TOP_K, t), jnp.int32),
                   jax.ShapeDtypeStruct((TOP_K, t), F32)],
        compiler_params=_params(("arbitrary",)),
        name="router",
    )(logits, bias_b)


def _route_tables(idx_t, w_t, groups, tg, rows, ntiles):
    i32 = jnp.int32
    e = idx_t.reshape(TOP_K, groups, tg).transpose(1, 0, 2).reshape(groups, TOP_K * tg)
    w = w_t.reshape(TOP_K, groups, tg).transpose(1, 0, 2).reshape(groups, TOP_K * tg)
    tl = jnp.broadcast_to(jnp.arange(tg, dtype=i32)[None, None, :], (groups, TOP_K, tg)).reshape(groups, TOP_K * tg)
    skey, sw = lax.sort((e * tg + tl, w), dimension=1, num_keys=1)
    stok = skey % tg
    cnt = jnp.sum((e[:, :, None] == jnp.arange(N_EXPERTS, dtype=i32)[None, None, :]).astype(i32), axis=1)
    off = jnp.cumsum(cnt, axis=1) - cnt
    ntile = (cnt + rows - 1) // rows
    tend = jnp.cumsum(ntile, axis=1)
    tstart = tend - ntile
    n = jnp.arange(ntiles, dtype=i32)
    te = jnp.sum((n[None, :, None] >= tend[:, None, :]).astype(i32), axis=2)
    valid_tile = n[None, :] < tend[:, -1:]
    te = jnp.minimum(te, N_EXPERTS - 1)
    k0 = (n[None, :] - jnp.take_along_axis(tstart, te, axis=1)) * rows
    nvalid = jnp.where(valid_tile, jnp.clip(jnp.take_along_axis(cnt, te, axis=1) - k0, 0, rows), 0)
    start = jnp.clip(jnp.take_along_axis(off, te, axis=1) + k0, 0, TOP_K * tg - 1)
    tok_g, tok_s, wv = _tile_windows(start, nvalid, stok, sw, groups, tg, rows, ntiles)
    return te, nvalid, tok_g, tok_s, wv


def _windows_body(start_ref, nv_ref, stok_ref, sw_ref, tokg_ref, toks_ref, wt_ref, *, rows, tg):
    g = pl.program_id(0)
    nt = tokg_ref.shape[1]
    nl = -(-rows // LANES)
    lane = lax.broadcasted_iota(jnp.int32, (1, LANES), 1)
    rr = lax.broadcasted_iota(jnp.int32, (1, nl * LANES), 1)

    def window(src, q, sh):
        blk = src[0, pl.ds(q, SUB), :]
        rolled = pltpu.roll(blk, (LANES - sh) % LANES, 1)
        parts = [jnp.where(lane < LANES - sh, rolled[k:k + 1], rolled[k + 1:k + 2]) for k in range(nl)]
        return jnp.concatenate(parts, axis=1)

    def body(n, carry):
        s = start_ref[g, n]
        q = s // LANES
        sh = s % LANES
        valid = rr < nv_ref[g, n]
        tok8 = window(stok_ref, q, sh) * SUB
        wv = window(sw_ref, q, sh)
        tokg_ref[0, n] = jnp.where(valid, tok8, 0)[:, :rows]
        toks_ref[0, n] = jnp.where(valid, tok8, tg * SUB)[:, :rows]
        wt_ref[0, n] = jnp.where(valid, wv, 0.0)[:, :rows]
        return carry

    lax.fori_loop(0, nt, body, 0)


def _tile_windows(start, nvalid, stok, sw, groups, tg, rows, ntiles):
    n_list = TOP_K * tg
    n_rows = n_list // LANES + SUB
    pad = n_rows * LANES - n_list
    stok2 = jnp.pad(stok, ((0, 0), (0, pad))).reshape(groups, n_rows, LANES)
    sw2 = jnp.pad(sw, ((0, 0), (0, pad))).reshape(groups, n_rows, LANES)
    lst = pl.BlockSpec((1, n_rows, LANES), lambda g, st, nv: (g, 0, 0))
    out = pl.BlockSpec((1, ntiles, 1, rows), lambda g, st, nv: (g, 0, 0, 0))
    grid_spec = pltpu.PrefetchScalarGridSpec(
        num_scalar_prefetch=2, grid=(groups,), in_specs=[lst, lst], out_specs=[out, out, out])
    shape = (groups, ntiles, 1, rows)
    return pl.pallas_call(
        functools.partial(_windows_body, rows=rows, tg=tg),
        grid_spec=grid_spec,
        out_shape=[jax.ShapeDtypeStruct(shape, jnp.int32), jax.ShapeDtypeStruct(shape, jnp.int32),
                   jax.ShapeDtypeStruct(shape, F32)],
        compiler_params=_params(("arbitrary",)),
        name="windows",
    )(start, nvalid, stok2, sw2)


def _routed_body(te_ref, nv_ref, tokg_ref, toks_ref, wt_ref, h_ref, wg_ref, wu_ref, wd_ref, acc_ref,
                 xbuf, ybuf):
    g = pl.program_id(0)
    n = pl.program_id(1)
    rows = MOE_ROWS
    slot = n % 2
    has_cur = nv_ref[g, n + 1] > 0
    has_prev = nv_ref[g, n] > 0

    @pl.when(n == 0)
    def _():
        acc_ref[...] = jnp.zeros(acc_ref.shape, F32)

    def compute():
        for r in range(rows):
            t8 = pl.multiple_of(tokg_ref[0, 0, 0, r], SUB)
            xbuf[pl.ds(r * SUB, SUB), :] = h_ref[pl.ds(t8, SUB), :]
        x2d = jnp.concatenate([xbuf[pl.ds(s, rows, stride=SUB), :] for s in range(SUB)], axis=1).astype(BF16)
        a = _dot(x2d, wg_ref[0].astype(BF16))
        b = _dot(x2d, wu_ref[0].astype(BF16))
        act = (a * _sigmoid(a) * b).astype(BF16)
        y = _dot(act, wd_ref[0].astype(BF16))
        for s in range(SUB):
            ybuf[slot, pl.ds(s, rows, stride=SUB), :] = y[:, s * LANES:(s + 1) * LANES]

    def scatter_prev():
        for c in range(rows // SUB):
            upd = []
            for r in range(c * SUB, (c + 1) * SUB):
                t8 = pl.multiple_of(toks_ref[0, 0, 0, r], SUB)
                upd.append((t8, acc_ref[0, pl.ds(t8, SUB), :]
                            + wt_ref[0, 0, 0, r] * ybuf[1 - slot, pl.ds(r * SUB, SUB), :]))
            for t8, v in upd:
                acc_ref[0, pl.ds(t8, SUB), :] = v

    @pl.when(jnp.logical_and(has_cur, has_prev))
    def _():
        scatter_prev()
        compute()

    @pl.when(jnp.logical_and(has_cur, jnp.logical_not(has_prev)))
    def _():
        compute()

    @pl.when(jnp.logical_and(jnp.logical_not(has_cur), has_prev))
    def _():
        scatter_prev()


def _routed(hv, tables, w_eg, w_eu, w_ed, groups, tg, ntiles):
    te, nvalid, tok_g, tok_s, wv = tables
    d, f = D_MODEL, EXPERT_DIM
    rows = MOE_ROWS
    i32 = jnp.int32
    te = jnp.concatenate([te, te[:, -1:]], axis=1)
    zero = jnp.zeros((groups, 1), i32)
    nvp = jnp.concatenate([zero, nvalid, zero], axis=1)
    cur = lambda g, n, te, nv: (g, jnp.minimum(n, ntiles - 1), 0, 0)
    prev = lambda g, n, te, nv: (g, jnp.maximum(n - 1, 0), 0, 0)
    smem = functools.partial(pl.BlockSpec, (1, 1, 1, rows), memory_space=pltpu.SMEM)
    grid_spec = pltpu.PrefetchScalarGridSpec(
        num_scalar_prefetch=2,
        grid=(groups, ntiles + 1),
        in_specs=[smem(index_map=cur), smem(index_map=prev), smem(index_map=prev),
                  pl.BlockSpec((tg * SUB, LANES), lambda g, n, te, nv: (g, 0), pipeline_mode=pl.Buffered(1)),
                  pl.BlockSpec((1, d, f), lambda g, n, te, nv: (te[g, n], 0, 0)),
                  pl.BlockSpec((1, d, f), lambda g, n, te, nv: (te[g, n], 0, 0)),
                  pl.BlockSpec((1, f, d), lambda g, n, te, nv: (te[g, n], 0, 0))],
        out_specs=pl.BlockSpec((1, (tg + 1) * SUB, LANES), lambda g, n, te, nv: (g, 0, 0),
                               pipeline_mode=pl.Buffered(1)),
        scratch_shapes=[pltpu.VMEM((rows * SUB, LANES), F32), pltpu.VMEM((2, rows * SUB, LANES), F32)],
    )
    return pl.pallas_call(
        _routed_body,
        grid_spec=grid_spec,
        out_shape=jax.ShapeDtypeStruct((groups, (tg + 1) * SUB, LANES), F32),
        compiler_params=_params(("arbitrary", "arbitrary")),
        name="routed",
    )(te, nvp, tok_g, tok_s, wv, hv, w_eg, w_eu, w_ed)


def _combine_body(h_ref, r_ref, x1_ref, g2_ref, sg_ref, su_ref, sd_ref, o_ref):
    tm = x1_ref.shape[0]
    h = jnp.concatenate([h_ref[pl.ds(s, tm, stride=SUB), :] for s in range(SUB)], axis=1)
    routed = jnp.concatenate([r_ref[0, pl.ds(s, tm, stride=SUB), :] for s in range(SUB)], axis=1)
    hb = h.astype(BF16)
    a = _dot(hb, sg_ref[...])
    b = _dot(hb, su_ref[...])
    shared = _dot((a * _sigmoid(a) * b).astype(BF16), sd_ref[...])
    o_ref[...] = x1_ref[...] + g2_ref[0] * (routed + shared)


def _combine(hv, routed, x1, g2, w_sg, w_su, w_sd, seq, tg):
    t, d = x1.shape
    tm = min(1024, seq, tg)
    per_b = seq // tm
    per_g = tg // tm
    f = w_sg.shape[1]
    const = lambda i: (0, 0)
    row = lambda i: (i, 0)
    return pl.pallas_call(
        _combine_body,
        grid=(t // tm,),
        in_specs=[pl.BlockSpec((tm * SUB, LANES), row),
                  pl.BlockSpec((1, tm * SUB, LANES), lambda i: (i // per_g, i % per_g, 0)),
                  pl.BlockSpec((tm, d), row),
                  pl.BlockSpec((1, 1, d), lambda i: (i // per_b, 0, 0)),
                  pl.BlockSpec((d, f), const),
                  pl.BlockSpec((d, f), const),
                  pl.BlockSpec((f, d), const)],
        out_specs=pl.BlockSpec((tm, d), row),
        out_shape=jax.ShapeDtypeStruct((t, d), F32),
        compiler_params=_params(("arbitrary",)),
        name="combine",
    )(hv, routed, x1, g2, w_sg, w_su, w_sd)


def kernel(x, c, positions, w_ada, b_ada, norm1_g, w_in, q_norm_g, k_norm_g, lambda_qk, subln_g, conv_w, conv_b, dt_bias, a_log, d_skip, ssd_norm_g, w_branch_a, w_branch_b, w_out, norm2_g, w_router, router_bias, w_exp_gate, w_exp_up, w_exp_down, w_sh_gate, w_sh_up, w_sh_down):
    batch, seq, d = x.shape
    t = batch * seq
    tq = min(1024, seq)
    tk = tq // 2
    xf = x.reshape(t, d)
    for layer in range(w_ada.shape[0]):
        lambda_init = 0.8 - 0.6 * math.exp(-0.3 * layer)
        c_pad = jnp.pad(c.astype(F32), ((0, 8 - batch), (0, 0)))
        mod = _ada(c_pad, w_ada[layer], b_ada[layer])[:batch].reshape(batch, 6, 1, d)
        sh1, sc1, g1, sh2, sc2, g2 = (mod[:, n] for n in range(6))

        w_l = w_in[layer]
        wdt = jnp.pad(w_l[:, MAIN_COLS:], ((0, 0), (0, LANES - SSD_HEADS)))
        wdt_hi, wdt_lo = _split2(wdt)
        proj, dt = _inproj(xf, norm1_g[layer], sc1, sh1, w_l[:, :MAIN_COLS].astype(BF16), wdt_hi, wdt_lo, seq)

        qt, kk, vt = _qkvprep(proj, positions, q_norm_g[layer], k_norm_g[layer], batch, seq, tk)
        ya = _attention(qt, kk, vt, lambda_qk[layer], subln_g[layer], lambda_init, batch, seq, tq, tk)
        yb = _ssd(proj, dt, conv_w[layer], conv_b[layer], dt_bias[layer], a_log[layer], d_skip[layer],
                  ssd_norm_g[layer], batch, seq)

        wr = jnp.pad(w_router[layer], ((0, 0), (0, LANES - N_EXPERTS))).astype(BF16)
        x1, h2, logits = _outproj(ya, yb, proj, xf, g1, sc2, sh2, norm2_g[layer],
                                  w_branch_a[layer].astype(BF16), w_branch_b[layer].astype(BF16),
                                  w_out[layer].astype(BF16), wr, seq)
        idx_t, w_t = _router(logits, router_bias[layer], seq)
        tg = min(4096, t)
        groups = t // tg
        ntiles = -(-tg * TOP_K // MOE_ROWS) + N_EXPERTS
        tables = _route_tables(idx_t, w_t, groups, tg, MOE_ROWS, ntiles)
        routed = _routed(h2, tables, w_exp_gate[layer], w_exp_up[layer], w_exp_down[layer], groups, tg, ntiles)
        xf = _combine(h2, routed, x1, g2, w_sh_gate[layer].astype(BF16), w_sh_up[layer].astype(BF16),
                      w_sh_down[layer].astype(BF16), seq, tg)
    return xf.reshape(batch, seq, d)
```

```python
import functools
import math

import jax
import jax.numpy as jnp
from jax import lax
from jax.experimental import pallas as pl
from jax.experimental.pallas import tpu as pltpu

F32 = jnp.float32
BF16 = jnp.bfloat16

D_MODEL = 1024
ATTN_HEADS = 8
ATTN_HEAD_DIM = 64
ATTN_V_DIM = 128
ROPE_THETA = 10000.0
SSD_D_INNER = 2048
SSD_HEAD_DIM = 64
SSD_HEADS = 32
SSD_GROUPS = 4
SSD_D_STATE = 128
SSD_CONV = 4
SSD_CHUNK = 128
SSD_CONV_DIM = 3072
N_EXPERTS = 64
TOP_K = 8
N_EXPERT_GROUPS = 8
TOPK_GROUPS = 4
EXPERT_DIM = 256
ROUTED_SCALE = 2.5
EPS = 1e-6
NEG = -1e30
LOG2E = math.log2(math.e)
Q_SCALE = ATTN_HEAD_DIM ** -0.5 * LOG2E
SSD_HALO = 16

LANES = 128
SUB = 8
MOE_ROWS = 576
MAIN_COLS = 10240
VMEM_LIMIT = 56 * 1024 * 1024


def _dot(a, b):
    return jnp.dot(a, b, preferred_element_type=F32)


def _sigmoid(x):
    return 1.0 / (1.0 + jnp.exp(-x))


def _split2(a):
    hi = a.astype(BF16)
    lo = (a - hi.astype(F32)).astype(BF16)
    return hi, lo


def _split3(a):
    hi = a.astype(BF16)
    r = a - hi.astype(F32)
    mid = r.astype(BF16)
    lo = (r - mid.astype(F32)).astype(BF16)
    return hi, mid, lo


def _params(sem, vmem=VMEM_LIMIT):
    return pltpu.CompilerParams(dimension_semantics=sem, vmem_limit_bytes=vmem)


def _ada_body(c_ref, w_ref, b_ref, o_ref):
    c = c_ref[...]
    sc = c * _sigmoid(c)
    chi, clo = _split2(sc)
    whi, wlo = _split2(w_ref[...])
    o_ref[...] = _dot(chi, whi) + _dot(chi, wlo) + _dot(clo, whi) + b_ref[...]


def _ada(c_pad, w, b):
    d, n = w.shape
    tn = 1024
    return pl.pallas_call(
        _ada_body,
        grid=(n // tn,),
        in_specs=[pl.BlockSpec((8, d), lambda j: (0, 0)),
                  pl.BlockSpec((d, tn), lambda j: (0, j)),
                  pl.BlockSpec((1, tn), lambda j: (0, j))],
        out_specs=pl.BlockSpec((8, tn), lambda j: (0, j)),
        out_shape=jax.ShapeDtypeStruct((8, n), F32),
        compiler_params=_params(("arbitrary",)),
        name="ada",
    )(c_pad, w, b.reshape(1, n))


def _inproj_body(x_ref, g_ref, sc_ref, sh_ref, w_ref, wdh_ref, wdl_ref, o_ref, dt_ref, h_scr):
    @pl.when(pl.program_id(1) == 0)
    def _():
        x = x_ref[...]
        ms = jnp.mean(x * x, axis=-1, keepdims=True)
        h = x * lax.rsqrt(ms + EPS) * g_ref[...]
        h = h * (1.0 + sc_ref[0]) + sh_ref[0]
        hb, hl = _split2(h)
        h_scr[...] = hb
        dt_ref[...] = _dot(hb, wdh_ref[...]) + _dot(hb, wdl_ref[...]) + _dot(hl, wdh_ref[...])

    o_ref[...] = _dot(h_scr[...], w_ref[...]).astype(BF16)


def _inproj(x2d, g, sc, sh, w_main, wdt_hi, wdt_lo, seq):
    t, d = x2d.shape
    tm = min(1024, seq)
    tn = 1024
    per_b = seq // tm
    return pl.pallas_call(
        _inproj_body,
        grid=(t // tm, MAIN_COLS // tn),
        in_specs=[pl.BlockSpec((tm, d), lambda i, j: (i, 0)),
                  pl.BlockSpec((1, d), lambda i, j: (0, 0)),
                  pl.BlockSpec((1, 1, d), lambda i, j: (i // per_b, 0, 0)),
                  pl.BlockSpec((1, 1, d), lambda i, j: (i // per_b, 0, 0)),
                  pl.BlockSpec((d, tn), lambda i, j: (0, j)),
                  pl.BlockSpec((d, LANES), lambda i, j: (0, 0)),
                  pl.BlockSpec((d, LANES), lambda i, j: (0, 0))],
        out_specs=[pl.BlockSpec((tm, tn), lambda i, j: (i, j)),
                   pl.BlockSpec((tm, LANES), lambda i, j: (i, 0))],
        out_shape=[jax.ShapeDtypeStruct((t, MAIN_COLS), BF16),
                   jax.ShapeDtypeStruct((t, LANES), F32)],
        scratch_shapes=[pltpu.VMEM((tm, d), BF16)],
        compiler_params=_params(("arbitrary", "arbitrary")),
        name="inproj",
    )(x2d, g.reshape(1, d), sc, sh, w_main, wdt_hi, wdt_lo)


def _qkvprep_body(q_ref, k_ref, v_ref, pos_ref, invf_ref, gq_ref, gk_ref, qt_ref, ko_ref, vt_ref, *, tk):
    ts = q_ref.shape[0]
    half = ATTN_HEAD_DIM // 2
    ang = pos_ref[0].astype(F32) * invf_ref[...]
    cos = jnp.cos(ang)
    sin = jnp.sin(ang)
    zeros = jnp.zeros((ATTN_HEAD_DIM, ts), F32)

    def norm_rope(blk, g):
        ms = jnp.mean(blk * blk, axis=0, keepdims=True)
        r = blk * lax.rsqrt(ms + EPS) * g
        x1 = r[:half]
        x2 = r[half:]
        return jnp.concatenate([x1 * cos - x2 * sin, x2 * cos + x1 * sin], axis=0)

    for h in range(ATTN_HEADS):
        sl = slice(h * LANES, (h + 1) * LANES)
        qht = q_ref[:, sl].astype(F32).T
        q1 = norm_rope(qht[:ATTN_HEAD_DIM], gq_ref[...]) * Q_SCALE
        q2 = norm_rope(qht[ATTN_HEAD_DIM:], gq_ref[...]) * Q_SCALE
        qt_ref[0, 2 * h] = jnp.concatenate([q1, zeros], axis=0).astype(BF16)
        qt_ref[0, 2 * h + 1] = jnp.concatenate([zeros, q2], axis=0).astype(BF16)
        kht = k_ref[:, sl].astype(F32).T
        k1 = norm_rope(kht[:ATTN_HEAD_DIM], gk_ref[...])
        k2 = norm_rope(kht[ATTN_HEAD_DIM:], gk_ref[...])
        ko_ref[0, h] = jnp.concatenate([k1, k2], axis=0).T.astype(BF16)
        vht = v_ref[:, sl].astype(F32).T.astype(BF16)
        for cc in range(ts // tk):
            vt_ref[0, h, cc] = vht[:, cc * tk:(cc + 1) * tk]


def _qkvprep(proj, positions, q_norm_g, k_norm_g, batch, seq, tk):
    ts = min(512, seq)
    ns = seq // ts
    half = ATTN_HEAD_DIM // 2
    inv_freq = ROPE_THETA ** (-jnp.arange(half, dtype=F32) * 2.0 / ATTN_HEAD_DIM)
    invf_b = jnp.broadcast_to(inv_freq[:, None], (half, ts))
    gq_b = jnp.broadcast_to(q_norm_g.astype(F32)[:, None], (ATTN_HEAD_DIM, ts))
    gk_b = jnp.broadcast_to(k_norm_g.astype(F32)[:, None], (ATTN_HEAD_DIM, ts))
    pos3 = positions.reshape(batch, 1, seq)
    const = lambda b, i: (0, 0)
    return pl.pallas_call(
        functools.partial(_qkvprep_body, tk=tk),
        grid=(batch, ns),
        in_specs=[pl.BlockSpec((ts, D_MODEL), lambda b, i: (b * ns + i, 0)),
                  pl.BlockSpec((ts, D_MODEL), lambda b, i: (b * ns + i, 1)),
                  pl.BlockSpec((ts, D_MODEL), lambda b, i: (b * ns + i, 2)),
                  pl.BlockSpec((1, 1, ts), lambda b, i: (b, 0, i)),
                  pl.BlockSpec((half, ts), const),
                  pl.BlockSpec((ATTN_HEAD_DIM, ts), const),
                  pl.BlockSpec((ATTN_HEAD_DIM, ts), const)],
        out_specs=[pl.BlockSpec((1, 2 * ATTN_HEADS, LANES, ts), lambda b, i: (b, 0, 0, i)),
                   pl.BlockSpec((1, ATTN_HEADS, ts, LANES), lambda b, i: (b, 0, i, 0)),
                   pl.BlockSpec((1, ATTN_HEADS, ts // tk, LANES, tk), lambda b, i: (b, 0, i, 0, 0))],
        out_shape=[jax.ShapeDtypeStruct((batch, 2 * ATTN_HEADS, LANES, seq), BF16),
                   jax.ShapeDtypeStruct((batch, ATTN_HEADS, seq, LANES), BF16),
                   jax.ShapeDtypeStruct((batch, ATTN_HEADS, seq // tk, LANES, tk), BF16)],
        compiler_params=_params(("arbitrary", "arbitrary")),
        name="qkvprep",
    )(proj, proj, proj, pos3, invf_b, gq_b, gk_b)


def _attn_body(lam_ref, sg_ref, qt_ref, k_ref, vt_ref, o_ref, acc_scr, sa_scr, sb_scr, *, tq, tk, lambda_init):
    i = pl.program_id(2)
    q_maps = (qt_ref[0, 0], qt_ref[0, 1])
    acc_scr[...] = jnp.zeros(acc_scr.shape, F32)
    tri = lax.broadcasted_iota(jnp.int32, (tk, tk), 0) <= lax.broadcasted_iota(jnp.int32, (tk, tk), 1)

    def qk(j, s_scr, lo=0):
        kb = k_ref[0, 0, pl.ds(pl.multiple_of(j * tk, tk), tk), :]
        for mi in range(2):
            s_scr[mi, :, lo:] = _dot(kb, q_maps[mi][:, lo:])

    def process(j, s_scr, carry, diag=None, lo=0):
        vb = vt_ref[0, 0, j]
        w = tq - lo
        out = []
        for mi in range(2):
            m_old, l_old = carry[2 * mi][:, lo:], carry[2 * mi + 1][:, lo:]
            if diag is None:
                s = s_scr[mi]
            elif diag == 0:
                s = jnp.concatenate([jnp.where(tri, s_scr[mi, :, :tk], NEG), s_scr[mi, :, tk:]], axis=1)
            else:
                s = jnp.where(tri, s_scr[mi, :, lo:], NEG)
            m_new = jnp.maximum(m_old, jnp.max(s, axis=0, keepdims=True))
            alpha = jnp.exp2(m_old - m_new)
            p = jnp.exp2(s - m_new[0:1])
            l_new = alpha * l_old + jnp.sum(p, axis=0, keepdims=True)
            acc_scr[mi, :, lo:] = (jnp.broadcast_to(alpha[0:1], (ATTN_V_DIM, w)) * acc_scr[mi, :, lo:]
                                   + _dot(vb, p.astype(BF16)))
            if lo:
                m_new = jnp.concatenate([carry[2 * mi][:, :lo], m_new], axis=1)
                l_new = jnp.concatenate([carry[2 * mi + 1][:, :lo], l_new], axis=1)
            out += [m_new, l_new]
        return tuple(out)

    def pair(p, carry):
        qk(2 * p + 1, sb_scr)
        carry = process(2 * p, sa_scr, carry)
        qk(2 * p + 2, sa_scr)
        return process(2 * p + 1, sb_scr, carry)

    init = (jnp.full((SUB, tq), NEG, F32), jnp.zeros((SUB, tq), F32)) * 2
    qk(0, sa_scr)
    carry = lax.fori_loop(0, i, pair, init)
    qk(2 * i + 1, sb_scr, lo=tk)
    carry = process(2 * i, sa_scr, carry, diag=0)
    carry = process(2 * i + 1, sb_scr, carry, diag=1, lo=tk)

    lv = lam_ref[...]
    lam = (jnp.exp(jnp.sum(lv[0:1] * lv[1:2], keepdims=True))
           - jnp.exp(jnp.sum(lv[2:3] * lv[3:4], keepdims=True)) + lambda_init)
    o = acc_scr[0] / carry[1][0:1] - lam * (acc_scr[1] / carry[3][0:1])
    ms = jnp.mean(o * o, axis=0, keepdims=True)
    o = o * lax.rsqrt(ms + EPS) * sg_ref[...] * (1.0 - lambda_init)
    o_ref[...] = o.T.astype(BF16)


def _attention(qt, kk, vt, lambda_qk, subln_g, lambda_init, batch, seq, tq, tk):
    nq = seq // tq
    sg_b = jnp.broadcast_to(subln_g.astype(F32)[:, None], (ATTN_V_DIM, tq))
    return pl.pallas_call(
        functools.partial(_attn_body, tq=tq, tk=tk, lambda_init=lambda_init),
        grid=(batch, ATTN_HEADS, nq),
        in_specs=[pl.BlockSpec((4, ATTN_HEAD_DIM), lambda b, h, i: (0, 0)),
                  pl.BlockSpec((ATTN_V_DIM, tq), lambda b, h, i: (0, 0)),
                  pl.BlockSpec((1, 2, LANES, tq), lambda b, h, i: (b, h, 0, i)),
                  pl.BlockSpec((1, 1, seq, LANES), lambda b, h, i: (b, h, 0, 0)),
                  pl.BlockSpec((1, 1, seq // tk, LANES, tk), lambda b, h, i: (b, h, 0, 0, 0))],
        out_specs=pl.BlockSpec((tq, LANES), lambda b, h, i: (b * nq + i, h)),
        out_shape=jax.ShapeDtypeStruct((batch * seq, ATTN_HEADS * ATTN_V_DIM), BF16),
        scratch_shapes=[pltpu.VMEM((2, ATTN_V_DIM, tq), F32),
                        pltpu.VMEM((2, tk, tq), F32),
                        pltpu.VMEM((2, tk, tq), F32)],
        compiler_params=_params(("arbitrary", "arbitrary", "arbitrary")),
        name="attn",
    )(lambda_qk.astype(F32), sg_b, qt, kk, vt)


def _ssd_body(z0_ref, z1_ref, x0_ref, x1_ref, bc_ref, dt_ref, cw_ref, cb_ref, dtb_ref, alog_ref,
              dsk_ref, ng_ref, tri_ref, shift_ref, y_ref, ext_scr, st_scr):
    L = SSD_CHUNK
    halo = SSD_HALO
    c = pl.program_id(1)

    @pl.when(c == 0)
    def _():
        ext_scr[0:halo, :] = jnp.zeros((halo, SSD_CONV_DIM), BF16)
        st_scr[...] = jnp.zeros(st_scr.shape, F32)

    ext_scr[halo:halo + L, 0:1024] = x0_ref[...]
    ext_scr[halo:halo + L, 1024:2048] = x1_ref[...]
    ext_scr[halo:halo + L, 2048:3072] = bc_ref[...]
    sh = _dot(shift_ref[...], ext_scr[...])
    acc = cb_ref[...] + cw_ref[SSD_CONV - 1:SSD_CONV, :] * ext_scr[halo:halo + L, :].astype(F32)
    for k in range(SSD_CONV - 1):
        acc = acc + cw_ref[k:k + 1, :] * sh[k * L:(k + 1) * L]
    u = acc * _sigmoid(acc)
    ext_scr[0:halo, :] = ext_scr[L:L + halo, :]

    xdt_in = dt_ref[...] + dtb_ref[...]
    dtp = jnp.maximum(xdt_in, 0.0) + jnp.log(1.0 + jnp.exp(-jnp.abs(xdt_in)))
    a = dtp * (-jnp.exp(alog_ref[...]) * LOG2E)
    tri = tri_ref[...]
    a_hi, a_mid, a_lo = _split3(a)
    acum = _dot(tri, a_hi) + _dot(tri, a_mid) + _dot(tri, a_lo)
    acum_t = acum.T
    lane = lax.broadcasted_iota(jnp.int32, (L, LANES), 1)
    first = lane < SSD_HEAD_DIM
    tril = lax.broadcasted_iota(jnp.int32, (L, L), 0) >= lax.broadcasted_iota(jnp.int32, (L, L), 1)
    gn = SSD_GROUPS * SSD_D_STATE
    gw = SSD_D_INNER // SSD_GROUPS

    def col(mat, h):
        return jnp.broadcast_to(mat[:, h:h + 1], (L, LANES))

    for g in range(SSD_GROUPS):
        bg = u[:, SSD_D_INNER + g * SSD_D_STATE:SSD_D_INNER + (g + 1) * SSD_D_STATE]
        cg = u[:, SSD_D_INNER + gn + g * SSD_D_STATE:SSD_D_INNER + gn + (g + 1) * SSD_D_STATE].astype(BF16)
        bgt = bg.T.astype(BF16)
        cb = _dot(cg, bgt)
        sprev = st_scr[g]
        yoff = _dot(cg, sprev.astype(BF16))
        y_parts, xdd_parts, dec_parts = [], [], []
        for p in range(4):
            h0 = 8 * g + 2 * p
            c0, c1 = col(acum, h0), col(acum, h0 + 1)
            ac_exp = jnp.where(first, c0, c1)
            dt_exp = jnp.where(first, col(dtp, h0), col(dtp, h0 + 1))
            xsl = u[:, h0 * SSD_HEAD_DIM:h0 * SSD_HEAD_DIM + LANES]
            xdt = xsl * dt_exp
            l0 = jnp.exp2(jnp.where(tril, c0 - acum_t[h0:h0 + 1, :], NEG))
            l1 = jnp.exp2(jnp.where(tril, c1 - acum_t[h0 + 1:h0 + 2, :], NEG))
            m2 = jnp.concatenate([(cb * l0).astype(BF16), (cb * l1).astype(BF16)], axis=1)
            xb = xdt.astype(BF16)
            zb = jnp.zeros_like(xb)
            rhs = jnp.concatenate([jnp.where(first, xb, zb), jnp.where(first, zb, xb)], axis=0)
            yd = _dot(m2, rhs)
            yo = yoff[:, p * LANES:(p + 1) * LANES] * jnp.exp2(ac_exp)
            dsk = dsk_ref[:, h0 * SSD_HEAD_DIM:h0 * SSD_HEAD_DIM + LANES]
            y_parts.append(yd + yo + xsl * dsk)
            last = ac_exp[L - 1:L, :]
            xdd_parts.append((xdt * jnp.exp2(last - ac_exp)).astype(BF16))
            dec_parts.append(jnp.exp2(last))
        xdd = jnp.concatenate(xdd_parts, axis=1)
        dec = jnp.concatenate(dec_parts, axis=1)
        st_scr[g] = sprev * dec + _dot(bgt, xdd)
        yg = jnp.concatenate(y_parts, axis=1)
        zref = z0_ref if g < 2 else z1_ref
        zg = zref[:, (g % 2) * gw:(g % 2 + 1) * gw].astype(F32)
        yg = yg * (zg * _sigmoid(zg))
        ms = jnp.mean(yg * yg, axis=-1, keepdims=True)
        yg = yg * lax.rsqrt(ms + EPS) * ng_ref[:, g * gw:(g + 1) * gw]
        y_ref[:, g * gw:(g + 1) * gw] = yg.astype(BF16)


def _ssd(proj, dt, conv_w, conv_b, dt_bias, a_log, d_skip, norm_g, batch, seq):
    L = SSD_CHUNK
    nc = seq // L
    pad = LANES - SSD_HEADS
    dtb = jnp.pad(dt_bias.astype(F32), (0, pad)).reshape(1, LANES)
    alog = jnp.pad(a_log.astype(F32), (0, pad)).reshape(1, LANES)
    dsk = jnp.repeat(d_skip.astype(F32), SSD_HEAD_DIM).reshape(1, SSD_D_INNER)
    tri = jnp.tril(jnp.ones((L, L), F32)).astype(BF16)
    tt = jnp.arange(L)
    shift = jnp.concatenate(
        [jax.nn.one_hot(SSD_HALO + tt - (SSD_CONV - 1) + k, L + SSD_HALO, dtype=BF16) for k in range(SSD_CONV - 1)],
        axis=0)
    row = lambda b, c: (b * nc + c)
    const = lambda b, c: (0, 0)
    return pl.pallas_call(
        _ssd_body,
        grid=(batch, nc),
        in_specs=[pl.BlockSpec((L, 1024), lambda b, c: (row(b, c), 5)),
                  pl.BlockSpec((L, 1024), lambda b, c: (row(b, c), 6)),
                  pl.BlockSpec((L, 1024), lambda b, c: (row(b, c), 7)),
                  pl.BlockSpec((L, 1024), lambda b, c: (row(b, c), 8)),
                  pl.BlockSpec((L, 1024), lambda b, c: (row(b, c), 9)),
                  pl.BlockSpec((L, LANES), lambda b, c: (row(b, c), 0)),
                  pl.BlockSpec((SSD_CONV, SSD_CONV_DIM), const),
                  pl.BlockSpec((1, SSD_CONV_DIM), const),
                  pl.BlockSpec((1, LANES), const),
                  pl.BlockSpec((1, LANES), const),
                  pl.BlockSpec((1, SSD_D_INNER), const),
                  pl.BlockSpec((1, SSD_D_INNER), const),
                  pl.BlockSpec((L, L), const),
                  pl.BlockSpec(((SSD_CONV - 1) * L, L + SSD_HALO), const)],
        out_specs=pl.BlockSpec((L, SSD_D_INNER), lambda b, c: (row(b, c), 0)),
        out_shape=jax.ShapeDtypeStruct((batch * seq, SSD_D_INNER), BF16),
        scratch_shapes=[pltpu.VMEM((L + SSD_HALO, SSD_CONV_DIM), BF16),
                        pltpu.VMEM((SSD_GROUPS, SSD_D_STATE, SSD_D_INNER // SSD_GROUPS), F32)],
        compiler_params=_params(("arbitrary", "arbitrary")),
        name="ssd",
    )(proj, proj, proj, proj, proj, dt, conv_w.astype(F32), conv_b.astype(F32).reshape(1, -1), dtb, alog,
      dsk, norm_g.astype(F32).reshape(1, -1), tri, shift)


def _outproj_body(ya_ref, yb_ref, ga_ref, gb_ref, x_ref, g1_ref, sc_ref, sh_ref, n2_ref,
                  wa_ref, wb_ref, wo_ref, wr_ref, x1_ref, h2_ref, lg_ref):
    pa = _dot(ya_ref[...], wa_ref[...])
    pb = _dot(yb_ref[...], wb_ref[...])
    merged = _sigmoid(ga_ref[...].astype(F32)) * pa + _sigmoid(gb_ref[...].astype(F32)) * pb
    x1 = x_ref[...] + g1_ref[0] * _dot(merged.astype(BF16), wo_ref[...])
    x1_ref[...] = x1
    ms = jnp.mean(x1 * x1, axis=-1, keepdims=True)
    h2 = x1 * lax.rsqrt(ms + EPS) * n2_ref[...]
    h2 = h2 * (1.0 + sc_ref[0]) + sh_ref[0]
    tm = h2.shape[0]
    for s in range(SUB):
        h2_ref[pl.ds(s, tm, stride=SUB), :] = h2[:, s * LANES:(s + 1) * LANES]
    lg_ref[...] = _dot(h2.astype(BF16), wr_ref[...])


def _outproj(ya, yb, proj, x2d, g1, sc2, sh2, norm2_g, wa, wb, wo, wr, seq):
    t, d = x2d.shape
    tm = min(512, seq)
    per_b = seq // tm
    const = lambda i: (0, 0)
    mod = lambda i: (i // per_b, 0, 0)
    return pl.pallas_call(
        _outproj_body,
        grid=(t // tm,),
        in_specs=[pl.BlockSpec((tm, d), lambda i: (i, 0)),
                  pl.BlockSpec((tm, SSD_D_INNER), lambda i: (i, 0)),
                  pl.BlockSpec((tm, d), lambda i: (i, 3)),
                  pl.BlockSpec((tm, d), lambda i: (i, 4)),
                  pl.BlockSpec((tm, d), lambda i: (i, 0)),
                  pl.BlockSpec((1, 1, d), mod),
                  pl.BlockSpec((1, 1, d), mod),
                  pl.BlockSpec((1, 1, d), mod),
                  pl.BlockSpec((1, d), const),
                  pl.BlockSpec((d, d), const),
                  pl.BlockSpec((SSD_D_INNER, d), const),
                  pl.BlockSpec((d, d), const),
                  pl.BlockSpec((d, LANES), const)],
        out_specs=[pl.BlockSpec((tm, d), lambda i: (i, 0)),
                   pl.BlockSpec((tm * SUB, LANES), lambda i: (i, 0)),
                   pl.BlockSpec((tm, LANES), lambda i: (i, 0))],
        out_shape=[jax.ShapeDtypeStruct((t, d), F32),
                   jax.ShapeDtypeStruct((t * SUB, LANES), F32),
                   jax.ShapeDtypeStruct((t, LANES), F32)],
        compiler_params=_params(("arbitrary",)),
        name="outproj",
    )(ya, yb, proj, proj, x2d, g1, sc2, sh2, norm2_g.astype(F32).reshape(1, d), wa, wb, wo, wr)


def _router_body(lg_ref, bias_ref, idx_ref, w_ref):
    tr = lg_ref.shape[0]
    per = N_EXPERTS // N_EXPERT_GROUPS
    lt = lg_ref[...].T
    scores = _sigmoid(lt[:N_EXPERTS])
    choice = scores + bias_ref[...]
    iota = lax.broadcasted_iota(jnp.int32, (per, tr), 0)
    grp = [choice[g * per:(g + 1) * per] for g in range(N_EXPERT_GROUPS)]
    sc_g = [scores[g * per:(g + 1) * per] for g in range(N_EXPERT_GROUPS)]

    gsc = jnp.zeros((N_EXPERT_GROUPS, tr), F32)
    for g in range(N_EXPERT_GROUPS):
        top1 = jnp.max(grp[g], axis=0, keepdims=True)
        idx1 = jnp.min(jnp.where(grp[g] == top1, iota, per), axis=0, keepdims=True)
        top2 = jnp.max(jnp.where(iota == idx1, NEG, grp[g]), axis=0, keepdims=True)
        gsc = jnp.where(iota == g, top1 + top2, gsc)

    gsel = jnp.zeros((N_EXPERT_GROUPS, tr), F32)
    cur = gsc
    for _ in range(TOPK_GROUPS):
        mx = jnp.max(cur, axis=0, keepdims=True)
        idx = jnp.min(jnp.where(cur == mx, iota, N_EXPERT_GROUPS), axis=0, keepdims=True)
        hit = iota == idx
        gsel = jnp.where(hit, 1.0, gsel)
        cur = jnp.where(hit, NEG, cur)

    cur_g = [jnp.where(gsel[g:g + 1] > 0.0, grp[g], NEG) for g in range(N_EXPERT_GROUPS)]
    idx_rows = jnp.zeros((TOP_K, tr), jnp.int32)
    w_rows = jnp.zeros((TOP_K, tr), F32)
    for kk in range(TOP_K):
        mx = cur_g[0]
        for g in range(1, N_EXPERT_GROUPS):
            mx = jnp.maximum(mx, cur_g[g])
        mx = jnp.max(mx, axis=0, keepdims=True)
        idx = jnp.where(cur_g[0] == mx, iota, N_EXPERTS)
        for g in range(1, N_EXPERT_GROUPS):
            idx = jnp.minimum(idx, jnp.where(cur_g[g] == mx, iota + g * per, N_EXPERTS))
        idx = jnp.min(idx, axis=0, keepdims=True)
        wk = jnp.zeros((per, tr), F32)
        for g in range(N_EXPERT_GROUPS):
            hit = (iota + g * per) == idx
            wk = wk + jnp.where(hit, sc_g[g], 0.0)
            cur_g[g] = jnp.where(hit, NEG, cur_g[g])
        idx_rows = jnp.where(iota == kk, idx, idx_rows)
        w_rows = jnp.where(iota == kk, jnp.sum(wk, axis=0, keepdims=True), w_rows)

    wsum = jnp.sum(w_rows, axis=0, keepdims=True)
    idx_ref[...] = idx_rows
    w_ref[...] = w_rows / wsum * ROUTED_SCALE


def _router(logits, router_bias, seq):
    t = logits.shape[0]
    tr = min(1024, seq)
    bias_b = jnp.broadcast_to(router_bias.astype(F32)[:, None], (N_EXPERTS, tr))
    return pl.pallas_call(
        _router_body,
        grid=(t // tr,),
        in_specs=[pl.BlockSpec((tr, LANES), lambda i: (i, 0)),
                  pl.BlockSpec((N_EXPERTS, tr), lambda i: (0, 0))],
        out_specs=[pl.BlockSpec((TOP_K, tr), lambda i: (0, i)),
                   pl.BlockSpec((TOP_K, tr), lambda i: (0, i))],
        out_shape=[jax.ShapeDtypeStruct((TOP_K, t), jnp.int32),
                   jax.ShapeDtypeStruct((TOP_K, t), F32)],
        compiler_params=_params(("arbitrary",)),
        name="router",
    )(logits, bias_b)


def _route_tables(idx_t, w_t, groups, tg, rows, ntiles):
    i32 = jnp.int32
    e = idx_t.reshape(TOP_K, groups, tg).transpose(1, 0, 2).reshape(groups, TOP_K * tg)
    w = w_t.reshape(TOP_K, groups, tg).transpose(1, 0, 2).reshape(groups, TOP_K * tg)
    tl = jnp.broadcast_to(jnp.arange(tg, dtype=i32)[None, None, :], (groups, TOP_K, tg)).reshape(groups, TOP_K * tg)
    skey, sw = lax.sort((e * tg + tl, w), dimension=1, num_keys=1)
    stok = skey % tg
    cnt = jnp.sum((e[:, :, None] == jnp.arange(N_EXPERTS, dtype=i32)[None, None, :]).astype(i32), axis=1)
    off = jnp.cumsum(cnt, axis=1) - cnt
    ntile = (cnt + rows - 1) // rows
    tend = jnp.cumsum(ntile, axis=1)
    tstart = tend - ntile
    n = jnp.arange(ntiles, dtype=i32)
    te = jnp.sum((n[None, :, None] >= tend[:, None, :]).astype(i32), axis=2)
    valid_tile = n[None, :] < tend[:, -1:]
    te = jnp.minimum(te, N_EXPERTS - 1)
    k0 = (n[None, :] - jnp.take_along_axis(tstart, te, axis=1)) * rows
    nvalid = jnp.where(valid_tile, jnp.clip(jnp.take_along_axis(cnt, te, axis=1) - k0, 0, rows), 0)
    start = jnp.clip(jnp.take_along_axis(off, te, axis=1) + k0, 0, TOP_K * tg - 1)
    tok_g, tok_s, wv = _tile_windows(start, nvalid, stok, sw, groups, tg, rows, ntiles)
    return te, nvalid, tok_g, tok_s, wv


def _windows_body(start_ref, nv_ref, stok_ref, sw_ref, tokg_ref, toks_ref, wt_ref, *, rows, tg):
    g = pl.program_id(0)
    nt = tokg_ref.shape[1]
    nl = -(-rows // LANES)
    lane = lax.broadcasted_iota(jnp.int32, (1, LANES), 1)
    rr = lax.broadcasted_iota(jnp.int32, (1, nl * LANES), 1)

    def window(src, q, sh):
        blk = src[0, pl.ds(q, SUB), :]
        rolled = pltpu.roll(blk, (LANES - sh) % LANES, 1)
        parts = [jnp.where(lane < LANES - sh, rolled[k:k + 1], rolled[k + 1:k + 2]) for k in range(nl)]
        return jnp.concatenate(parts, axis=1)

    def body(n, carry):
        s = start_ref[g, n]
        q = s // LANES
        sh = s % LANES
        valid = rr < nv_ref[g, n]
        tok8 = window(stok_ref, q, sh) * SUB
        wv = window(sw_ref, q, sh)
        tokg_ref[0, n] = jnp.where(valid, tok8, 0)[:, :rows]
        toks_ref[0, n] = jnp.where(valid, tok8, tg * SUB)[:, :rows]
        wt_ref[0, n] = jnp.where(valid, wv, 0.0)[:, :rows]
        return carry

    lax.fori_loop(0, nt, body, 0)


def _tile_windows(start, nvalid, stok, sw, groups, tg, rows, ntiles):
    n_list = TOP_K * tg
    n_rows = n_list // LANES + SUB
    pad = n_rows * LANES - n_list
    stok2 = jnp.pad(stok, ((0, 0), (0, pad))).reshape(groups, n_rows, LANES)
    sw2 = jnp.pad(sw, ((0, 0), (0, pad))).reshape(groups, n_rows, LANES)
    lst = pl.BlockSpec((1, n_rows, LANES), lambda g, st, nv: (g, 0, 0))
    out = pl.BlockSpec((1, ntiles, 1, rows), lambda g, st, nv: (g, 0, 0, 0))
    grid_spec = pltpu.PrefetchScalarGridSpec(
        num_scalar_prefetch=2, grid=(groups,), in_specs=[lst, lst], out_specs=[out, out, out])
    shape = (groups, ntiles, 1, rows)
    return pl.pallas_call(
        functools.partial(_windows_body, rows=rows, tg=tg),
        grid_spec=grid_spec,
        out_shape=[jax.ShapeDtypeStruct(shape, jnp.int32), jax.ShapeDtypeStruct(shape, jnp.int32),
                   jax.ShapeDtypeStruct(shape, F32)],
        compiler_params=_params(("arbitrary",)),
        name="windows",
    )(start, nvalid, stok2, sw2)


def _routed_body(te_ref, nv_ref, tokg_ref, toks_ref, wt_ref, h_ref, wg_ref, wu_ref, wd_ref, acc_ref,
                 xbuf, ybuf):
    g = pl.program_id(0)
    n = pl.program_id(1)
    rows = MOE_ROWS
    slot = n % 2
    has_cur = nv_ref[g, n + 1] > 0
    has_prev = nv_ref[g, n] > 0

    @pl.when(n == 0)
    def _():
        acc_ref[...] = jnp.zeros(acc_ref.shape, F32)

    def compute():
        for r in range(rows):
            t8 = pl.multiple_of(tokg_ref[0, 0, 0, r], SUB)
            xbuf[pl.ds(r * SUB, SUB), :] = h_ref[pl.ds(t8, SUB), :]
        x2d = jnp.concatenate([xbuf[pl.ds(s, rows, stride=SUB), :] for s in range(SUB)], axis=1).astype(BF16)
        a = _dot(x2d, wg_ref[0].astype(BF16))
        b = _dot(x2d, wu_ref[0].astype(BF16))
        act = (a * _sigmoid(a) * b).astype(BF16)
        y = _dot(act, wd_ref[0].astype(BF16))
        for s in range(SUB):
            ybuf[slot, pl.ds(s, rows, stride=SUB), :] = y[:, s * LANES:(s + 1) * LANES]

    def scatter_prev():
        for r in range(rows):
            t8 = pl.multiple_of(toks_ref[0, 0, 0, r], SUB)
            ybuf[1 - slot, pl.ds(r * SUB, SUB), :] = (acc_ref[0, pl.ds(t8, SUB), :]
                                                      + wt_ref[0, 0, 0, r] * ybuf[1 - slot, pl.ds(r * SUB, SUB), :])
        for r in range(rows):
            t8 = pl.multiple_of(toks_ref[0, 0, 0, r], SUB)
            acc_ref[0, pl.ds(t8, SUB), :] = ybuf[1 - slot, pl.ds(r * SUB, SUB), :]

    @pl.when(jnp.logical_and(has_cur, has_prev))
    def _():
        scatter_prev()
        compute()

    @pl.when(jnp.logical_and(has_cur, jnp.logical_not(has_prev)))
    def _():
        compute()

    @pl.when(jnp.logical_and(jnp.logical_not(has_cur), has_prev))
    def _():
        scatter_prev()


def _routed(hv, tables, w_eg, w_eu, w_ed, groups, tg, ntiles):
    te, nvalid, tok_g, tok_s, wv = tables
    d, f = D_MODEL, EXPERT_DIM
    rows = MOE_ROWS
    i32 = jnp.int32
    te = jnp.concatenate([te, te[:, -1:]], axis=1)
    zero = jnp.zeros((groups, 1), i32)
    nvp = jnp.concatenate([zero, nvalid, zero], axis=1)
    cur = lambda g, n, te, nv: (g, jnp.minimum(n, ntiles - 1), 0, 0)
    prev = lambda g, n, te, nv: (g, jnp.maximum(n - 1, 0), 0, 0)
    smem = functools.partial(pl.BlockSpec, (1, 1, 1, rows), memory_space=pltpu.SMEM)
    grid_spec = pltpu.PrefetchScalarGridSpec(
        num_scalar_prefetch=2,
        grid=(groups, ntiles + 1),
        in_specs=[smem(index_map=cur), smem(index_map=prev), smem(index_map=prev),
                  pl.BlockSpec((tg * SUB, LANES), lambda g, n, te, nv: (g, 0), pipeline_mode=pl.Buffered(1)),
                  pl.BlockSpec((1, d, f), lambda g, n, te, nv: (te[g, n], 0, 0)),
                  pl.BlockSpec((1, d, f), lambda g, n, te, nv: (te[g, n], 0, 0)),
                  pl.BlockSpec((1, f, d), lambda g, n, te, nv: (te[g, n], 0, 0))],
        out_specs=pl.BlockSpec((1, (tg + 1) * SUB, LANES), lambda g, n, te, nv: (g, 0, 0),
                               pipeline_mode=pl.Buffered(1)),
        scratch_shapes=[pltpu.VMEM((rows * SUB, LANES), F32), pltpu.VMEM((2, rows * SUB, LANES), F32)],
    )
    return pl.pallas_call(
        _routed_body,
        grid_spec=grid_spec,
        out_shape=jax.ShapeDtypeStruct((groups, (tg + 1) * SUB, LANES), F32),
        compiler_params=_params(("arbitrary", "arbitrary")),
        name="routed",
    )(te, nvp, tok_g, tok_s, wv, hv, w_eg, w_eu, w_ed)


def _combine_body(h_ref, r_ref, x1_ref, g2_ref, sg_ref, su_ref, sd_ref, o_ref):
    tm = x1_ref.shape[0]
    h = jnp.concatenate([h_ref[pl.ds(s, tm, stride=SUB), :] for s in range(SUB)], axis=1)
    routed = jnp.concatenate([r_ref[0, pl.ds(s, tm, stride=SUB), :] for s in range(SUB)], axis=1)
    hb = h.astype(BF16)
    a = _dot(hb, sg_ref[...])
    b = _dot(hb, su_ref[...])
    shared = _dot((a * _sigmoid(a) * b).astype(BF16), sd_ref[...])
    o_ref[...] = x1_ref[...] + g2_ref[0] * (routed + shared)


def _combine(hv, routed, x1, g2, w_sg, w_su, w_sd, seq, tg):
    t, d = x1.shape
    tm = min(1024, seq, tg)
    per_b = seq // tm
    per_g = tg // tm
    f = w_sg.shape[1]
    const = lambda i: (0, 0)
    row = lambda i: (i, 0)
    return pl.pallas_call(
        _combine_body,
        grid=(t // tm,),
        in_specs=[pl.BlockSpec((tm * SUB, LANES), row),
                  pl.BlockSpec((1, tm * SUB, LANES), lambda i: (i // per_g, i % per_g, 0)),
                  pl.BlockSpec((tm, d), row),
                  pl.BlockSpec((1, 1, d), lambda i: (i // per_b, 0, 0)),
                  pl.BlockSpec((d, f), const),
                  pl.BlockSpec((d, f), const),
                  pl.BlockSpec((f, d), const)],
        out_specs=pl.BlockSpec((tm, d), row),
        out_shape=jax.ShapeDtypeStruct((t, d), F32),
        compiler_params=_params(("arbitrary",)),
        name="combine",
    )(hv, routed, x1, g2, w_sg, w_su, w_sd)


def kernel(x, c, positions, w_ada, b_ada, norm1_g, w_in, q_norm_g, k_norm_g, lambda_qk, subln_g, conv_w, conv_b, dt_bias, a_log, d_skip, ssd_norm_g, w_branch_a, w_branch_b, w_out, norm2_g, w_router, router_bias, w_exp_gate, w_exp_up, w_exp_down, w_sh_gate, w_sh_up, w_sh_down):
    batch, seq, d = x.shape
    t = batch * seq
    tq = min(1024, seq)
    tk = tq // 2
    xf = x.reshape(t, d)
    for layer in range(w_ada.shape[0]):
        lambda_init = 0.8 - 0.6 * math.exp(-0.3 * layer)
        c_pad = jnp.pad(c.astype(F32), ((0, 8 - batch), (0, 0)))
        mod = _ada(c_pad, w_ada[layer], b_ada[layer])[:batch].reshape(batch, 6, 1, d)
        sh1, sc1, g1, sh2, sc2, g2 = (mod[:, n] for n in range(6))

        w_l = w_in[layer]
        wdt = jnp.pad(w_l[:, MAIN_COLS:], ((0, 0), (0, LANES - SSD_HEADS)))
        wdt_hi, wdt_lo = _split2(wdt)
        proj, dt = _inproj(xf, norm1_g[layer], sc1, sh1, w_l[:, :MAIN_COLS].astype(BF16), wdt_hi, wdt_lo, seq)

        qt, kk, vt = _qkvprep(proj, positions, q_norm_g[layer], k_norm_g[layer], batch, seq, tk)
        ya = _attention(qt, kk, vt, lambda_qk[layer], subln_g[layer], lambda_init, batch, seq, tq, tk)
        yb = _ssd(proj, dt, conv_w[layer], conv_b[layer], dt_bias[layer], a_log[layer], d_skip[layer],
                  ssd_norm_g[layer], batch, seq)

        wr = jnp.pad(w_router[layer], ((0, 0), (0, LANES - N_EXPERTS))).astype(BF16)
        x1, h2, logits = _outproj(ya, yb, proj, xf, g1, sc2, sh2, norm2_g[layer],
                                  w_branch_a[layer].astype(BF16), w_branch_b[layer].astype(BF16),
                                  w_out[layer].astype(BF16), wr, seq)
        idx_t, w_t = _router(logits, router_bias[layer], seq)
        tg = min(4096, t)
        groups = t // tg
        ntiles = -(-tg * TOP_K // MOE_ROWS) + N_EXPERTS
        tables = _route_tables(idx_t, w_t, groups, tg, MOE_ROWS, ntiles)
        routed = _routed(h2, tables, w_exp_gate[layer], w_exp_up[layer], w_exp_down[layer], groups, tg, ntiles)
        xf = _combine(h2, routed, x1, g2, w_sh_gate[layer].astype(BF16), w_sh_up[layer].astype(BF16),
                      w_sh_down[layer].astype(BF16), seq, tg)
    return xf.reshape(batch, seq, d)
```

```python
import functools
import math

import jax
import jax.numpy as jnp
from jax import lax
from jax.experimental import pallas as pl
from jax.experimental.pallas import tpu as pltpu

F32 = jnp.float32
BF16 = jnp.bfloat16

D_MODEL = 1024
ATTN_HEADS = 8
ATTN_HEAD_DIM = 64
ATTN_V_DIM = 128
ROPE_THETA = 10000.0
SSD_D_INNER = 2048
SSD_HEAD_DIM = 64
SSD_HEADS = 32
SSD_GROUPS = 4
SSD_D_STATE = 128
SSD_CONV = 4
SSD_CHUNK = 128
SSD_CONV_DIM = 3072
N_EXPERTS = 64
TOP_K = 8
N_EXPERT_GROUPS = 8
TOPK_GROUPS = 4
EXPERT_DIM = 256
ROUTED_SCALE = 2.5
EPS = 1e-6
NEG = -1e30
LOG2E = math.log2(math.e)
Q_SCALE = ATTN_HEAD_DIM ** -0.5 * LOG2E
SSD_HALO = 16

LANES = 128
SUB = 8
MOE_ROWS = 576
MAIN_COLS = 10240
VMEM_LIMIT = 56 * 1024 * 1024


def _dot(a, b):
    return jnp.dot(a, b, preferred_element_type=F32)


def _sigmoid(x):
    return 1.0 / (1.0 + jnp.exp(-x))


def _split2(a):
    hi = a.astype(BF16)
    lo = (a - hi.astype(F32)).astype(BF16)
    return hi, lo


def _split3(a):
    hi = a.astype(BF16)
    r = a - hi.astype(F32)
    mid = r.astype(BF16)
    lo = (r - mid.astype(F32)).astype(BF16)
    return hi, mid, lo


def _params(sem, vmem=VMEM_LIMIT):
    return pltpu.CompilerParams(dimension_semantics=sem, vmem_limit_bytes=vmem)


def _ada_body(c_ref, w_ref, b_ref, o_ref):
    c = c_ref[...]
    sc = c * _sigmoid(c)
    chi, clo = _split2(sc)
    whi, wlo = _split2(w_ref[...])
    o_ref[...] = _dot(chi, whi) + _dot(chi, wlo) + _dot(clo, whi) + b_ref[...]


def _ada(c_pad, w, b):
    d, n = w.shape
    tn = 1024
    return pl.pallas_call(
        _ada_body,
        grid=(n // tn,),
        in_specs=[pl.BlockSpec((8, d), lambda j: (0, 0)),
                  pl.BlockSpec((d, tn), lambda j: (0, j)),
                  pl.BlockSpec((1, tn), lambda j: (0, j))],
        out_specs=pl.BlockSpec((8, tn), lambda j: (0, j)),
        out_shape=jax.ShapeDtypeStruct((8, n), F32),
        compiler_params=_params(("arbitrary",)),
        name="ada",
    )(c_pad, w, b.reshape(1, n))


def _inproj_body(x_ref, g_ref, sc_ref, sh_ref, w_ref, wdh_ref, wdl_ref, o_ref, dt_ref, h_scr):
    @pl.when(pl.program_id(1) == 0)
    def _():
        x = x_ref[...]
        ms = jnp.mean(x * x, axis=-1, keepdims=True)
        h = x * lax.rsqrt(ms + EPS) * g_ref[...]
        h = h * (1.0 + sc_ref[0]) + sh_ref[0]
        hb, hl = _split2(h)
        h_scr[...] = hb
        dt_ref[...] = _dot(hb, wdh_ref[...]) + _dot(hb, wdl_ref[...]) + _dot(hl, wdh_ref[...])

    o_ref[...] = _dot(h_scr[...], w_ref[...]).astype(BF16)


def _inproj(x2d, g, sc, sh, w_main, wdt_hi, wdt_lo, seq):
    t, d = x2d.shape
    tm = min(1024, seq)
    tn = 1024
    per_b = seq // tm
    return pl.pallas_call(
        _inproj_body,
        grid=(t // tm, MAIN_COLS // tn),
        in_specs=[pl.BlockSpec((tm, d), lambda i, j: (i, 0)),
                  pl.BlockSpec((1, d), lambda i, j: (0, 0)),
                  pl.BlockSpec((1, 1, d), lambda i, j: (i // per_b, 0, 0)),
                  pl.BlockSpec((1, 1, d), lambda i, j: (i // per_b, 0, 0)),
                  pl.BlockSpec((d, tn), lambda i, j: (0, j)),
                  pl.BlockSpec((d, LANES), lambda i, j: (0, 0)),
                  pl.BlockSpec((d, LANES), lambda i, j: (0, 0))],
        out_specs=[pl.BlockSpec((tm, tn), lambda i, j: (i, j)),
                   pl.BlockSpec((tm, LANES), lambda i, j: (i, 0))],
        out_shape=[jax.ShapeDtypeStruct((t, MAIN_COLS), BF16),
                   jax.ShapeDtypeStruct((t, LANES), F32)],
        scratch_shapes=[pltpu.VMEM((tm, d), BF16)],
        compiler_params=_params(("arbitrary", "arbitrary")),
        name="inproj",
    )(x2d, g.reshape(1, d), sc, sh, w_main, wdt_hi, wdt_lo)


def _qkvprep_body(q_ref, k_ref, v_ref, pos_ref, invf_ref, gq_ref, gk_ref, qt_ref, ko_ref, vt_ref, *, tk):
    ts = q_ref.shape[0]
    half = ATTN_HEAD_DIM // 2
    ang = pos_ref[0].astype(F32) * invf_ref[...]
    cos = jnp.cos(ang)
    sin = jnp.sin(ang)
    zeros = jnp.zeros((ATTN_HEAD_DIM, ts), F32)

    def norm_rope(blk, g):
        ms = jnp.mean(blk * blk, axis=0, keepdims=True)
        r = blk * lax.rsqrt(ms + EPS) * g
        x1 = r[:half]
        x2 = r[half:]
        return jnp.concatenate([x1 * cos - x2 * sin, x2 * cos + x1 * sin], axis=0)

    for h in range(ATTN_HEADS):
        sl = slice(h * LANES, (h + 1) * LANES)
        qht = q_ref[:, sl].astype(F32).T
        q1 = norm_rope(qht[:ATTN_HEAD_DIM], gq_ref[...]) * Q_SCALE
        q2 = norm_rope(qht[ATTN_HEAD_DIM:], gq_ref[...]) * Q_SCALE
        qt_ref[0, 2 * h] = jnp.concatenate([q1, zeros], axis=0).astype(BF16)
        qt_ref[0, 2 * h + 1] = jnp.concatenate([zeros, q2], axis=0).astype(BF16)
        kht = k_ref[:, sl].astype(F32).T
        k1 = norm_rope(kht[:ATTN_HEAD_DIM], gk_ref[...])
        k2 = norm_rope(kht[ATTN_HEAD_DIM:], gk_ref[...])
        ko_ref[0, h] = jnp.concatenate([k1, k2], axis=0).T.astype(BF16)
        vht = v_ref[:, sl].astype(F32).T.astype(BF16)
        for cc in range(ts // tk):
            vt_ref[0, h, cc] = vht[:, cc * tk:(cc + 1) * tk]


def _qkvprep(proj, positions, q_norm_g, k_norm_g, batch, seq, tk):
    ts = min(512, seq)
    ns = seq // ts
    half = ATTN_HEAD_DIM // 2
    inv_freq = ROPE_THETA ** (-jnp.arange(half, dtype=F32) * 2.0 / ATTN_HEAD_DIM)
    invf_b = jnp.broadcast_to(inv_freq[:, None], (half, ts))
    gq_b = jnp.broadcast_to(q_norm_g.astype(F32)[:, None], (ATTN_HEAD_DIM, ts))
    gk_b = jnp.broadcast_to(k_norm_g.astype(F32)[:, None], (ATTN_HEAD_DIM, ts))
    pos3 = positions.reshape(batch, 1, seq)
    const = lambda b, i: (0, 0)
    return pl.pallas_call(
        functools.partial(_qkvprep_body, tk=tk),
        grid=(batch, ns),
        in_specs=[pl.BlockSpec((ts, D_MODEL), lambda b, i: (b * ns + i, 0)),
                  pl.BlockSpec((ts, D_MODEL), lambda b, i: (b * ns + i, 1)),
                  pl.BlockSpec((ts, D_MODEL), lambda b, i: (b * ns + i, 2)),
                  pl.BlockSpec((1, 1, ts), lambda b, i: (b, 0, i)),
                  pl.BlockSpec((half, ts), const),
                  pl.BlockSpec((ATTN_HEAD_DIM, ts), const),
                  pl.BlockSpec((ATTN_HEAD_DIM, ts), const)],
        out_specs=[pl.BlockSpec((1, 2 * ATTN_HEADS, LANES, ts), lambda b, i: (b, 0, 0, i)),
                   pl.BlockSpec((1, ATTN_HEADS, ts, LANES), lambda b, i: (b, 0, i, 0)),
                   pl.BlockSpec((1, ATTN_HEADS, ts // tk, LANES, tk), lambda b, i: (b, 0, i, 0, 0))],
        out_shape=[jax.ShapeDtypeStruct((batch, 2 * ATTN_HEADS, LANES, seq), BF16),
                   jax.ShapeDtypeStruct((batch, ATTN_HEADS, seq, LANES), BF16),
                   jax.ShapeDtypeStruct((batch, ATTN_HEADS, seq // tk, LANES, tk), BF16)],
        compiler_params=_params(("arbitrary", "arbitrary")),
        name="qkvprep",
    )(proj, proj, proj, pos3, invf_b, gq_b, gk_b)


def _attn_body(lam_ref, sg_ref, qt_ref, k_ref, vt_ref, o_ref, acc_scr, sa_scr, sb_scr, *, tq, tk, lambda_init):
    i = pl.program_id(2)
    q_maps = (qt_ref[0, 0], qt_ref[0, 1])
    acc_scr[...] = jnp.zeros(acc_scr.shape, F32)
    tri = lax.broadcasted_iota(jnp.int32, (tk, tk), 0) <= lax.broadcasted_iota(jnp.int32, (tk, tk), 1)

    def qk(j, s_scr, lo=0):
        kb = k_ref[0, 0, pl.ds(pl.multiple_of(j * tk, tk), tk), :]
        for mi in range(2):
            s_scr[mi, :, lo:] = _dot(kb, q_maps[mi][:, lo:])

    def process(j, s_scr, carry, diag=None, lo=0):
        vb = vt_ref[0, 0, j]
        w = tq - lo
        out = []
        for mi in range(2):
            m_old, l_old = carry[2 * mi][:, lo:], carry[2 * mi + 1][:, lo:]
            if diag is None:
                s = s_scr[mi]
            elif diag == 0:
                s = jnp.concatenate([jnp.where(tri, s_scr[mi, :, :tk], NEG), s_scr[mi, :, tk:]], axis=1)
            else:
                s = jnp.where(tri, s_scr[mi, :, lo:], NEG)
            m_new = jnp.maximum(m_old, jnp.max(s, axis=0, keepdims=True))
            alpha = jnp.exp2(m_old - m_new)
            p = jnp.exp2(s - m_new[0:1])
            l_new = alpha * l_old + jnp.sum(p, axis=0, keepdims=True)
            acc_scr[mi, :, lo:] = (jnp.broadcast_to(alpha[0:1], (ATTN_V_DIM, w)) * acc_scr[mi, :, lo:]
                                   + _dot(vb, p.astype(BF16)))
            if lo:
                m_new = jnp.concatenate([carry[2 * mi][:, :lo], m_new], axis=1)
                l_new = jnp.concatenate([carry[2 * mi + 1][:, :lo], l_new], axis=1)
            out += [m_new, l_new]
        return tuple(out)

    def pair(p, carry):
        qk(2 * p + 1, sb_scr)
        carry = process(2 * p, sa_scr, carry)
        qk(2 * p + 2, sa_scr)
        return process(2 * p + 1, sb_scr, carry)

    init = (jnp.full((SUB, tq), NEG, F32), jnp.zeros((SUB, tq), F32)) * 2
    qk(0, sa_scr)
    carry = lax.fori_loop(0, i, pair, init)
    qk(2 * i + 1, sb_scr, lo=tk)
    carry = process(2 * i, sa_scr, carry, diag=0)
    carry = process(2 * i + 1, sb_scr, carry, diag=1, lo=tk)

    lv = lam_ref[...]
    lam = (jnp.exp(jnp.sum(lv[0:1] * lv[1:2], keepdims=True))
           - jnp.exp(jnp.sum(lv[2:3] * lv[3:4], keepdims=True)) + lambda_init)
    o = acc_scr[0] / carry[1][0:1] - lam * (acc_scr[1] / carry[3][0:1])
    ms = jnp.mean(o * o, axis=0, keepdims=True)
    o = o * lax.rsqrt(ms + EPS) * sg_ref[...] * (1.0 - lambda_init)
    o_ref[...] = o.T.astype(BF16)


def _attention(qt, kk, vt, lambda_qk, subln_g, lambda_init, batch, seq, tq, tk):
    nq = seq // tq
    sg_b = jnp.broadcast_to(subln_g.astype(F32)[:, None], (ATTN_V_DIM, tq))
    return pl.pallas_call(
        functools.partial(_attn_body, tq=tq, tk=tk, lambda_init=lambda_init),
        grid=(batch, ATTN_HEADS, nq),
        in_specs=[pl.BlockSpec((4, ATTN_HEAD_DIM), lambda b, h, i: (0, 0)),
                  pl.BlockSpec((ATTN_V_DIM, tq), lambda b, h, i: (0, 0)),
                  pl.BlockSpec((1, 2, LANES, tq), lambda b, h, i: (b, h, 0, i)),
                  pl.BlockSpec((1, 1, seq, LANES), lambda b, h, i: (b, h, 0, 0)),
                  pl.BlockSpec((1, 1, seq // tk, LANES, tk), lambda b, h, i: (b, h, 0, 0, 0))],
        out_specs=pl.BlockSpec((tq, LANES), lambda b, h, i: (b * nq + i, h)),
        out_shape=jax.ShapeDtypeStruct((batch * seq, ATTN_HEADS * ATTN_V_DIM), BF16),
        scratch_shapes=[pltpu.VMEM((2, ATTN_V_DIM, tq), F32),
                        pltpu.VMEM((2, tk, tq), F32),
                        pltpu.VMEM((2, tk, tq), F32)],
        compiler_params=_params(("arbitrary", "arbitrary", "arbitrary")),
        name="attn",
    )(lambda_qk.astype(F32), sg_b, qt, kk, vt)


def _ssd_body(z0_ref, z1_ref, x0_ref, x1_ref, bc_ref, dt_ref, cw_ref, cb_ref, dtb_ref, alog_ref,
              dsk_ref, ng_ref, tri_ref, shift_ref, y_ref, ext_scr, st_scr):
    L = SSD_CHUNK
    halo = SSD_HALO
    c = pl.program_id(1)

    @pl.when(c == 0)
    def _():
        ext_scr[0:halo, :] = jnp.zeros((halo, SSD_CONV_DIM), BF16)
        st_scr[...] = jnp.zeros(st_scr.shape, F32)

    ext_scr[halo:halo + L, 0:1024] = x0_ref[...]
    ext_scr[halo:halo + L, 1024:2048] = x1_ref[...]
    ext_scr[halo:halo + L, 2048:3072] = bc_ref[...]
    sh = _dot(shift_ref[...], ext_scr[...])
    acc = cb_ref[...] + cw_ref[SSD_CONV - 1:SSD_CONV, :] * ext_scr[halo:halo + L, :].astype(F32)
    for k in range(SSD_CONV - 1):
        acc = acc + cw_ref[k:k + 1, :] * sh[k * L:(k + 1) * L]
    u = acc * _sigmoid(acc)
    ext_scr[0:halo, :] = ext_scr[L:L + halo, :]

    xdt_in = dt_ref[...] + dtb_ref[...]
    dtp = jnp.maximum(xdt_in, 0.0) + jnp.log(1.0 + jnp.exp(-jnp.abs(xdt_in)))
    a = dtp * (-jnp.exp(alog_ref[...]) * LOG2E)
    tri = tri_ref[...]
    a_hi, a_mid, a_lo = _split3(a)
    acum = _dot(tri, a_hi) + _dot(tri, a_mid) + _dot(tri, a_lo)
    acum_t = acum.T
    lane = lax.broadcasted_iota(jnp.int32, (L, LANES), 1)
    first = lane < SSD_HEAD_DIM
    tril = lax.broadcasted_iota(jnp.int32, (L, L), 0) >= lax.broadcasted_iota(jnp.int32, (L, L), 1)
    gn = SSD_GROUPS * SSD_D_STATE
    gw = SSD_D_INNER // SSD_GROUPS

    def col(mat, h):
        return jnp.broadcast_to(mat[:, h:h + 1], (L, LANES))

    for g in range(SSD_GROUPS):
        bg = u[:, SSD_D_INNER + g * SSD_D_STATE:SSD_D_INNER + (g + 1) * SSD_D_STATE]
        cg = u[:, SSD_D_INNER + gn + g * SSD_D_STATE:SSD_D_INNER + gn + (g + 1) * SSD_D_STATE].astype(BF16)
        bgt = bg.T.astype(BF16)
        cb = _dot(cg, bgt)
        sprev = st_scr[g]
        yoff = _dot(cg, sprev.astype(BF16))
        y_parts, xdd_parts, dec_parts = [], [], []
        for p in range(4):
            h0 = 8 * g + 2 * p
            c0, c1 = col(acum, h0), col(acum, h0 + 1)
            ac_exp = jnp.where(first, c0, c1)
            dt_exp = jnp.where(first, col(dtp, h0), col(dtp, h0 + 1))
            xsl = u[:, h0 * SSD_HEAD_DIM:h0 * SSD_HEAD_DIM + LANES]
            xdt = xsl * dt_exp
            l0 = jnp.exp2(jnp.where(tril, c0 - acum_t[h0:h0 + 1, :], NEG))
            l1 = jnp.exp2(jnp.where(tril, c1 - acum_t[h0 + 1:h0 + 2, :], NEG))
            m2 = jnp.concatenate([(cb * l0).astype(BF16), (cb * l1).astype(BF16)], axis=1)
            xb = xdt.astype(BF16)
            zb = jnp.zeros_like(xb)
            rhs = jnp.concatenate([jnp.where(first, xb, zb), jnp.where(first, zb, xb)], axis=0)
            yd = _dot(m2, rhs)
            yo = yoff[:, p * LANES:(p + 1) * LANES] * jnp.exp2(ac_exp)
            dsk = dsk_ref[:, h0 * SSD_HEAD_DIM:h0 * SSD_HEAD_DIM + LANES]
            y_parts.append(yd + yo + xsl * dsk)
            last = ac_exp[L - 1:L, :]
            xdd_parts.append((xdt * jnp.exp2(last - ac_exp)).astype(BF16))
            dec_parts.append(jnp.exp2(last))
        xdd = jnp.concatenate(xdd_parts, axis=1)
        dec = jnp.concatenate(dec_parts, axis=1)
        st_scr[g] = sprev * dec + _dot(bgt, xdd)
        yg = jnp.concatenate(y_parts, axis=1)
        zref = z0_ref if g < 2 else z1_ref
        zg = zref[:, (g % 2) * gw:(g % 2 + 1) * gw].astype(F32)
        yg = yg * (zg * _sigmoid(zg))
        ms = jnp.mean(yg * yg, axis=-1, keepdims=True)
        yg = yg * lax.rsqrt(ms + EPS) * ng_ref[:, g * gw:(g + 1) * gw]
        y_ref[:, g * gw:(g + 1) * gw] = yg.astype(BF16)


def _ssd(proj, dt, conv_w, conv_b, dt_bias, a_log, d_skip, norm_g, batch, seq):
    L = SSD_CHUNK
    nc = seq // L
    pad = LANES - SSD_HEADS
    dtb = jnp.pad(dt_bias.astype(F32), (0, pad)).reshape(1, LANES)
    alog = jnp.pad(a_log.astype(F32), (0, pad)).reshape(1, LANES)
    dsk = jnp.repeat(d_skip.astype(F32), SSD_HEAD_DIM).reshape(1, SSD_D_INNER)
    tri = jnp.tril(jnp.ones((L, L), F32)).astype(BF16)
    tt = jnp.arange(L)
    shift = jnp.concatenate(
        [jax.nn.one_hot(SSD_HALO + tt - (SSD_CONV - 1) + k, L + SSD_HALO, dtype=BF16) for k in range(SSD_CONV - 1)],
        axis=0)
    row = lambda b, c: (b * nc + c)
    const = lambda b, c: (0, 0)
    return pl.pallas_call(
        _ssd_body,
        grid=(batch, nc),
        in_specs=[pl.BlockSpec((L, 1024), lambda b, c: (row(b, c), 5)),
                  pl.BlockSpec((L, 1024), lambda b, c: (row(b, c), 6)),
                  pl.BlockSpec((L, 1024), lambda b, c: (row(b, c), 7)),
                  pl.BlockSpec((L, 1024), lambda b, c: (row(b, c), 8)),
                  pl.BlockSpec((L, 1024), lambda b, c: (row(b, c), 9)),
                  pl.BlockSpec((L, LANES), lambda b, c: (row(b, c), 0)),
                  pl.BlockSpec((SSD_CONV, SSD_CONV_DIM), const),
                  pl.BlockSpec((1, SSD_CONV_DIM), const),
                  pl.BlockSpec((1, LANES), const),
                  pl.BlockSpec((1, LANES), const),
                  pl.BlockSpec((1, SSD_D_INNER), const),
                  pl.BlockSpec((1, SSD_D_INNER), const),
                  pl.BlockSpec((L, L), const),
                  pl.BlockSpec(((SSD_CONV - 1) * L, L + SSD_HALO), const)],
        out_specs=pl.BlockSpec((L, SSD_D_INNER), lambda b, c: (row(b, c), 0)),
        out_shape=jax.ShapeDtypeStruct((batch * seq, SSD_D_INNER), BF16),
        scratch_shapes=[pltpu.VMEM((L + SSD_HALO, SSD_CONV_DIM), BF16),
                        pltpu.VMEM((SSD_GROUPS, SSD_D_STATE, SSD_D_INNER // SSD_GROUPS), F32)],
        compiler_params=_params(("arbitrary", "arbitrary")),
        name="ssd",
    )(proj, proj, proj, proj, proj, dt, conv_w.astype(F32), conv_b.astype(F32).reshape(1, -1), dtb, alog,
      dsk, norm_g.astype(F32).reshape(1, -1), tri, shift)


def _outproj_body(ya_ref, yb_ref, ga_ref, gb_ref, x_ref, g1_ref, sc_ref, sh_ref, n2_ref,
                  wa_ref, wb_ref, wo_ref, wr_ref, x1_ref, h2_ref, lg_ref):
    pa = _dot(ya_ref[...], wa_ref[...])
    pb = _dot(yb_ref[...], wb_ref[...])
    merged = _sigmoid(ga_ref[...].astype(F32)) * pa + _sigmoid(gb_ref[...].astype(F32)) * pb
    x1 = x_ref[...] + g1_ref[0] * _dot(merged.astype(BF16), wo_ref[...])
    x1_ref[...] = x1
    ms = jnp.mean(x1 * x1, axis=-1, keepdims=True)
    h2 = x1 * lax.rsqrt(ms + EPS) * n2_ref[...]
    h2 = h2 * (1.0 + sc_ref[0]) + sh_ref[0]
    tm = h2.shape[0]
    for s in range(SUB):
        h2_ref[pl.ds(s, tm, stride=SUB), :] = h2[:, s * LANES:(s + 1) * LANES]
    lg_ref[...] = _dot(h2.astype(BF16), wr_ref[...])


def _outproj(ya, yb, proj, x2d, g1, sc2, sh2, norm2_g, wa, wb, wo, wr, seq):
    t, d = x2d.shape
    tm = min(512, seq)
    per_b = seq // tm
    const = lambda i: (0, 0)
    mod = lambda i: (i // per_b, 0, 0)
    return pl.pallas_call(
        _outproj_body,
        grid=(t // tm,),
        in_specs=[pl.BlockSpec((tm, d), lambda i: (i, 0)),
                  pl.BlockSpec((tm, SSD_D_INNER), lambda i: (i, 0)),
                  pl.BlockSpec((tm, d), lambda i: (i, 3)),
                  pl.BlockSpec((tm, d), lambda i: (i, 4)),
                  pl.BlockSpec((tm, d), lambda i: (i, 0)),
                  pl.BlockSpec((1, 1, d), mod),
                  pl.BlockSpec((1, 1, d), mod),
                  pl.BlockSpec((1, 1, d), mod),
                  pl.BlockSpec((1, d), const),
                  pl.BlockSpec((d, d), const),
                  pl.BlockSpec((SSD_D_INNER, d), const),
                  pl.BlockSpec((d, d), const),
                  pl.BlockSpec((d, LANES), const)],
        out_specs=[pl.BlockSpec((tm, d), lambda i: (i, 0)),
                   pl.BlockSpec((tm * SUB, LANES), lambda i: (i, 0)),
                   pl.BlockSpec((tm, LANES), lambda i: (i, 0))],
        out_shape=[jax.ShapeDtypeStruct((t, d), F32),
                   jax.ShapeDtypeStruct((t * SUB, LANES), F32),
                   jax.ShapeDtypeStruct((t, LANES), F32)],
        compiler_params=_params(("arbitrary",)),
        name="outproj",
    )(ya, yb, proj, proj, x2d, g1, sc2, sh2, norm2_g.astype(F32).reshape(1, d), wa, wb, wo, wr)


def _router_body(lg_ref, bias_ref, idx_ref, w_ref):
    tr = lg_ref.shape[0]
    per = N_EXPERTS // N_EXPERT_GROUPS
    lt = lg_ref[...].T
    scores = _sigmoid(lt[:N_EXPERTS])
    choice = scores + bias_ref[...]
    iota = lax.broadcasted_iota(jnp.int32, (per, tr), 0)
    grp = [choice[g * per:(g + 1) * per] for g in range(N_EXPERT_GROUPS)]
    sc_g = [scores[g * per:(g + 1) * per] for g in range(N_EXPERT_GROUPS)]

    gsc = jnp.zeros((N_EXPERT_GROUPS, tr), F32)
    for g in range(N_EXPERT_GROUPS):
        top1 = jnp.max(grp[g], axis=0, keepdims=True)
        idx1 = jnp.min(jnp.where(grp[g] == top1, iota, per), axis=0, keepdims=True)
        top2 = jnp.max(jnp.where(iota == idx1, NEG, grp[g]), axis=0, keepdims=True)
        gsc = jnp.where(iota == g, top1 + top2, gsc)

    gsel = jnp.zeros((N_EXPERT_GROUPS, tr), F32)
    cur = gsc
    for _ in range(TOPK_GROUPS):
        mx = jnp.max(cur, axis=0, keepdims=True)
        idx = jnp.min(jnp.where(cur == mx, iota, N_EXPERT_GROUPS), axis=0, keepdims=True)
        hit = iota == idx
        gsel = jnp.where(hit, 1.0, gsel)
        cur = jnp.where(hit, NEG, cur)

    cur_g = [jnp.where(gsel[g:g + 1] > 0.0, grp[g], NEG) for g in range(N_EXPERT_GROUPS)]
    idx_rows = jnp.zeros((TOP_K, tr), jnp.int32)
    w_rows = jnp.zeros((TOP_K, tr), F32)
    for kk in range(TOP_K):
        mx = cur_g[0]
        for g in range(1, N_EXPERT_GROUPS):
            mx = jnp.maximum(mx, cur_g[g])
        mx = jnp.max(mx, axis=0, keepdims=True)
        idx = jnp.where(cur_g[0] == mx, iota, N_EXPERTS)
        for g in range(1, N_EXPERT_GROUPS):
            idx = jnp.minimum(idx, jnp.where(cur_g[g] == mx, iota + g * per, N_EXPERTS))
        idx = jnp.min(idx, axis=0, keepdims=True)
        wk = jnp.zeros((per, tr), F32)
        for g in range(N_EXPERT_GROUPS):
            hit = (iota + g * per) == idx
            wk = wk + jnp.where(hit, sc_g[g], 0.0)
            cur_g[g] = jnp.where(hit, NEG, cur_g[g])
        idx_rows = jnp.where(iota == kk, idx, idx_rows)
        w_rows = jnp.where(iota == kk, jnp.sum(wk, axis=0, keepdims=True), w_rows)

    wsum = jnp.sum(w_rows, axis=0, keepdims=True)
    idx_ref[...] = idx_rows
    w_ref[...] = w_rows / wsum * ROUTED_SCALE


def _router(logits, router_bias, seq):
    t = logits.shape[0]
    tr = min(1024, seq)
    bias_b = jnp.broadcast_to(router_bias.astype(F32)[:, None], (N_EXPERTS, tr))
    return pl.pallas_call(
        _router_body,
        grid=(t // tr,),
        in_specs=[pl.BlockSpec((tr, LANES), lambda i: (i, 0)),
                  pl.BlockSpec((N_EXPERTS, tr), lambda i: (0, 0))],
        out_specs=[pl.BlockSpec((TOP_K, tr), lambda i: (0, i)),
                   pl.BlockSpec((TOP_K, tr), lambda i: (0, i))],
        out_shape=[jax.ShapeDtypeStruct((TOP_K, t), jnp.int32),
                   jax.ShapeDtypeStruct((TOP_K, t), F32)],
        compiler_params=_params(("arbitrary",)),
        name="router",
    )(logits, bias_b)


def _route_tables(idx_t, w_t, groups, tg, rows, ntiles):
    i32 = jnp.int32
    e = idx_t.reshape(TOP_K, groups, tg).transpose(1, 0, 2).reshape(groups, TOP_K * tg)
    w = w_t.reshape(TOP_K, groups, tg).transpose(1, 0, 2).reshape(groups, TOP_K * tg)
    tl = jnp.broadcast_to(jnp.arange(tg, dtype=i32)[None, None, :], (groups, TOP_K, tg)).reshape(groups, TOP_K * tg)
    skey, sw = lax.sort((e * tg + tl, w), dimension=1, num_keys=1)
    stok = skey % tg
    cnt = jnp.sum((e[:, :, None] == jnp.arange(N_EXPERTS, dtype=i32)[None, None, :]).astype(i32), axis=1)
    off = jnp.cumsum(cnt, axis=1) - cnt
    ntile = (cnt + rows - 1) // rows
    tend = jnp.cumsum(ntile, axis=1)
    tstart = tend - ntile
    n = jnp.arange(ntiles, dtype=i32)
    te = jnp.sum((n[None, :, None] >= tend[:, None, :]).astype(i32), axis=2)
    valid_tile = n[None, :] < tend[:, -1:]
    te = jnp.minimum(te, N_EXPERTS - 1)
    k0 = (n[None, :] - jnp.take_along_axis(tstart, te, axis=1)) * rows
    nvalid = jnp.where(valid_tile, jnp.clip(jnp.take_along_axis(cnt, te, axis=1) - k0, 0, rows), 0)
    start = jnp.clip(jnp.take_along_axis(off, te, axis=1) + k0, 0, TOP_K * tg - 1)
    tok_g, tok_s, wv = _tile_windows(start, nvalid, stok, sw, groups, tg, rows, ntiles)
    return te, nvalid, tok_g, tok_s, wv


def _windows_body(start_ref, nv_ref, stok_ref, sw_ref, tokg_ref, toks_ref, wt_ref, *, rows, tg):
    g = pl.program_id(0)
    nt = tokg_ref.shape[1]
    nl = -(-rows // LANES)
    lane = lax.broadcasted_iota(jnp.int32, (1, LANES), 1)
    rr = lax.broadcasted_iota(jnp.int32, (1, nl * LANES), 1)

    def window(src, q, sh):
        blk = src[0, pl.ds(q, SUB), :]
        rolled = pltpu.roll(blk, (LANES - sh) % LANES, 1)
        parts = [jnp.where(lane < LANES - sh, rolled[k:k + 1], rolled[k + 1:k + 2]) for k in range(nl)]
        return jnp.concatenate(parts, axis=1)

    def body(n, carry):
        s = start_ref[g, n]
        q = s // LANES
        sh = s % LANES
        valid = rr < nv_ref[g, n]
        tok8 = window(stok_ref, q, sh) * SUB
        wv = window(sw_ref, q, sh)
        tokg_ref[0, n] = jnp.where(valid, tok8, 0)[:, :rows]
        toks_ref[0, n] = jnp.where(valid, tok8, tg * SUB)[:, :rows]
        wt_ref[0, n] = jnp.where(valid, wv, 0.0)[:, :rows]
        return carry

    lax.fori_loop(0, nt, body, 0)


def _tile_windows(start, nvalid, stok, sw, groups, tg, rows, ntiles):
    n_list = TOP_K * tg
    n_rows = n_list // LANES + SUB
    pad = n_rows * LANES - n_list
    stok2 = jnp.pad(stok, ((0, 0), (0, pad))).reshape(groups, n_rows, LANES)
    sw2 = jnp.pad(sw, ((0, 0), (0, pad))).reshape(groups, n_rows, LANES)
    lst = pl.BlockSpec((1, n_rows, LANES), lambda g, st, nv: (g, 0, 0))
    out = pl.BlockSpec((1, ntiles, 1, rows), lambda g, st, nv: (g, 0, 0, 0))
    grid_spec = pltpu.PrefetchScalarGridSpec(
        num_scalar_prefetch=2, grid=(groups,), in_specs=[lst, lst], out_specs=[out, out, out])
    shape = (groups, ntiles, 1, rows)
    return pl.pallas_call(
        functools.partial(_windows_body, rows=rows, tg=tg),
        grid_spec=grid_spec,
        out_shape=[jax.ShapeDtypeStruct(shape, jnp.int32), jax.ShapeDtypeStruct(shape, jnp.int32),
                   jax.ShapeDtypeStruct(shape, F32)],
        compiler_params=_params(("arbitrary",)),
        name="windows",
    )(start, nvalid, stok2, sw2)


def _routed_body(te_ref, nv_ref, tokg_ref, toks_ref, wt_ref, h_ref, wg_ref, wu_ref, wd_ref, acc_ref,
                 xbuf, ybuf):
    g = pl.program_id(0)
    n = pl.program_id(1)
    rows = MOE_ROWS
    slot = n % 2
    nv_cur = nv_ref[g, n + 1]
    nv_prev = nv_ref[g, n]
    t_cur = (nv_cur + SUB - 1) // SUB
    t_prev = (nv_prev + SUB - 1) // SUB
    t_both = jnp.minimum(t_cur, t_prev)

    @pl.when(n == 0)
    def _():
        acc_ref[...] = jnp.zeros(acc_ref.shape, F32)
        xbuf[...] = jnp.zeros(xbuf.shape, F32)

    def row_chunk(c, gather, scatter):
        upd = []
        for k in range(SUB):
            r = c * SUB + k
            r8 = pl.multiple_of(r * SUB, SUB)
            if scatter:
                t8 = pl.multiple_of(toks_ref[0, 0, 0, r], SUB)
                upd.append((t8, acc_ref[0, pl.ds(t8, SUB), :] + wt_ref[0, 0, 0, r] * ybuf[1 - slot, pl.ds(r8, SUB), :]))
            if gather:
                t8 = pl.multiple_of(tokg_ref[0, 0, 0, r], SUB)
                xbuf[pl.ds(r8, SUB), :] = h_ref[pl.ds(t8, SUB), :]
        for t8, v in upd:
            acc_ref[0, pl.ds(t8, SUB), :] = v

    def loop(lo, hi, gather, scatter):
        def body(c, carry):
            row_chunk(c, gather, scatter)
            return carry
        lax.fori_loop(lo, hi, body, 0)

    loop(0, t_both, True, True)
    loop(t_both, t_cur, True, False)
    loop(t_both, t_prev, False, True)

    @pl.when(nv_cur > 0)
    def _():
        x2d = jnp.concatenate([xbuf[pl.ds(s, rows, stride=SUB), :] for s in range(SUB)], axis=1).astype(BF16)
        a = _dot(x2d, wg_ref[0].astype(BF16))
        b = _dot(x2d, wu_ref[0].astype(BF16))
        act = (a * _sigmoid(a) * b).astype(BF16)
        y = _dot(act, wd_ref[0].astype(BF16))
        for s in range(SUB):
            ybuf[slot, pl.ds(s, rows, stride=SUB), :] = y[:, s * LANES:(s + 1) * LANES]


def _routed(hv, tables, w_eg, w_eu, w_ed, groups, tg, ntiles):
    te, nvalid, tok_g, tok_s, wv = tables
    d, f = D_MODEL, EXPERT_DIM
    rows = MOE_ROWS
    i32 = jnp.int32
    te = jnp.concatenate([te, te[:, -1:]], axis=1)
    zero = jnp.zeros((groups, 1), i32)
    nvp = jnp.concatenate([zero, nvalid, zero], axis=1)
    cur = lambda g, n, te, nv: (g, jnp.minimum(n, ntiles - 1), 0, 0)
    prev = lambda g, n, te, nv: (g, jnp.maximum(n - 1, 0), 0, 0)
    smem = functools.partial(pl.BlockSpec, (1, 1, 1, rows), memory_space=pltpu.SMEM)
    grid_spec = pltpu.PrefetchScalarGridSpec(
        num_scalar_prefetch=2,
        grid=(groups, ntiles + 1),
        in_specs=[smem(index_map=cur), smem(index_map=prev), smem(index_map=prev),
                  pl.BlockSpec((tg * SUB, LANES), lambda g, n, te, nv: (g, 0), pipeline_mode=pl.Buffered(1)),
                  pl.BlockSpec((1, d, f), lambda g, n, te, nv: (te[g, n], 0, 0)),
                  pl.BlockSpec((1, d, f), lambda g, n, te, nv: (te[g, n], 0, 0)),
                  pl.BlockSpec((1, f, d), lambda g, n, te, nv: (te[g, n], 0, 0))],
        out_specs=pl.BlockSpec((1, (tg + 1) * SUB, LANES), lambda g, n, te, nv: (g, 0, 0),
                               pipeline_mode=pl.Buffered(1)),
        scratch_shapes=[pltpu.VMEM((rows * SUB, LANES), F32), pltpu.VMEM((2, rows * SUB, LANES), F32)],
    )
    return pl.pallas_call(
        _routed_body,
        grid_spec=grid_spec,
        out_shape=jax.ShapeDtypeStruct((groups, (tg + 1) * SUB, LANES), F32),
        compiler_params=_params(("arbitrary", "arbitrary")),
        name="routed",
    )(te, nvp, tok_g, tok_s, wv, hv, w_eg, w_eu, w_ed)


def _combine_body(h_ref, r_ref, x1_ref, g2_ref, sg_ref, su_ref, sd_ref, o_ref):
    tm = x1_ref.shape[0]
    h = jnp.concatenate([h_ref[pl.ds(s, tm, stride=SUB), :] for s in range(SUB)], axis=1)
    routed = jnp.concatenate([r_ref[0, pl.ds(s, tm, stride=SUB), :] for s in range(SUB)], axis=1)
    hb = h.astype(BF16)
    a = _dot(hb, sg_ref[...])
    b = _dot(hb, su_ref[...])
    shared = _dot((a * _sigmoid(a) * b).astype(BF16), sd_ref[...])
    o_ref[...] = x1_ref[...] + g2_ref[0] * (routed + shared)


def _combine(hv, routed, x1, g2, w_sg, w_su, w_sd, seq, tg):
    t, d = x1.shape
    tm = min(1024, seq, tg)
    per_b = seq // tm
    per_g = tg // tm
    f = w_sg.shape[1]
    const = lambda i: (0, 0)
    row = lambda i: (i, 0)
    return pl.pallas_call(
        _combine_body,
        grid=(t // tm,),
        in_specs=[pl.BlockSpec((tm * SUB, LANES), row),
                  pl.BlockSpec((1, tm * SUB, LANES), lambda i: (i // per_g, i % per_g, 0)),
                  pl.BlockSpec((tm, d), row),
                  pl.BlockSpec((1, 1, d), lambda i: (i // per_b, 0, 0)),
                  pl.BlockSpec((d, f), const),
                  pl.BlockSpec((d, f), const),
                  pl.BlockSpec((f, d), const)],
        out_specs=pl.BlockSpec((tm, d), row),
        out_shape=jax.ShapeDtypeStruct((t, d), F32),
        compiler_params=_params(("arbitrary",)),
        name="combine",
    )(hv, routed, x1, g2, w_sg, w_su, w_sd)


def kernel(x, c, positions, w_ada, b_ada, norm1_g, w_in, q_norm_g, k_norm_g, lambda_qk, subln_g, conv_w, conv_b, dt_bias, a_log, d_skip, ssd_norm_g, w_branch_a, w_branch_b, w_out, norm2_g, w_router, router_bias, w_exp_gate, w_exp_up, w_exp_down, w_sh_gate, w_sh_up, w_sh_down):
    batch, seq, d = x.shape
    t = batch * seq
    tq = min(1024, seq)
    tk = tq // 2
    xf = x.reshape(t, d)
    for layer in range(w_ada.shape[0]):
        lambda_init = 0.8 - 0.6 * math.exp(-0.3 * layer)
        c_pad = jnp.pad(c.astype(F32), ((0, 8 - batch), (0, 0)))
        mod = _ada(c_pad, w_ada[layer], b_ada[layer])[:batch].reshape(batch, 6, 1, d)
        sh1, sc1, g1, sh2, sc2, g2 = (mod[:, n] for n in range(6))

        w_l = w_in[layer]
        wdt = jnp.pad(w_l[:, MAIN_COLS:], ((0, 0), (0, LANES - SSD_HEADS)))
        wdt_hi, wdt_lo = _split2(wdt)
        proj, dt = _inproj(xf, norm1_g[layer], sc1, sh1, w_l[:, :MAIN_COLS].astype(BF16), wdt_hi, wdt_lo, seq)

        qt, kk, vt = _qkvprep(proj, positions, q_norm_g[layer], k_norm_g[layer], batch, seq, tk)
        ya = _attention(qt, kk, vt, lambda_qk[layer], subln_g[layer], lambda_init, batch, seq, tq, tk)
        yb = _ssd(proj, dt, conv_w[layer], conv_b[layer], dt_bias[layer], a_log[layer], d_skip[layer],
                  ssd_norm_g[layer], batch, seq)

        wr = jnp.pad(w_router[layer], ((0, 0), (0, LANES - N_EXPERTS))).astype(BF16)
        x1, h2, logits = _outproj(ya, yb, proj, xf, g1, sc2, sh2, norm2_g[layer],
                                  w_branch_a[layer].astype(BF16), w_branch_b[layer].astype(BF16),
                                  w_out[layer].astype(BF16), wr, seq)
        idx_t, w_t = _router(logits, router_bias[layer], seq)
        tg = min(4096, t)
        groups = t // tg
        ntiles = -(-tg * TOP_K // MOE_ROWS) + N_EXPERTS
        tables = _route_tables(idx_t, w_t, groups, tg, MOE_ROWS, ntiles)
        routed = _routed(h2, tables, w_exp_gate[layer], w_exp_up[layer], w_exp_down[layer], groups, tg, ntiles)
        xf = _combine(h2, routed, x1, g2, w_sh_gate[layer].astype(BF16), w_sh_up[layer].astype(BF16),
                      w_sh_down[layer].astype(BF16), seq, tg)
    return xf.reshape(batch, seq, d)
```

```python
import functools
import math

import jax
import jax.numpy as jnp
from jax import lax
from jax.experimental import pallas as pl
from jax.experimental.pallas import tpu as pltpu

F32 = jnp.float32
BF16 = jnp.bfloat16

D_MODEL = 1024
ATTN_HEADS = 8
ATTN_HEAD_DIM = 64
ATTN_V_DIM = 128
ROPE_THETA = 10000.0
SSD_D_INNER = 2048
SSD_HEAD_DIM = 64
SSD_HEADS = 32
SSD_GROUPS = 4
SSD_D_STATE = 128
SSD_CONV = 4
SSD_CHUNK = 128
SSD_CONV_DIM = 3072
N_EXPERTS = 64
TOP_K = 8
N_EXPERT_GROUPS = 8
TOPK_GROUPS = 4
EXPERT_DIM = 256
ROUTED_SCALE = 2.5
EPS = 1e-6
NEG = -1e30
LOG2E = math.log2(math.e)
Q_SCALE = ATTN_HEAD_DIM ** -0.5 * LOG2E
SSD_HALO = 16

LANES = 128
SUB = 8
MOE_ROWS = 576
MAIN_COLS = 10240
VMEM_LIMIT = 56 * 1024 * 1024


def _dot(a, b):
    return jnp.dot(a, b, preferred_element_type=F32)


def _sigmoid(x):
    return 1.0 / (1.0 + jnp.exp(-x))


def _split2(a):
    hi = a.astype(BF16)
    lo = (a - hi.astype(F32)).astype(BF16)
    return hi, lo


def _split3(a):
    hi = a.astype(BF16)
    r = a - hi.astype(F32)
    mid = r.astype(BF16)
    lo = (r - mid.astype(F32)).astype(BF16)
    return hi, mid, lo


def _params(sem, vmem=VMEM_LIMIT):
    return pltpu.CompilerParams(dimension_semantics=sem, vmem_limit_bytes=vmem)


def _ada_body(c_ref, w_ref, b_ref, o_ref):
    c = c_ref[...]
    sc = c * _sigmoid(c)
    chi, clo = _split2(sc)
    whi, wlo = _split2(w_ref[...])
    o_ref[...] = _dot(chi, whi) + _dot(chi, wlo) + _dot(clo, whi) + b_ref[...]


def _ada(c_pad, w, b):
    d, n = w.shape
    tn = 1024
    return pl.pallas_call(
        _ada_body,
        grid=(n // tn,),
        in_specs=[pl.BlockSpec((8, d), lambda j: (0, 0)),
                  pl.BlockSpec((d, tn), lambda j: (0, j)),
                  pl.BlockSpec((1, tn), lambda j: (0, j))],
        out_specs=pl.BlockSpec((8, tn), lambda j: (0, j)),
        out_shape=jax.ShapeDtypeStruct((8, n), F32),
        compiler_params=_params(("arbitrary",)),
        name="ada",
    )(c_pad, w, b.reshape(1, n))


def _inproj_body(x_ref, g_ref, sc_ref, sh_ref, w_ref, wdh_ref, wdl_ref, o_ref, dt_ref, h_scr):
    @pl.when(pl.program_id(1) == 0)
    def _():
        x = x_ref[...]
        ms = jnp.mean(x * x, axis=-1, keepdims=True)
        h = x * lax.rsqrt(ms + EPS) * g_ref[...]
        h = h * (1.0 + sc_ref[0]) + sh_ref[0]
        hb, hl = _split2(h)
        h_scr[...] = hb
        dt_ref[...] = _dot(hb, wdh_ref[...]) + _dot(hb, wdl_ref[...]) + _dot(hl, wdh_ref[...])

    o_ref[...] = _dot(h_scr[...], w_ref[...]).astype(BF16)


def _inproj(x2d, g, sc, sh, w_main, wdt_hi, wdt_lo, seq):
    t, d = x2d.shape
    tm = min(1024, seq)
    tn = 2048
    per_b = seq // tm
    return pl.pallas_call(
        _inproj_body,
        grid=(t // tm, MAIN_COLS // tn),
        in_specs=[pl.BlockSpec((tm, d), lambda i, j: (i, 0)),
                  pl.BlockSpec((1, d), lambda i, j: (0, 0)),
                  pl.BlockSpec((1, 1, d), lambda i, j: (i // per_b, 0, 0)),
                  pl.BlockSpec((1, 1, d), lambda i, j: (i // per_b, 0, 0)),
                  pl.BlockSpec((d, tn), lambda i, j: (0, j)),
                  pl.BlockSpec((d, LANES), lambda i, j: (0, 0)),
                  pl.BlockSpec((d, LANES), lambda i, j: (0, 0))],
        out_specs=[pl.BlockSpec((tm, tn), lambda i, j: (i, j)),
                   pl.BlockSpec((tm, LANES), lambda i, j: (i, 0))],
        out_shape=[jax.ShapeDtypeStruct((t, MAIN_COLS), BF16),
                   jax.ShapeDtypeStruct((t, LANES), F32)],
        scratch_shapes=[pltpu.VMEM((tm, d), BF16)],
        compiler_params=_params(("arbitrary", "arbitrary")),
        name="inproj",
    )(x2d, g.reshape(1, d), sc, sh, w_main, wdt_hi, wdt_lo)


def _qkvprep_body(q_ref, k_ref, v_ref, pos_ref, invf_ref, gq_ref, gk_ref, qt_ref, ko_ref, vt_ref, *, tk):
    ts = q_ref.shape[0]
    half = ATTN_HEAD_DIM // 2
    ang = pos_ref[0].astype(F32) * invf_ref[...]
    cos = jnp.cos(ang)
    sin = jnp.sin(ang)
    zeros = jnp.zeros((ATTN_HEAD_DIM, ts), F32)

    def norm_rope(blk, g):
        ms = jnp.mean(blk * blk, axis=0, keepdims=True)
        r = blk * lax.rsqrt(ms + EPS) * g
        x1 = r[:half]
        x2 = r[half:]
        return jnp.concatenate([x1 * cos - x2 * sin, x2 * cos + x1 * sin], axis=0)

    for h in range(ATTN_HEADS):
        sl = slice(h * LANES, (h + 1) * LANES)
        qht = q_ref[:, sl].astype(F32).T
        q1 = norm_rope(qht[:ATTN_HEAD_DIM], gq_ref[...]) * Q_SCALE
        q2 = norm_rope(qht[ATTN_HEAD_DIM:], gq_ref[...]) * Q_SCALE
        qt_ref[0, 2 * h] = jnp.concatenate([q1, zeros], axis=0).astype(BF16)
        qt_ref[0, 2 * h + 1] = jnp.concatenate([zeros, q2], axis=0).astype(BF16)
        kht = k_ref[:, sl].astype(F32).T
        k1 = norm_rope(kht[:ATTN_HEAD_DIM], gk_ref[...])
        k2 = norm_rope(kht[ATTN_HEAD_DIM:], gk_ref[...])
        ko_ref[0, h] = jnp.concatenate([k1, k2], axis=0).T.astype(BF16)
        vht = v_ref[:, sl].astype(F32).T.astype(BF16)
        for cc in range(ts // tk):
            vt_ref[0, h, cc] = vht[:, cc * tk:(cc + 1) * tk]


def _qkvprep(proj, positions, q_norm_g, k_norm_g, batch, seq, tk):
    ts = min(512, seq)
    ns = seq // ts
    half = ATTN_HEAD_DIM // 2
    inv_freq = ROPE_THETA ** (-jnp.arange(half, dtype=F32) * 2.0 / ATTN_HEAD_DIM)
    invf_b = jnp.broadcast_to(inv_freq[:, None], (half, ts))
    gq_b = jnp.broadcast_to(q_norm_g.astype(F32)[:, None], (ATTN_HEAD_DIM, ts))
    gk_b = jnp.broadcast_to(k_norm_g.astype(F32)[:, None], (ATTN_HEAD_DIM, ts))
    pos3 = positions.reshape(batch, 1, seq)
    const = lambda b, i: (0, 0)
    return pl.pallas_call(
        functools.partial(_qkvprep_body, tk=tk),
        grid=(batch, ns),
        in_specs=[pl.BlockSpec((ts, D_MODEL), lambda b, i: (b * ns + i, 0)),
                  pl.BlockSpec((ts, D_MODEL), lambda b, i: (b * ns + i, 1)),
                  pl.BlockSpec((ts, D_MODEL), lambda b, i: (b * ns + i, 2)),
                  pl.BlockSpec((1, 1, ts), lambda b, i: (b, 0, i)),
                  pl.BlockSpec((half, ts), const),
                  pl.BlockSpec((ATTN_HEAD_DIM, ts), const),
                  pl.BlockSpec((ATTN_HEAD_DIM, ts), const)],
        out_specs=[pl.BlockSpec((1, 2 * ATTN_HEADS, LANES, ts), lambda b, i: (b, 0, 0, i)),
                   pl.BlockSpec((1, ATTN_HEADS, ts, LANES), lambda b, i: (b, 0, i, 0)),
                   pl.BlockSpec((1, ATTN_HEADS, ts // tk, LANES, tk), lambda b, i: (b, 0, i, 0, 0))],
        out_shape=[jax.ShapeDtypeStruct((batch, 2 * ATTN_HEADS, LANES, seq), BF16),
                   jax.ShapeDtypeStruct((batch, ATTN_HEADS, seq, LANES), BF16),
                   jax.ShapeDtypeStruct((batch, ATTN_HEADS, seq // tk, LANES, tk), BF16)],
        compiler_params=_params(("arbitrary", "arbitrary")),
        name="qkvprep",
    )(proj, proj, proj, pos3, invf_b, gq_b, gk_b)


def _attn_body(lam_ref, sg_ref, qt_ref, k_ref, vt_ref, o_ref, acc_scr, sa_scr, sb_scr, *, tq, tk, lambda_init):
    i = pl.program_id(2)
    q_maps = (qt_ref[0, 0], qt_ref[0, 1])
    acc_scr[...] = jnp.zeros(acc_scr.shape, F32)
    tri = lax.broadcasted_iota(jnp.int32, (tk, tk), 0) <= lax.broadcasted_iota(jnp.int32, (tk, tk), 1)

    def qk(j, s_scr, lo=0):
        kb = k_ref[0, 0, pl.ds(pl.multiple_of(j * tk, tk), tk), :]
        for mi in range(2):
            s_scr[mi, :, lo:] = _dot(kb, q_maps[mi][:, lo:])

    def process(j, s_scr, carry, diag=None, lo=0):
        vb = vt_ref[0, 0, j]
        w = tq - lo
        out = []
        for mi in range(2):
            m_old, l_old = carry[2 * mi][:, lo:], carry[2 * mi + 1][:, lo:]
            if diag is None:
                s = s_scr[mi]
            elif diag == 0:
                s = jnp.concatenate([jnp.where(tri, s_scr[mi, :, :tk], NEG), s_scr[mi, :, tk:]], axis=1)
            else:
                s = jnp.where(tri, s_scr[mi, :, lo:], NEG)
            m_new = jnp.maximum(m_old, jnp.max(s, axis=0, keepdims=True))
            alpha = jnp.exp2(m_old - m_new)
            p = jnp.exp2(s - m_new[0:1])
            l_new = alpha * l_old + jnp.sum(p, axis=0, keepdims=True)
            acc_scr[mi, :, lo:] = (jnp.broadcast_to(alpha[0:1], (ATTN_V_DIM, w)) * acc_scr[mi, :, lo:]
                                   + _dot(vb, p.astype(BF16)))
            if lo:
                m_new = jnp.concatenate([carry[2 * mi][:, :lo], m_new], axis=1)
                l_new = jnp.concatenate([carry[2 * mi + 1][:, :lo], l_new], axis=1)
            out += [m_new, l_new]
        return tuple(out)

    def pair(p, carry):
        qk(2 * p + 1, sb_scr)
        carry = process(2 * p, sa_scr, carry)
        qk(2 * p + 2, sa_scr)
        return process(2 * p + 1, sb_scr, carry)

    init = (jnp.full((SUB, tq), NEG, F32), jnp.zeros((SUB, tq), F32)) * 2
    qk(0, sa_scr)
    carry = lax.fori_loop(0, i, pair, init)
    qk(2 * i + 1, sb_scr, lo=tk)
    carry = process(2 * i, sa_scr, carry, diag=0)
    carry = process(2 * i + 1, sb_scr, carry, diag=1, lo=tk)

    lv = lam_ref[...]
    lam = (jnp.exp(jnp.sum(lv[0:1] * lv[1:2], keepdims=True))
           - jnp.exp(jnp.sum(lv[2:3] * lv[3:4], keepdims=True)) + lambda_init)
    o = acc_scr[0] / carry[1][0:1] - lam * (acc_scr[1] / carry[3][0:1])
    ms = jnp.mean(o * o, axis=0, keepdims=True)
    o = o * lax.rsqrt(ms + EPS) * sg_ref[...] * (1.0 - lambda_init)
    o_ref[...] = o.T.astype(BF16)


def _attention(qt, kk, vt, lambda_qk, subln_g, lambda_init, batch, seq, tq, tk):
    nq = seq // tq
    sg_b = jnp.broadcast_to(subln_g.astype(F32)[:, None], (ATTN_V_DIM, tq))
    return pl.pallas_call(
        functools.partial(_attn_body, tq=tq, tk=tk, lambda_init=lambda_init),
        grid=(batch, ATTN_HEADS, nq),
        in_specs=[pl.BlockSpec((4, ATTN_HEAD_DIM), lambda b, h, i: (0, 0)),
                  pl.BlockSpec((ATTN_V_DIM, tq), lambda b, h, i: (0, 0)),
                  pl.BlockSpec((1, 2, LANES, tq), lambda b, h, i: (b, h, 0, i)),
                  pl.BlockSpec((1, 1, seq, LANES), lambda b, h, i: (b, h, 0, 0)),
                  pl.BlockSpec((1, 1, seq // tk, LANES, tk), lambda b, h, i: (b, h, 0, 0, 0))],
        out_specs=pl.BlockSpec((tq, LANES), lambda b, h, i: (b * nq + i, h)),
        out_shape=jax.ShapeDtypeStruct((batch * seq, ATTN_HEADS * ATTN_V_DIM), BF16),
        scratch_shapes=[pltpu.VMEM((2, ATTN_V_DIM, tq), F32),
                        pltpu.VMEM((2, tk, tq), F32),
                        pltpu.VMEM((2, tk, tq), F32)],
        compiler_params=_params(("arbitrary", "arbitrary", "arbitrary")),
        name="attn",
    )(lambda_qk.astype(F32), sg_b, qt, kk, vt)


def _ssd_body(z0_ref, z1_ref, x0_ref, x1_ref, bc_ref, dt_ref, cw_ref, cb_ref, dtb_ref, alog_ref,
              dsk_ref, ng_ref, tri_ref, shift_ref, y_ref, ext_scr, st_scr):
    L = SSD_CHUNK
    halo = SSD_HALO
    c = pl.program_id(1)

    @pl.when(c == 0)
    def _():
        ext_scr[0:halo, :] = jnp.zeros((halo, SSD_CONV_DIM), BF16)
        st_scr[...] = jnp.zeros(st_scr.shape, F32)

    ext_scr[halo:halo + L, 0:1024] = x0_ref[...]
    ext_scr[halo:halo + L, 1024:2048] = x1_ref[...]
    ext_scr[halo:halo + L, 2048:3072] = bc_ref[...]
    sh = _dot(shift_ref[...], ext_scr[...])
    acc = cb_ref[...] + cw_ref[SSD_CONV - 1:SSD_CONV, :] * ext_scr[halo:halo + L, :].astype(F32)
    for k in range(SSD_CONV - 1):
        acc = acc + cw_ref[k:k + 1, :] * sh[k * L:(k + 1) * L]
    u = acc * _sigmoid(acc)
    ext_scr[0:halo, :] = ext_scr[L:L + halo, :]

    xdt_in = dt_ref[...] + dtb_ref[...]
    dtp = jnp.maximum(xdt_in, 0.0) + jnp.log(1.0 + jnp.exp(-jnp.abs(xdt_in)))
    a = dtp * (-jnp.exp(alog_ref[...]) * LOG2E)
    tri = tri_ref[...]
    a_hi, a_mid, a_lo = _split3(a)
    acum = _dot(tri, a_hi) + _dot(tri, a_mid) + _dot(tri, a_lo)
    acum_t = acum.T
    lane = lax.broadcasted_iota(jnp.int32, (L, LANES), 1)
    first = lane < SSD_HEAD_DIM
    tril = lax.broadcasted_iota(jnp.int32, (L, L), 0) >= lax.broadcasted_iota(jnp.int32, (L, L), 1)
    gn = SSD_GROUPS * SSD_D_STATE
    gw = SSD_D_INNER // SSD_GROUPS

    def col(mat, h):
        return jnp.broadcast_to(mat[:, h:h + 1], (L, LANES))

    for g in range(SSD_GROUPS):
        bg = u[:, SSD_D_INNER + g * SSD_D_STATE:SSD_D_INNER + (g + 1) * SSD_D_STATE]
        cg = u[:, SSD_D_INNER + gn + g * SSD_D_STATE:SSD_D_INNER + gn + (g + 1) * SSD_D_STATE].astype(BF16)
        bgt = bg.T.astype(BF16)
        cb = _dot(cg, bgt)
        sprev = st_scr[g]
        yoff = _dot(cg, sprev.astype(BF16))
        y_parts, xdd_parts, dec_parts = [], [], []
        for p in range(4):
            h0 = 8 * g + 2 * p
            c0, c1 = col(acum, h0), col(acum, h0 + 1)
            ac_exp = jnp.where(first, c0, c1)
            dt_exp = jnp.where(first, col(dtp, h0), col(dtp, h0 + 1))
            xsl = u[:, h0 * SSD_HEAD_DIM:h0 * SSD_HEAD_DIM + LANES]
            xdt = xsl * dt_exp
            l0 = jnp.exp2(jnp.where(tril, c0 - acum_t[h0:h0 + 1, :], NEG))
            l1 = jnp.exp2(jnp.where(tril, c1 - acum_t[h0 + 1:h0 + 2, :], NEG))
            m2 = jnp.concatenate([(cb * l0).astype(BF16), (cb * l1).astype(BF16)], axis=1)
            xb = xdt.astype(BF16)
            zb = jnp.zeros_like(xb)
            rhs = jnp.concatenate([jnp.where(first, xb, zb), jnp.where(first, zb, xb)], axis=0)
            yd = _dot(m2, rhs)
            yo = yoff[:, p * LANES:(p + 1) * LANES] * jnp.exp2(ac_exp)
            dsk = dsk_ref[:, h0 * SSD_HEAD_DIM:h0 * SSD_HEAD_DIM + LANES]
            y_parts.append(yd + yo + xsl * dsk)
            last = ac_exp[L - 1:L, :]
            xdd_parts.append((xdt * jnp.exp2(last - ac_exp)).astype(BF16))
            dec_parts.append(jnp.exp2(last))
        xdd = jnp.concatenate(xdd_parts, axis=1)
        dec = jnp.concatenate(dec_parts, axis=1)
        st_scr[g] = sprev * dec + _dot(bgt, xdd)
        yg = jnp.concatenate(y_parts, axis=1)
        zref = z0_ref if g < 2 else z1_ref
        zg = zref[:, (g % 2) * gw:(g % 2 + 1) * gw].astype(F32)
        yg = yg * (zg * _sigmoid(zg))
        ms = jnp.mean(yg * yg, axis=-1, keepdims=True)
        yg = yg * lax.rsqrt(ms + EPS) * ng_ref[:, g * gw:(g + 1) * gw]
        y_ref[:, g * gw:(g + 1) * gw] = yg.astype(BF16)


def _ssd(proj, dt, conv_w, conv_b, dt_bias, a_log, d_skip, norm_g, batch, seq):
    L = SSD_CHUNK
    nc = seq // L
    pad = LANES - SSD_HEADS
    dtb = jnp.pad(dt_bias.astype(F32), (0, pad)).reshape(1, LANES)
    alog = jnp.pad(a_log.astype(F32), (0, pad)).reshape(1, LANES)
    dsk = jnp.repeat(d_skip.astype(F32), SSD_HEAD_DIM).reshape(1, SSD_D_INNER)
    tri = jnp.tril(jnp.ones((L, L), F32)).astype(BF16)
    tt = jnp.arange(L)
    shift = jnp.concatenate(
        [jax.nn.one_hot(SSD_HALO + tt - (SSD_CONV - 1) + k, L + SSD_HALO, dtype=BF16) for k in range(SSD_CONV - 1)],
        axis=0)
    row = lambda b, c: (b * nc + c)
    const = lambda b, c: (0, 0)
    return pl.pallas_call(
        _ssd_body,
        grid=(batch, nc),
        in_specs=[pl.BlockSpec((L, 1024), lambda b, c: (row(b, c), 5)),
                  pl.BlockSpec((L, 1024), lambda b, c: (row(b, c), 6)),
                  pl.BlockSpec((L, 1024), lambda b, c: (row(b, c), 7)),
                  pl.BlockSpec((L, 1024), lambda b, c: (row(b, c), 8)),
                  pl.BlockSpec((L, 1024), lambda b, c: (row(b, c), 9)),
                  pl.BlockSpec((L, LANES), lambda b, c: (row(b, c), 0)),
                  pl.BlockSpec((SSD_CONV, SSD_CONV_DIM), const),
                  pl.BlockSpec((1, SSD_CONV_DIM), const),
                  pl.BlockSpec((1, LANES), const),
                  pl.BlockSpec((1, LANES), const),
                  pl.BlockSpec((1, SSD_D_INNER), const),
                  pl.BlockSpec((1, SSD_D_INNER), const),
                  pl.BlockSpec((L, L), const),
                  pl.BlockSpec(((SSD_CONV - 1) * L, L + SSD_HALO), const)],
        out_specs=pl.BlockSpec((L, SSD_D_INNER), lambda b, c: (row(b, c), 0)),
        out_shape=jax.ShapeDtypeStruct((batch * seq, SSD_D_INNER), BF16),
        scratch_shapes=[pltpu.VMEM((L + SSD_HALO, SSD_CONV_DIM), BF16),
                        pltpu.VMEM((SSD_GROUPS, SSD_D_STATE, SSD_D_INNER // SSD_GROUPS), F32)],
        compiler_params=_params(("arbitrary", "arbitrary")),
        name="ssd",
    )(proj, proj, proj, proj, proj, dt, conv_w.astype(F32), conv_b.astype(F32).reshape(1, -1), dtb, alog,
      dsk, norm_g.astype(F32).reshape(1, -1), tri, shift)


def _outproj_body(ya_ref, yb_ref, ga_ref, gb_ref, x_ref, g1_ref, sc_ref, sh_ref, n2_ref,
                  wa_ref, wb_ref, wo_ref, wr_ref, x1_ref, h2_ref, lg_ref):
    pa = _dot(ya_ref[...], wa_ref[...])
    pb = _dot(yb_ref[...], wb_ref[...])
    merged = _sigmoid(ga_ref[...].astype(F32)) * pa + _sigmoid(gb_ref[...].astype(F32)) * pb
    x1 = x_ref[...] + g1_ref[0] * _dot(merged.astype(BF16), wo_ref[...])
    x1_ref[...] = x1
    ms = jnp.mean(x1 * x1, axis=-1, keepdims=True)
    h2 = x1 * lax.rsqrt(ms + EPS) * n2_ref[...]
    h2 = h2 * (1.0 + sc_ref[0]) + sh_ref[0]
    tm = h2.shape[0]
    for s in range(SUB):
        h2_ref[pl.ds(s, tm, stride=SUB), :] = h2[:, s * LANES:(s + 1) * LANES]
    lg_ref[...] = _dot(h2.astype(BF16), wr_ref[...])


def _outproj(ya, yb, proj, x2d, g1, sc2, sh2, norm2_g, wa, wb, wo, wr, seq):
    t, d = x2d.shape
    tm = min(512, seq)
    per_b = seq // tm
    const = lambda i: (0, 0)
    mod = lambda i: (i // per_b, 0, 0)
    return pl.pallas_call(
        _outproj_body,
        grid=(t // tm,),
        in_specs=[pl.BlockSpec((tm, d), lambda i: (i, 0)),
                  pl.BlockSpec((tm, SSD_D_INNER), lambda i: (i, 0)),
                  pl.BlockSpec((tm, d), lambda i: (i, 3)),
                  pl.BlockSpec((tm, d), lambda i: (i, 4)),
                  pl.BlockSpec((tm, d), lambda i: (i, 0)),
                  pl.BlockSpec((1, 1, d), mod),
                  pl.BlockSpec((1, 1, d), mod),
                  pl.BlockSpec((1, 1, d), mod),
                  pl.BlockSpec((1, d), const),
                  pl.BlockSpec((d, d), const),
                  pl.BlockSpec((SSD_D_INNER, d), const),
                  pl.BlockSpec((d, d), const),
                  pl.BlockSpec((d, LANES), const)],
        out_specs=[pl.BlockSpec((tm, d), lambda i: (i, 0)),
                   pl.BlockSpec((tm * SUB, LANES), lambda i: (i, 0)),
                   pl.BlockSpec((tm, LANES), lambda i: (i, 0))],
        out_shape=[jax.ShapeDtypeStruct((t, d), F32),
                   jax.ShapeDtypeStruct((t * SUB, LANES), F32),
                   jax.ShapeDtypeStruct((t, LANES), F32)],
        compiler_params=_params(("arbitrary",)),
        name="outproj",
    )(ya, yb, proj, proj, x2d, g1, sc2, sh2, norm2_g.astype(F32).reshape(1, d), wa, wb, wo, wr)


def _router_body(lg_ref, bias_ref, idx_ref, w_ref):
    tr = lg_ref.shape[0]
    per = N_EXPERTS // N_EXPERT_GROUPS
    lt = lg_ref[...].T
    scores = _sigmoid(lt[:N_EXPERTS])
    choice = scores + bias_ref[...]
    iota = lax.broadcasted_iota(jnp.int32, (per, tr), 0)
    grp = [choice[g * per:(g + 1) * per] for g in range(N_EXPERT_GROUPS)]
    sc_g = [scores[g * per:(g + 1) * per] for g in range(N_EXPERT_GROUPS)]

    gsc = jnp.zeros((N_EXPERT_GROUPS, tr), F32)
    for g in range(N_EXPERT_GROUPS):
        top1 = jnp.max(grp[g], axis=0, keepdims=True)
        idx1 = jnp.min(jnp.where(grp[g] == top1, iota, per), axis=0, keepdims=True)
        top2 = jnp.max(jnp.where(iota == idx1, NEG, grp[g]), axis=0, keepdims=True)
        gsc = jnp.where(iota == g, top1 + top2, gsc)

    gsel = jnp.zeros((N_EXPERT_GROUPS, tr), F32)
    cur = gsc
    for _ in range(TOPK_GROUPS):
        mx = jnp.max(cur, axis=0, keepdims=True)
        idx = jnp.min(jnp.where(cur == mx, iota, N_EXPERT_GROUPS), axis=0, keepdims=True)
        hit = iota == idx
        gsel = jnp.where(hit, 1.0, gsel)
        cur = jnp.where(hit, NEG, cur)

    cur_g = [jnp.where(gsel[g:g + 1] > 0.0, grp[g], NEG) for g in range(N_EXPERT_GROUPS)]
    idx_rows = jnp.zeros((TOP_K, tr), jnp.int32)
    w_rows = jnp.zeros((TOP_K, tr), F32)
    for kk in range(TOP_K):
        mx = cur_g[0]
        for g in range(1, N_EXPERT_GROUPS):
            mx = jnp.maximum(mx, cur_g[g])
        mx = jnp.max(mx, axis=0, keepdims=True)
        idx = jnp.where(cur_g[0] == mx, iota, N_EXPERTS)
        for g in range(1, N_EXPERT_GROUPS):
            idx = jnp.minimum(idx, jnp.where(cur_g[g] == mx, iota + g * per, N_EXPERTS))
        idx = jnp.min(idx, axis=0, keepdims=True)
        wk = jnp.zeros((per, tr), F32)
        for g in range(N_EXPERT_GROUPS):
            hit = (iota + g * per) == idx
            wk = wk + jnp.where(hit, sc_g[g], 0.0)
            cur_g[g] = jnp.where(hit, NEG, cur_g[g])
        idx_rows = jnp.where(iota == kk, idx, idx_rows)
        w_rows = jnp.where(iota == kk, jnp.sum(wk, axis=0, keepdims=True), w_rows)

    wsum = jnp.sum(w_rows, axis=0, keepdims=True)
    idx_ref[...] = idx_rows
    w_ref[...] = w_rows / wsum * ROUTED_SCALE


def _router(logits, router_bias, seq):
    t = logits.shape[0]
    tr = min(1024, seq)
    bias_b = jnp.broadcast_to(router_bias.astype(F32)[:, None], (N_EXPERTS, tr))
    return pl.pallas_call(
        _router_body,
        grid=(t // tr,),
        in_specs=[pl.BlockSpec((tr, LANES), lambda i: (i, 0)),
                  pl.BlockSpec((N_EXPERTS, tr), lambda i: (0, 0))],
        out_specs=[pl.BlockSpec((TOP_K, tr), lambda i: (0, i)),
                   pl.BlockSpec((TOP_K, tr), lambda i: (0, i))],
        out_shape=[jax.ShapeDtypeStruct((TOP_K, t), jnp.int32),
                   jax.ShapeDtypeStruct((TOP_K, t), F32)],
        compiler_params=_params(("arbitrary",)),
        name="router",
    )(logits, bias_b)


def _route_tables(idx_t, w_t, groups, tg, rows, ntiles):
    i32 = jnp.int32
    e = idx_t.reshape(TOP_K, groups, tg).transpose(1, 0, 2).reshape(groups, TOP_K * tg)
    w = w_t.reshape(TOP_K, groups, tg).transpose(1, 0, 2).reshape(groups, TOP_K * tg)
    tl = jnp.broadcast_to(jnp.arange(tg, dtype=i32)[None, None, :], (groups, TOP_K, tg)).reshape(groups, TOP_K * tg)
    skey, sw = lax.sort((e * tg + tl, w), dimension=1, num_keys=1)
    stok = skey % tg
    cnt = jnp.sum((e[:, :, None] == jnp.arange(N_EXPERTS, dtype=i32)[None, None, :]).astype(i32), axis=1)
    off = jnp.cumsum(cnt, axis=1) - cnt
    ntile = (cnt + rows - 1) // rows
    tend = jnp.cumsum(ntile, axis=1)
    tstart = tend - ntile
    n = jnp.arange(ntiles, dtype=i32)
    te = jnp.sum((n[None, :, None] >= tend[:, None, :]).astype(i32), axis=2)
    valid_tile = n[None, :] < tend[:, -1:]
    te = jnp.minimum(te, N_EXPERTS - 1)
    k0 = (n[None, :] - jnp.take_along_axis(tstart, te, axis=1)) * rows
    nvalid = jnp.where(valid_tile, jnp.clip(jnp.take_along_axis(cnt, te, axis=1) - k0, 0, rows), 0)
    start = jnp.clip(jnp.take_along_axis(off, te, axis=1) + k0, 0, TOP_K * tg - 1)
    tok_g, tok_s, wv = _tile_windows(start, nvalid, stok, sw, groups, tg, rows, ntiles)
    return te, nvalid, tok_g, tok_s, wv


def _windows_body(start_ref, nv_ref, stok_ref, sw_ref, tokg_ref, toks_ref, wt_ref, *, rows, tg):
    g = pl.program_id(0)
    nt = tokg_ref.shape[1]
    nl = -(-rows // LANES)
    lane = lax.broadcasted_iota(jnp.int32, (1, LANES), 1)
    rr = lax.broadcasted_iota(jnp.int32, (1, nl * LANES), 1)

    def window(src, q, sh):
        blk = src[0, pl.ds(q, SUB), :]
        rolled = pltpu.roll(blk, (LANES - sh) % LANES, 1)
        parts = [jnp.where(lane < LANES - sh, rolled[k:k + 1], rolled[k + 1:k + 2]) for k in range(nl)]
        return jnp.concatenate(parts, axis=1)

    def body(n, carry):
        s = start_ref[g, n]
        q = s // LANES
        sh = s % LANES
        valid = rr < nv_ref[g, n]
        tok8 = window(stok_ref, q, sh) * SUB
        wv = window(sw_ref, q, sh)
        tokg_ref[0, n] = jnp.where(valid, tok8, 0)[:, :rows]
        toks_ref[0, n] = jnp.where(valid, tok8, tg * SUB)[:, :rows]
        wt_ref[0, n] = jnp.where(valid, wv, 0.0)[:, :rows]
        return carry

    lax.fori_loop(0, nt, body, 0, unroll=4)


def _tile_windows(start, nvalid, stok, sw, groups, tg, rows, ntiles):
    n_list = TOP_K * tg
    n_rows = n_list // LANES + SUB
    pad = n_rows * LANES - n_list
    stok2 = jnp.pad(stok, ((0, 0), (0, pad))).reshape(groups, n_rows, LANES)
    sw2 = jnp.pad(sw, ((0, 0), (0, pad))).reshape(groups, n_rows, LANES)
    lst = pl.BlockSpec((1, n_rows, LANES), lambda g, st, nv: (g, 0, 0))
    out = pl.BlockSpec((1, ntiles, 1, rows), lambda g, st, nv: (g, 0, 0, 0))
    grid_spec = pltpu.PrefetchScalarGridSpec(
        num_scalar_prefetch=2, grid=(groups,), in_specs=[lst, lst], out_specs=[out, out, out])
    shape = (groups, ntiles, 1, rows)
    return pl.pallas_call(
        functools.partial(_windows_body, rows=rows, tg=tg),
        grid_spec=grid_spec,
        out_shape=[jax.ShapeDtypeStruct(shape, jnp.int32), jax.ShapeDtypeStruct(shape, jnp.int32),
                   jax.ShapeDtypeStruct(shape, F32)],
        compiler_params=_params(("arbitrary",)),
        name="windows",
    )(start, nvalid, stok2, sw2)


def _routed_body(te_ref, nv_ref, tokg_ref, toks_ref, wt_ref, h_ref, wg_ref, wu_ref, wd_ref, acc_ref,
                 xbuf, ybuf):
    g = pl.program_id(0)
    n = pl.program_id(1)
    rows = MOE_ROWS
    slot = n % 2
    has_cur = nv_ref[g, n + 1] > 0
    has_prev = nv_ref[g, n] > 0

    @pl.when(n == 0)
    def _():
        acc_ref[...] = jnp.zeros(acc_ref.shape, F32)

    def compute():
        for r in range(rows):
            t8 = pl.multiple_of(tokg_ref[0, 0, 0, r], SUB)
            xbuf[pl.ds(r * SUB, SUB), :] = h_ref[pl.ds(t8, SUB), :]
        x2d = jnp.concatenate([xbuf[pl.ds(s, rows, stride=SUB), :] for s in range(SUB)], axis=1).astype(BF16)
        a = _dot(x2d, wg_ref[0].astype(BF16))
        b = _dot(x2d, wu_ref[0].astype(BF16))
        act = (a * _sigmoid(a) * b).astype(BF16)
        y = _dot(act, wd_ref[0].astype(BF16))
        for s in range(SUB):
            ybuf[slot, pl.ds(s, rows, stride=SUB), :] = y[:, s * LANES:(s + 1) * LANES]

    def scatter_prev():
        for c in range(rows // SUB):
            upd = []
            for r in range(c * SUB, (c + 1) * SUB):
                t8 = pl.multiple_of(toks_ref[0, 0, 0, r], SUB)
                upd.append((t8, acc_ref[0, pl.ds(t8, SUB), :]
                            + wt_ref[0, 0, 0, r] * ybuf[1 - slot, pl.ds(r * SUB, SUB), :]))
            for t8, v in upd:
                acc_ref[0, pl.ds(t8, SUB), :] = v

    @pl.when(jnp.logical_and(has_cur, has_prev))
    def _():
        scatter_prev()
        compute()

    @pl.when(jnp.logical_and(has_cur, jnp.logical_not(has_prev)))
    def _():
        compute()

    @pl.when(jnp.logical_and(jnp.logical_not(has_cur), has_prev))
    def _():
        scatter_prev()


def _routed(hv, tables, w_eg, w_eu, w_ed, groups, tg, ntiles):
    te, nvalid, tok_g, tok_s, wv = tables
    d, f = D_MODEL, EXPERT_DIM
    rows = MOE_ROWS
    i32 = jnp.int32
    te = jnp.concatenate([te, te[:, -1:]], axis=1)
    zero = jnp.zeros((groups, 1), i32)
    nvp = jnp.concatenate([zero, nvalid, zero], axis=1)
    cur = lambda g, n, te, nv: (g, jnp.minimum(n, ntiles - 1), 0, 0)
    prev = lambda g, n, te, nv: (g, jnp.maximum(n - 1, 0), 0, 0)
    smem = functools.partial(pl.BlockSpec, (1, 1, 1, rows), memory_space=pltpu.SMEM)
    grid_spec = pltpu.PrefetchScalarGridSpec(
        num_scalar_prefetch=2,
        grid=(groups, ntiles + 1),
        in_specs=[smem(index_map=cur), smem(index_map=prev), smem(index_map=prev),
                  pl.BlockSpec((tg * SUB, LANES), lambda g, n, te, nv: (g, 0), pipeline_mode=pl.Buffered(1)),
                  pl.BlockSpec((1, d, f), lambda g, n, te, nv: (te[g, n], 0, 0)),
                  pl.BlockSpec((1, d, f), lambda g, n, te, nv: (te[g, n], 0, 0)),
                  pl.BlockSpec((1, f, d), lambda g, n, te, nv: (te[g, n], 0, 0))],
        out_specs=pl.BlockSpec((1, (tg + 1) * SUB, LANES), lambda g, n, te, nv: (g, 0, 0),
                               pipeline_mode=pl.Buffered(1)),
        scratch_shapes=[pltpu.VMEM((rows * SUB, LANES), F32), pltpu.VMEM((2, rows * SUB, LANES), F32)],
    )
    return pl.pallas_call(
        _routed_body,
        grid_spec=grid_spec,
        out_shape=jax.ShapeDtypeStruct((groups, (tg + 1) * SUB, LANES), F32),
        compiler_params=_params(("arbitrary", "arbitrary")),
        name="routed",
    )(te, nvp, tok_g, tok_s, wv, hv, w_eg, w_eu, w_ed)


def _combine_body(h_ref, r_ref, x1_ref, g2_ref, sg_ref, su_ref, sd_ref, o_ref):
    tm = x1_ref.shape[0]
    h = jnp.concatenate([h_ref[pl.ds(s, tm, stride=SUB), :] for s in range(SUB)], axis=1)
    routed = jnp.concatenate([r_ref[0, pl.ds(s, tm, stride=SUB), :] for s in range(SUB)], axis=1)
    hb = h.astype(BF16)
    a = _dot(hb, sg_ref[...])
    b = _dot(hb, su_ref[...])
    shared = _dot((a * _sigmoid(a) * b).astype(BF16), sd_ref[...])
    o_ref[...] = x1_ref[...] + g2_ref[0] * (routed + shared)


def _combine(hv, routed, x1, g2, w_sg, w_su, w_sd, seq, tg):
    t, d = x1.shape
    tm = min(1024, seq, tg)
    per_b = seq // tm
    per_g = tg // tm
    f = w_sg.shape[1]
    const = lambda i: (0, 0)
    row = lambda i: (i, 0)
    return pl.pallas_call(
        _combine_body,
        grid=(t // tm,),
        in_specs=[pl.BlockSpec((tm * SUB, LANES), row),
                  pl.BlockSpec((1, tm * SUB, LANES), lambda i: (i // per_g, i % per_g, 0)),
                  pl.BlockSpec((tm, d), row),
                  pl.BlockSpec((1, 1, d), lambda i: (i // per_b, 0, 0)),
                  pl.BlockSpec((d, f), const),
                  pl.BlockSpec((d, f), const),
                  pl.BlockSpec((f, d), const)],
        out_specs=pl.BlockSpec((tm, d), row),
        out_shape=jax.ShapeDtypeStruct((t, d), F32),
        compiler_params=_params(("arbitrary",)),
        name="combine",
    )(hv, routed, x1, g2, w_sg, w_su, w_sd)


def kernel(x, c, positions, w_ada, b_ada, norm1_g, w_in, q_norm_g, k_norm_g, lambda_qk, subln_g, conv_w, conv_b, dt_bias, a_log, d_skip, ssd_norm_g, w_branch_a, w_branch_b, w_out, norm2_g, w_router, router_bias, w_exp_gate, w_exp_up, w_exp_down, w_sh_gate, w_sh_up, w_sh_down):
    batch, seq, d = x.shape
    t = batch * seq
    tq = min(1024, seq)
    tk = tq // 2
    xf = x.reshape(t, d)
    for layer in range(w_ada.shape[0]):
        lambda_init = 0.8 - 0.6 * math.exp(-0.3 * layer)
        c_pad = jnp.pad(c.astype(F32), ((0, 8 - batch), (0, 0)))
        mod = _ada(c_pad, w_ada[layer], b_ada[layer])[:batch].reshape(batch, 6, 1, d)
        sh1, sc1, g1, sh2, sc2, g2 = (mod[:, n] for n in range(6))

        w_l = w_in[layer]
        wdt = jnp.pad(w_l[:, MAIN_COLS:], ((0, 0), (0, LANES - SSD_HEADS)))
        wdt_hi, wdt_lo = _split2(wdt)
        proj, dt = _inproj(xf, norm1_g[layer], sc1, sh1, w_l[:, :MAIN_COLS].astype(BF16), wdt_hi, wdt_lo, seq)

        qt, kk, vt = _qkvprep(proj, positions, q_norm_g[layer], k_norm_g[layer], batch, seq, tk)
        ya = _attention(qt, kk, vt, lambda_qk[layer], subln_g[layer], lambda_init, batch, seq, tq, tk)
        yb = _ssd(proj, dt, conv_w[layer], conv_b[layer], dt_bias[layer], a_log[layer], d_skip[layer],
                  ssd_norm_g[layer], batch, seq)

        wr = jnp.pad(w_router[layer], ((0, 0), (0, LANES - N_EXPERTS))).astype(BF16)
        x1, h2, logits = _outproj(ya, yb, proj, xf, g1, sc2, sh2, norm2_g[layer],
                                  w_branch_a[layer].astype(BF16), w_branch_b[layer].astype(BF16),
                                  w_out[layer].astype(BF16), wr, seq)
        idx_t, w_t = _router(logits, router_bias[layer], seq)
        tg = min(4096, t)
        groups = t // tg
        ntiles = -(-tg * TOP_K // MOE_ROWS) + N_EXPERTS
        tables = _route_tables(idx_t, w_t, groups, tg, MOE_ROWS, ntiles)
        routed = _routed(h2, tables, w_exp_gate[layer], w_exp_up[layer], w_exp_down[layer], groups, tg, ntiles)
        xf = _combine(h2, routed, x1, g2, w_sh_gate[layer].astype(BF16), w_sh_up[layer].astype(BF16),
                      w_sh_down[layer].astype(BF16), seq, tg)
    return xf.reshape(batch, seq, d)
```

```python
import functools
import math

import jax
import jax.numpy as jnp
from jax import lax
from jax.experimental import pallas as pl
from jax.experimental.pallas import tpu as pltpu

F32 = jnp.float32
BF16 = jnp.bfloat16

D_MODEL = 1024
ATTN_HEADS = 8
ATTN_HEAD_DIM = 64
ATTN_V_DIM = 128
ROPE_THETA = 10000.0
SSD_D_INNER = 2048
SSD_HEAD_DIM = 64
SSD_HEADS = 32
SSD_GROUPS = 4
SSD_D_STATE = 128
SSD_CONV = 4
SSD_CHUNK = 128
SSD_CONV_DIM = 3072
N_EXPERTS = 64
TOP_K = 8
N_EXPERT_GROUPS = 8
TOPK_GROUPS = 4
EXPERT_DIM = 256
ROUTED_SCALE = 2.5
EPS = 1e-6
NEG = -1e30
LOG2E = math.log2(math.e)
Q_SCALE = ATTN_HEAD_DIM ** -0.5 * LOG2E
SSD_HALO = 16

LANES = 128
SUB = 8
MOE_ROWS = 576
MAIN_COLS = 10240
VMEM_LIMIT = 56 * 1024 * 1024


def _dot(a, b):
    return jnp.dot(a, b, preferred_element_type=F32)


def _sigmoid(x):
    return 1.0 / (1.0 + jnp.exp(-x))


def _split2(a):
    hi = a.astype(BF16)
    lo = (a - hi.astype(F32)).astype(BF16)
    return hi, lo


def _split3(a):
    hi = a.astype(BF16)
    r = a - hi.astype(F32)
    mid = r.astype(BF16)
    lo = (r - mid.astype(F32)).astype(BF16)
    return hi, mid, lo


def _params(sem, vmem=VMEM_LIMIT):
    return pltpu.CompilerParams(dimension_semantics=sem, vmem_limit_bytes=vmem)


def _ada_body(c_ref, w_ref, b_ref, o_ref):
    c = c_ref[...]
    sc = c * _sigmoid(c)
    chi, clo = _split2(sc)
    whi, wlo = _split2(w_ref[...])
    o_ref[...] = _dot(chi, whi) + _dot(chi, wlo) + _dot(clo, whi) + b_ref[...]


def _ada(c_pad, w, b):
    d, n = w.shape
    tn = 1024
    return pl.pallas_call(
        _ada_body,
        grid=(n // tn,),
        in_specs=[pl.BlockSpec((8, d), lambda j: (0, 0)),
                  pl.BlockSpec((d, tn), lambda j: (0, j)),
                  pl.BlockSpec((1, tn), lambda j: (0, j))],
        out_specs=pl.BlockSpec((8, tn), lambda j: (0, j)),
        out_shape=jax.ShapeDtypeStruct((8, n), F32),
        compiler_params=_params(("arbitrary",)),
        name="ada",
    )(c_pad, w, b.reshape(1, n))


def _inproj_body(x_ref, g_ref, sc_ref, sh_ref, w_ref, wdh_ref, wdl_ref, o_ref, dt_ref, h_scr):
    @pl.when(pl.program_id(1) == 0)
    def _():
        x = x_ref[...]
        ms = jnp.mean(x * x, axis=-1, keepdims=True)
        h = x * lax.rsqrt(ms + EPS) * g_ref[...]
        h = h * (1.0 + sc_ref[0]) + sh_ref[0]
        hb, hl = _split2(h)
        h_scr[...] = hb
        dt_ref[...] = _dot(hb, wdh_ref[...]) + _dot(hb, wdl_ref[...]) + _dot(hl, wdh_ref[...])

    o_ref[...] = _dot(h_scr[...], w_ref[...]).astype(BF16)


def _inproj(x2d, g, sc, sh, w_main, wdt_hi, wdt_lo, seq):
    t, d = x2d.shape
    tm = min(1024, seq)
    tn = 2048
    per_b = seq // tm
    return pl.pallas_call(
        _inproj_body,
        grid=(t // tm, MAIN_COLS // tn),
        in_specs=[pl.BlockSpec((tm, d), lambda i, j: (i, 0)),
                  pl.BlockSpec((1, d), lambda i, j: (0, 0)),
                  pl.BlockSpec((1, 1, d), lambda i, j: (i // per_b, 0, 0)),
                  pl.BlockSpec((1, 1, d), lambda i, j: (i // per_b, 0, 0)),
                  pl.BlockSpec((d, tn), lambda i, j: (0, j)),
                  pl.BlockSpec((d, LANES), lambda i, j: (0, 0)),
                  pl.BlockSpec((d, LANES), lambda i, j: (0, 0))],
        out_specs=[pl.BlockSpec((tm, tn), lambda i, j: (i, j)),
                   pl.BlockSpec((tm, LANES), lambda i, j: (i, 0))],
        out_shape=[jax.ShapeDtypeStruct((t, MAIN_COLS), BF16),
                   jax.ShapeDtypeStruct((t, LANES), F32)],
        scratch_shapes=[pltpu.VMEM((tm, d), BF16)],
        compiler_params=_params(("arbitrary", "arbitrary")),
        name="inproj",
    )(x2d, g.reshape(1, d), sc, sh, w_main, wdt_hi, wdt_lo)


def _qkvprep_body(q_ref, k_ref, v_ref, pos_ref, invf_ref, gq_ref, gk_ref, qt_ref, ko_ref, vt_ref, *, tk):
    ts = q_ref.shape[0]
    half = ATTN_HEAD_DIM // 2
    ang = pos_ref[0].astype(F32) * invf_ref[...]
    cos = jnp.cos(ang)
    sin = jnp.sin(ang)
    zeros = jnp.zeros((ATTN_HEAD_DIM, ts), F32)

    def norm_rope(blk, g):
        ms = jnp.mean(blk * blk, axis=0, keepdims=True)
        r = blk * lax.rsqrt(ms + EPS) * g
        x1 = r[:half]
        x2 = r[half:]
        return jnp.concatenate([x1 * cos - x2 * sin, x2 * cos + x1 * sin], axis=0)

    for h in range(ATTN_HEADS):
        sl = slice(h * LANES, (h + 1) * LANES)
        qht = q_ref[:, sl].astype(F32).T
        q1 = norm_rope(qht[:ATTN_HEAD_DIM], gq_ref[...]) * Q_SCALE
        q2 = norm_rope(qht[ATTN_HEAD_DIM:], gq_ref[...]) * Q_SCALE
        qt_ref[0, 2 * h] = jnp.concatenate([q1, zeros], axis=0).astype(BF16)
        qt_ref[0, 2 * h + 1] = jnp.concatenate([zeros, q2], axis=0).astype(BF16)
        kht = k_ref[:, sl].astype(F32).T
        k1 = norm_rope(kht[:ATTN_HEAD_DIM], gk_ref[...])
        k2 = norm_rope(kht[ATTN_HEAD_DIM:], gk_ref[...])
        ko_ref[0, h] = jnp.concatenate([k1, k2], axis=0).T.astype(BF16)
        vht = v_ref[:, sl].astype(F32).T.astype(BF16)
        for cc in range(ts // tk):
            vt_ref[0, h, cc] = vht[:, cc * tk:(cc + 1) * tk]


def _qkvprep(proj, positions, q_norm_g, k_norm_g, batch, seq, tk):
    ts = min(512, seq)
    ns = seq // ts
    half = ATTN_HEAD_DIM // 2
    inv_freq = ROPE_THETA ** (-jnp.arange(half, dtype=F32) * 2.0 / ATTN_HEAD_DIM)
    invf_b = jnp.broadcast_to(inv_freq[:, None], (half, ts))
    gq_b = jnp.broadcast_to(q_norm_g.astype(F32)[:, None], (ATTN_HEAD_DIM, ts))
    gk_b = jnp.broadcast_to(k_norm_g.astype(F32)[:, None], (ATTN_HEAD_DIM, ts))
    pos3 = positions.reshape(batch, 1, seq)
    const = lambda b, i: (0, 0)
    return pl.pallas_call(
        functools.partial(_qkvprep_body, tk=tk),
        grid=(batch, ns),
        in_specs=[pl.BlockSpec((ts, D_MODEL), lambda b, i: (b * ns + i, 0)),
                  pl.BlockSpec((ts, D_MODEL), lambda b, i: (b * ns + i, 1)),
                  pl.BlockSpec((ts, D_MODEL), lambda b, i: (b * ns + i, 2)),
                  pl.BlockSpec((1, 1, ts), lambda b, i: (b, 0, i)),
                  pl.BlockSpec((half, ts), const),
                  pl.BlockSpec((ATTN_HEAD_DIM, ts), const),
                  pl.BlockSpec((ATTN_HEAD_DIM, ts), const)],
        out_specs=[pl.BlockSpec((1, 2 * ATTN_HEADS, LANES, ts), lambda b, i: (b, 0, 0, i)),
                   pl.BlockSpec((1, ATTN_HEADS, ts, LANES), lambda b, i: (b, 0, i, 0)),
                   pl.BlockSpec((1, ATTN_HEADS, ts // tk, LANES, tk), lambda b, i: (b, 0, i, 0, 0))],
        out_shape=[jax.ShapeDtypeStruct((batch, 2 * ATTN_HEADS, LANES, seq), BF16),
                   jax.ShapeDtypeStruct((batch, ATTN_HEADS, seq, LANES), BF16),
                   jax.ShapeDtypeStruct((batch, ATTN_HEADS, seq // tk, LANES, tk), BF16)],
        compiler_params=_params(("arbitrary", "arbitrary")),
        name="qkvprep",
    )(proj, proj, proj, pos3, invf_b, gq_b, gk_b)


def _attn_body(lam_ref, sg_ref, qt_ref, k_ref, vt_ref, o_ref, acc_scr, sa_scr, sb_scr, *, tq, tk, lambda_init):
    i = pl.program_id(2)
    q_maps = (qt_ref[0, 0], qt_ref[0, 1])
    acc_scr[...] = jnp.zeros(acc_scr.shape, F32)
    tri = lax.broadcasted_iota(jnp.int32, (tk, tk), 0) <= lax.broadcasted_iota(jnp.int32, (tk, tk), 1)

    def qk(j, s_scr, lo=0):
        kb = k_ref[0, 0, pl.ds(pl.multiple_of(j * tk, tk), tk), :]
        for mi in range(2):
            s_scr[mi, :, lo:] = _dot(kb, q_maps[mi][:, lo:])

    def process(j, s_scr, carry, diag=None, lo=0):
        vb = vt_ref[0, 0, j]
        w = tq - lo
        out = []
        for mi in range(2):
            m_old, l_old = carry[2 * mi][:, lo:], carry[2 * mi + 1][:, lo:]
            if diag is None:
                s = s_scr[mi]
            elif diag == 0:
                s = jnp.concatenate([jnp.where(tri, s_scr[mi, :, :tk], NEG), s_scr[mi, :, tk:]], axis=1)
            else:
                s = jnp.where(tri, s_scr[mi, :, lo:], NEG)
            m_new = jnp.maximum(m_old, jnp.max(s, axis=0, keepdims=True))
            alpha = jnp.exp2(m_old - m_new)
            p = jnp.exp2(s - m_new[0:1])
            l_new = alpha * l_old + jnp.sum(p, axis=0, keepdims=True)
            acc_scr[mi, :, lo:] = (jnp.broadcast_to(alpha[0:1], (ATTN_V_DIM, w)) * acc_scr[mi, :, lo:]
                                   + _dot(vb, p.astype(BF16)))
            if lo:
                m_new = jnp.concatenate([carry[2 * mi][:, :lo], m_new], axis=1)
                l_new = jnp.concatenate([carry[2 * mi + 1][:, :lo], l_new], axis=1)
            out += [m_new, l_new]
        return tuple(out)

    def pair(p, carry):
        qk(2 * p + 1, sb_scr)
        carry = process(2 * p, sa_scr, carry)
        qk(2 * p + 2, sa_scr)
        return process(2 * p + 1, sb_scr, carry)

    init = (jnp.full((SUB, tq), NEG, F32), jnp.zeros((SUB, tq), F32)) * 2
    qk(0, sa_scr)
    carry = lax.fori_loop(0, i, pair, init)
    qk(2 * i + 1, sb_scr, lo=tk)
    carry = process(2 * i, sa_scr, carry, diag=0)
    carry = process(2 * i + 1, sb_scr, carry, diag=1, lo=tk)

    lv = lam_ref[...]
    lam = (jnp.exp(jnp.sum(lv[0:1] * lv[1:2], keepdims=True))
           - jnp.exp(jnp.sum(lv[2:3] * lv[3:4], keepdims=True)) + lambda_init)
    o = acc_scr[0] / carry[1][0:1] - lam * (acc_scr[1] / carry[3][0:1])
    ms = jnp.mean(o * o, axis=0, keepdims=True)
    o = o * lax.rsqrt(ms + EPS) * sg_ref[...] * (1.0 - lambda_init)
    o_ref[...] = o.T.astype(BF16)


def _attention(qt, kk, vt, lambda_qk, subln_g, lambda_init, batch, seq, tq, tk):
    nq = seq // tq
    sg_b = jnp.broadcast_to(subln_g.astype(F32)[:, None], (ATTN_V_DIM, tq))
    return pl.pallas_call(
        functools.partial(_attn_body, tq=tq, tk=tk, lambda_init=lambda_init),
        grid=(batch, ATTN_HEADS, nq),
        in_specs=[pl.BlockSpec((4, ATTN_HEAD_DIM), lambda b, h, i: (0, 0)),
                  pl.BlockSpec((ATTN_V_DIM, tq), lambda b, h, i: (0, 0)),
                  pl.BlockSpec((1, 2, LANES, tq), lambda b, h, i: (b, h, 0, i)),
                  pl.BlockSpec((1, 1, seq, LANES), lambda b, h, i: (b, h, 0, 0)),
                  pl.BlockSpec((1, 1, seq // tk, LANES, tk), lambda b, h, i: (b, h, 0, 0, 0))],
        out_specs=pl.BlockSpec((tq, LANES), lambda b, h, i: (b * nq + i, h)),
        out_shape=jax.ShapeDtypeStruct((batch * seq, ATTN_HEADS * ATTN_V_DIM), BF16),
        scratch_shapes=[pltpu.VMEM((2, ATTN_V_DIM, tq), F32),
                        pltpu.VMEM((2, tk, tq), F32),
                        pltpu.VMEM((2, tk, tq), F32)],
        compiler_params=_params(("arbitrary", "arbitrary", "arbitrary")),
        name="attn",
    )(lambda_qk.astype(F32), sg_b, qt, kk, vt)


def _ssd_body(z0_ref, z1_ref, x0_ref, x1_ref, bc_ref, dt_ref, cw_ref, cb_ref, dtb_ref, alog_ref,
              dsk_ref, ng_ref, tri_ref, shift_ref, y_ref, ext_scr, st_scr):
    L = SSD_CHUNK
    halo = SSD_HALO
    c = pl.program_id(1)

    @pl.when(c == 0)
    def _():
        ext_scr[0:halo, :] = jnp.zeros((halo, SSD_CONV_DIM), BF16)
        st_scr[...] = jnp.zeros(st_scr.shape, F32)

    ext_scr[halo:halo + L, 0:1024] = x0_ref[...]
    ext_scr[halo:halo + L, 1024:2048] = x1_ref[...]
    ext_scr[halo:halo + L, 2048:3072] = bc_ref[...]
    shift = shift_ref[...]

    def conv_silu(lo, hi):
        sh = _dot(shift, ext_scr[:, lo:hi])
        acc = cb_ref[:, lo:hi] + cw_ref[SSD_CONV - 1:SSD_CONV, lo:hi] * ext_scr[halo:halo + L, lo:hi].astype(F32)
        for k in range(SSD_CONV - 1):
            acc = acc + cw_ref[k:k + 1, lo:hi] * sh[k * L:(k + 1) * L]
        return acc * _sigmoid(acc)

    xdt_in = dt_ref[...] + dtb_ref[...]
    dtp = jnp.maximum(xdt_in, 0.0) + jnp.log(1.0 + jnp.exp(-jnp.abs(xdt_in)))
    a = dtp * (-jnp.exp(alog_ref[...]) * LOG2E)
    tri = tri_ref[...]
    a_hi, a_mid, a_lo = _split3(a)
    acum = _dot(tri, a_hi) + _dot(tri, a_mid) + _dot(tri, a_lo)
    acum_t = acum.T
    lane = lax.broadcasted_iota(jnp.int32, (L, LANES), 1)
    first = lane < SSD_HEAD_DIM
    tril = lax.broadcasted_iota(jnp.int32, (L, L), 0) >= lax.broadcasted_iota(jnp.int32, (L, L), 1)
    gn = SSD_GROUPS * SSD_D_STATE
    gw = SSD_D_INNER // SSD_GROUPS

    def col(mat, h):
        return jnp.broadcast_to(mat[:, h:h + 1], (L, LANES))

    for g in range(SSD_GROUPS):
        bg = conv_silu(SSD_D_INNER + g * SSD_D_STATE, SSD_D_INNER + (g + 1) * SSD_D_STATE)
        cg = conv_silu(SSD_D_INNER + gn + g * SSD_D_STATE, SSD_D_INNER + gn + (g + 1) * SSD_D_STATE).astype(BF16)
        xg = conv_silu(g * gw, (g + 1) * gw)
        bgt = bg.T.astype(BF16)
        cb = _dot(cg, bgt)
        sprev = st_scr[g]
        yoff = _dot(cg, sprev.astype(BF16))
        y_parts, xdd_parts, dec_parts = [], [], []
        for p in range(4):
            h0 = 8 * g + 2 * p
            c0, c1 = col(acum, h0), col(acum, h0 + 1)
            ac_exp = jnp.where(first, c0, c1)
            dt_exp = jnp.where(first, col(dtp, h0), col(dtp, h0 + 1))
            xsl = xg[:, p * LANES:(p + 1) * LANES]
            xdt = xsl * dt_exp
            l0 = jnp.exp2(jnp.where(tril, c0 - acum_t[h0:h0 + 1, :], NEG))
            l1 = jnp.exp2(jnp.where(tril, c1 - acum_t[h0 + 1:h0 + 2, :], NEG))
            m2 = jnp.concatenate([(cb * l0).astype(BF16), (cb * l1).astype(BF16)], axis=1)
            xb = xdt.astype(BF16)
            zb = jnp.zeros_like(xb)
            rhs = jnp.concatenate([jnp.where(first, xb, zb), jnp.where(first, zb, xb)], axis=0)
            yd = _dot(m2, rhs)
            yo = yoff[:, p * LANES:(p + 1) * LANES] * jnp.exp2(ac_exp)
            dsk = dsk_ref[:, h0 * SSD_HEAD_DIM:h0 * SSD_HEAD_DIM + LANES]
            y_parts.append(yd + yo + xsl * dsk)
            last = ac_exp[L - 1:L, :]
            xdd_parts.append((xdt * jnp.exp2(last - ac_exp)).astype(BF16))
            dec_parts.append(jnp.exp2(last))
        xdd = jnp.concatenate(xdd_parts, axis=1)
        dec = jnp.concatenate(dec_parts, axis=1)
        st_scr[g] = sprev * dec + _dot(bgt, xdd)
        yg = jnp.concatenate(y_parts, axis=1)
        zref = z0_ref if g < 2 else z1_ref
        zg = zref[:, (g % 2) * gw:(g % 2 + 1) * gw].astype(F32)
        yg = yg * (zg * _sigmoid(zg))
        ms = jnp.mean(yg * yg, axis=-1, keepdims=True)
        yg = yg * lax.rsqrt(ms + EPS) * ng_ref[:, g * gw:(g + 1) * gw]
        y_ref[:, g * gw:(g + 1) * gw] = yg.astype(BF16)
    ext_scr[0:halo, :] = ext_scr[L:L + halo, :]


def _ssd(proj, dt, conv_w, conv_b, dt_bias, a_log, d_skip, norm_g, batch, seq):
    L = SSD_CHUNK
    nc = seq // L
    pad = LANES - SSD_HEADS
    dtb = jnp.pad(dt_bias.astype(F32), (0, pad)).reshape(1, LANES)
    alog = jnp.pad(a_log.astype(F32), (0, pad)).reshape(1, LANES)
    dsk = jnp.repeat(d_skip.astype(F32), SSD_HEAD_DIM).reshape(1, SSD_D_INNER)
    tri = jnp.tril(jnp.ones((L, L), F32)).astype(BF16)
    tt = jnp.arange(L)
    shift = jnp.concatenate(
        [jax.nn.one_hot(SSD_HALO + tt - (SSD_CONV - 1) + k, L + SSD_HALO, dtype=BF16) for k in range(SSD_CONV - 1)],
        axis=0)
    row = lambda b, c: (b * nc + c)
    const = lambda b, c: (0, 0)
    return pl.pallas_call(
        _ssd_body,
        grid=(batch, nc),
        in_specs=[pl.BlockSpec((L, 1024), lambda b, c: (row(b, c), 5)),
                  pl.BlockSpec((L, 1024), lambda b, c: (row(b, c), 6)),
                  pl.BlockSpec((L, 1024), lambda b, c: (row(b, c), 7)),
                  pl.BlockSpec((L, 1024), lambda b, c: (row(b, c), 8)),
                  pl.BlockSpec((L, 1024), lambda b, c: (row(b, c), 9)),
                  pl.BlockSpec((L, LANES), lambda b, c: (row(b, c), 0)),
                  pl.BlockSpec((SSD_CONV, SSD_CONV_DIM), const),
                  pl.BlockSpec((1, SSD_CONV_DIM), const),
                  pl.BlockSpec((1, LANES), const),
                  pl.BlockSpec((1, LANES), const),
                  pl.BlockSpec((1, SSD_D_INNER), const),
                  pl.BlockSpec((1, SSD_D_INNER), const),
                  pl.BlockSpec((L, L), const),
                  pl.BlockSpec(((SSD_CONV - 1) * L, L + SSD_HALO), const)],
        out_specs=pl.BlockSpec((L, SSD_D_INNER), lambda b, c: (row(b, c), 0)),
        out_shape=jax.ShapeDtypeStruct((batch * seq, SSD_D_INNER), BF16),
        scratch_shapes=[pltpu.VMEM((L + SSD_HALO, SSD_CONV_DIM), BF16),
                        pltpu.VMEM((SSD_GROUPS, SSD_D_STATE, SSD_D_INNER // SSD_GROUPS), F32)],
        compiler_params=_params(("arbitrary", "arbitrary")),
        name="ssd",
    )(proj, proj, proj, proj, proj, dt, conv_w.astype(F32), conv_b.astype(F32).reshape(1, -1), dtb, alog,
      dsk, norm_g.astype(F32).reshape(1, -1), tri, shift)


def _outproj_body(ya_ref, yb_ref, ga_ref, gb_ref, x_ref, g1_ref, sc_ref, sh_ref, n2_ref,
                  wa_ref, wb_ref, wo_ref, wr_ref, x1_ref, h2_ref, lg_ref):
    pa = _dot(ya_ref[...], wa_ref[...])
    pb = _dot(yb_ref[...], wb_ref[...])
    merged = _sigmoid(ga_ref[...].astype(F32)) * pa + _sigmoid(gb_ref[...].astype(F32)) * pb
    x1 = x_ref[...] + g1_ref[0] * _dot(merged.astype(BF16), wo_ref[...])
    x1_ref[...] = x1
    ms = jnp.mean(x1 * x1, axis=-1, keepdims=True)
    h2 = x1 * lax.rsqrt(ms + EPS) * n2_ref[...]
    h2 = h2 * (1.0 + sc_ref[0]) + sh_ref[0]
    tm = h2.shape[0]
    for s in range(SUB):
        h2_ref[pl.ds(s, tm, stride=SUB), :] = h2[:, s * LANES:(s + 1) * LANES]
    lg_ref[...] = _dot(h2.astype(BF16), wr_ref[...])


def _outproj(ya, yb, proj, x2d, g1, sc2, sh2, norm2_g, wa, wb, wo, wr, seq):
    t, d = x2d.shape
    tm = min(512, seq)
    per_b = seq // tm
    const = lambda i: (0, 0)
    mod = lambda i: (i // per_b, 0, 0)
    return pl.pallas_call(
        _outproj_body,
        grid=(t // tm,),
        in_specs=[pl.BlockSpec((tm, d), lambda i: (i, 0)),
                  pl.BlockSpec((tm, SSD_D_INNER), lambda i: (i, 0)),
                  pl.BlockSpec((tm, d), lambda i: (i, 3)),
                  pl.BlockSpec((tm, d), lambda i: (i, 4)),
                  pl.BlockSpec((tm, d), lambda i: (i, 0)),
                  pl.BlockSpec((1, 1, d), mod),
                  pl.BlockSpec((1, 1, d), mod),
                  pl.BlockSpec((1, 1, d), mod),
                  pl.BlockSpec((1, d), const),
                  pl.BlockSpec((d, d), const),
                  pl.BlockSpec((SSD_D_INNER, d), const),
                  pl.BlockSpec((d, d), const),
                  pl.BlockSpec((d, LANES), const)],
        out_specs=[pl.BlockSpec((tm, d), lambda i: (i, 0)),
                   pl.BlockSpec((tm * SUB, LANES), lambda i: (i, 0)),
                   pl.BlockSpec((tm, LANES), lambda i: (i, 0))],
        out_shape=[jax.ShapeDtypeStruct((t, d), F32),
                   jax.ShapeDtypeStruct((t * SUB, LANES), F32),
                   jax.ShapeDtypeStruct((t, LANES), F32)],
        compiler_params=_params(("arbitrary",)),
        name="outproj",
    )(ya, yb, proj, proj, x2d, g1, sc2, sh2, norm2_g.astype(F32).reshape(1, d), wa, wb, wo, wr)


def _router_body(lg_ref, bias_ref, idx_ref, w_ref):
    tr = lg_ref.shape[0]
    per = N_EXPERTS // N_EXPERT_GROUPS
    lt = lg_ref[...].T
    scores = _sigmoid(lt[:N_EXPERTS])
    choice = scores + bias_ref[...]
    iota = lax.broadcasted_iota(jnp.int32, (per, tr), 0)
    grp = [choice[g * per:(g + 1) * per] for g in range(N_EXPERT_GROUPS)]
    sc_g = [scores[g * per:(g + 1) * per] for g in range(N_EXPERT_GROUPS)]

    gsc = jnp.zeros((N_EXPERT_GROUPS, tr), F32)
    for g in range(N_EXPERT_GROUPS):
        top1 = jnp.max(grp[g], axis=0, keepdims=True)
        idx1 = jnp.min(jnp.where(grp[g] == top1, iota, per), axis=0, keepdims=True)
        top2 = jnp.max(jnp.where(iota == idx1, NEG, grp[g]), axis=0, keepdims=True)
        gsc = jnp.where(iota == g, top1 + top2, gsc)

    gsel = jnp.zeros((N_EXPERT_GROUPS, tr), F32)
    cur = gsc
    for _ in range(TOPK_GROUPS):
        mx = jnp.max(cur, axis=0, keepdims=True)
        idx = jnp.min(jnp.where(cur == mx, iota, N_EXPERT_GROUPS), axis=0, keepdims=True)
        hit = iota == idx
        gsel = jnp.where(hit, 1.0, gsel)
        cur = jnp.where(hit, NEG, cur)

    cur_g = [jnp.where(gsel[g:g + 1] > 0.0, grp[g], NEG) for g in range(N_EXPERT_GROUPS)]
    idx_rows = jnp.zeros((TOP_K, tr), jnp.int32)
    w_rows = jnp.zeros((TOP_K, tr), F32)
    for kk in range(TOP_K):
        mx = cur_g[0]
        for g in range(1, N_EXPERT_GROUPS):
            mx = jnp.maximum(mx, cur_g[g])
        mx = jnp.max(mx, axis=0, keepdims=True)
        idx = jnp.where(cur_g[0] == mx, iota, N_EXPERTS)
        for g in range(1, N_EXPERT_GROUPS):
            idx = jnp.minimum(idx, jnp.where(cur_g[g] == mx, iota + g * per, N_EXPERTS))
        idx = jnp.min(idx, axis=0, keepdims=True)
        wk = jnp.zeros((per, tr), F32)
        for g in range(N_EXPERT_GROUPS):
            hit = (iota + g * per) == idx
            wk = wk + jnp.where(hit, sc_g[g], 0.0)
            cur_g[g] = jnp.where(hit, NEG, cur_g[g])
        idx_rows = jnp.where(iota == kk, idx, idx_rows)
        w_rows = jnp.where(iota == kk, jnp.sum(wk, axis=0, keepdims=True), w_rows)

    wsum = jnp.sum(w_rows, axis=0, keepdims=True)
    idx_ref[...] = idx_rows
    w_ref[...] = w_rows / wsum * ROUTED_SCALE


def _router(logits, router_bias, seq):
    t = logits.shape[0]
    tr = min(1024, seq)
    bias_b = jnp.broadcast_to(router_bias.astype(F32)[:, None], (N_EXPERTS, tr))
    return pl.pallas_call(
        _router_body,
        grid=(t // tr,),
        in_specs=[pl.BlockSpec((tr, LANES), lambda i: (i, 0)),
                  pl.BlockSpec((N_EXPERTS, tr), lambda i: (0, 0))],
        out_specs=[pl.BlockSpec((TOP_K, tr), lambda i: (0, i)),
                   pl.BlockSpec((TOP_K, tr), lambda i: (0, i))],
        out_shape=[jax.ShapeDtypeStruct((TOP_K, t), jnp.int32),
                   jax.ShapeDtypeStruct((TOP_K, t), F32)],
        compiler_params=_params(("arbitrary",)),
        name="router",
    )(logits, bias_b)


def _route_tables(idx_t, w_t, groups, tg, rows, ntiles):
    i32 = jnp.int32
    e = idx_t.reshape(TOP_K, groups, tg).transpose(1, 0, 2).reshape(groups, TOP_K * tg)
    w = w_t.reshape(TOP_K, groups, tg).transpose(1, 0, 2).reshape(groups, TOP_K * tg)
    tl = jnp.broadcast_to(jnp.arange(tg, dtype=i32)[None, None, :], (groups, TOP_K, tg)).reshape(groups, TOP_K * tg)
    skey, sw = lax.sort((e * tg + tl, w), dimension=1, num_keys=1)
    stok = skey % tg
    cnt = jnp.sum((e[:, :, None] == jnp.arange(N_EXPERTS, dtype=i32)[None, None, :]).astype(i32), axis=1)
    off = jnp.cumsum(cnt, axis=1) - cnt
    ntile = (cnt + rows - 1) // rows
    tend = jnp.cumsum(ntile, axis=1)
    tstart = tend - ntile
    n = jnp.arange(ntiles, dtype=i32)
    te = jnp.sum((n[None, :, None] >= tend[:, None, :]).astype(i32), axis=2)
    valid_tile = n[None, :] < tend[:, -1:]
    te = jnp.minimum(te, N_EXPERTS - 1)
    k0 = (n[None, :] - jnp.take_along_axis(tstart, te, axis=1)) * rows
    nvalid = jnp.where(valid_tile, jnp.clip(jnp.take_along_axis(cnt, te, axis=1) - k0, 0, rows), 0)
    start = jnp.clip(jnp.take_along_axis(off, te, axis=1) + k0, 0, TOP_K * tg - 1)
    tok_g, tok_s, wv = _tile_windows(start, nvalid, stok, sw, groups, tg, rows, ntiles)
    return te, nvalid, tok_g, tok_s, wv


def _windows_body(start_ref, nv_ref, stok_ref, sw_ref, tokg_ref, toks_ref, wt_ref, *, rows, tg):
    g = pl.program_id(0)
    nt = tokg_ref.shape[1]
    nl = -(-rows // LANES)
    lane = lax.broadcasted_iota(jnp.int32, (1, LANES), 1)
    rr = lax.broadcasted_iota(jnp.int32, (1, nl * LANES), 1)

    def window(src, q, sh):
        blk = src[0, pl.ds(q, SUB), :]
        rolled = pltpu.roll(blk, (LANES - sh) % LANES, 1)
        parts = [jnp.where(lane < LANES - sh, rolled[k:k + 1], rolled[k + 1:k + 2]) for k in range(nl)]
        return jnp.concatenate(parts, axis=1)

    def body(n, carry):
        s = start_ref[g, n]
        q = s // LANES
        sh = s % LANES
        valid = rr < nv_ref[g, n]
        tok8 = window(stok_ref, q, sh) * SUB
        wv = window(sw_ref, q, sh)
        tokg_ref[0, n] = jnp.where(valid, tok8, 0)[:, :rows]
        toks_ref[0, n] = jnp.where(valid, tok8, tg * SUB)[:, :rows]
        wt_ref[0, n] = jnp.where(valid, wv, 0.0)[:, :rows]
        return carry

    lax.fori_loop(0, nt, body, 0, unroll=4)


def _tile_windows(start, nvalid, stok, sw, groups, tg, rows, ntiles):
    n_list = TOP_K * tg
    n_rows = n_list // LANES + SUB
    pad = n_rows * LANES - n_list
    stok2 = jnp.pad(stok, ((0, 0), (0, pad))).reshape(groups, n_rows, LANES)
    sw2 = jnp.pad(sw, ((0, 0), (0, pad))).reshape(groups, n_rows, LANES)
    lst = pl.BlockSpec((1, n_rows, LANES), lambda g, st, nv: (g, 0, 0))
    out = pl.BlockSpec((1, ntiles, 1, rows), lambda g, st, nv: (g, 0, 0, 0))
    grid_spec = pltpu.PrefetchScalarGridSpec(
        num_scalar_prefetch=2, grid=(groups,), in_specs=[lst, lst], out_specs=[out, out, out])
    shape = (groups, ntiles, 1, rows)
    return pl.pallas_call(
        functools.partial(_windows_body, rows=rows, tg=tg),
        grid_spec=grid_spec,
        out_shape=[jax.ShapeDtypeStruct(shape, jnp.int32), jax.ShapeDtypeStruct(shape, jnp.int32),
                   jax.ShapeDtypeStruct(shape, F32)],
        compiler_params=_params(("arbitrary",)),
        name="windows",
    )(start, nvalid, stok2, sw2)


def _routed_body(te_ref, nv_ref, tokg_ref, toks_ref, wt_ref, h_ref, wg_ref, wu_ref, wd_ref, acc_ref,
                 xbuf, ybuf):
    g = pl.program_id(0)
    n = pl.program_id(1)
    rows = MOE_ROWS
    slot = n % 2
    has_cur = nv_ref[g, n + 1] > 0
    has_prev = nv_ref[g, n] > 0

    @pl.when(n == 0)
    def _():
        acc_ref[...] = jnp.zeros(acc_ref.shape, F32)

    def compute():
        for r in range(rows):
            t8 = pl.multiple_of(tokg_ref[0, 0, 0, r], SUB)
            xbuf[pl.ds(r * SUB, SUB), :] = h_ref[pl.ds(t8, SUB), :]
        x2d = jnp.concatenate([xbuf[pl.ds(s, rows, stride=SUB), :] for s in range(SUB)], axis=1).astype(BF16)
        a = _dot(x2d, wg_ref[0].astype(BF16))
        b = _dot(x2d, wu_ref[0].astype(BF16))
        act = (a * _sigmoid(a) * b).astype(BF16)
        y = _dot(act, wd_ref[0].astype(BF16))
        for s in range(SUB):
            ybuf[slot, pl.ds(s, rows, stride=SUB), :] = y[:, s * LANES:(s + 1) * LANES]

    def scatter_prev():
        for c in range(rows // SUB):
            upd = []
            for r in range(c * SUB, (c + 1) * SUB):
                t8 = pl.multiple_of(toks_ref[0, 0, 0, r], SUB)
                upd.append((t8, acc_ref[0, pl.ds(t8, SUB), :]
                            + wt_ref[0, 0, 0, r] * ybuf[1 - slot, pl.ds(r * SUB, SUB), :]))
            for t8, v in upd:
                acc_ref[0, pl.ds(t8, SUB), :] = v

    @pl.when(jnp.logical_and(has_cur, has_prev))
    def _():
        scatter_prev()
        compute()

    @pl.when(jnp.logical_and(has_cur, jnp.logical_not(has_prev)))
    def _():
        compute()

    @pl.when(jnp.logical_and(jnp.logical_not(has_cur), has_prev))
    def _():
        scatter_prev()


def _routed(hv, tables, w_eg, w_eu, w_ed, groups, tg, ntiles):
    te, nvalid, tok_g, tok_s, wv = tables
    d, f = D_MODEL, EXPERT_DIM
    rows = MOE_ROWS
    i32 = jnp.int32
    te = jnp.concatenate([te, te[:, -1:]], axis=1)
    zero = jnp.zeros((groups, 1), i32)
    nvp = jnp.concatenate([zero, nvalid, zero], axis=1)
    cur = lambda g, n, te, nv: (g, jnp.minimum(n, ntiles - 1), 0, 0)
    prev = lambda g, n, te, nv: (g, jnp.maximum(n - 1, 0), 0, 0)
    smem = functools.partial(pl.BlockSpec, (1, 1, 1, rows), memory_space=pltpu.SMEM)
    grid_spec = pltpu.PrefetchScalarGridSpec(
        num_scalar_prefetch=2,
        grid=(groups, ntiles + 1),
        in_specs=[smem(index_map=cur), smem(index_map=prev), smem(index_map=prev),
                  pl.BlockSpec((tg * SUB, LANES), lambda g, n, te, nv: (g, 0), pipeline_mode=pl.Buffered(1)),
                  pl.BlockSpec((1, d, f), lambda g, n, te, nv: (te[g, n], 0, 0)),
                  pl.BlockSpec((1, d, f), lambda g, n, te, nv: (te[g, n], 0, 0)),
                  pl.BlockSpec((1, f, d), lambda g, n, te, nv: (te[g, n], 0, 0))],
        out_specs=pl.BlockSpec((1, (tg + 1) * SUB, LANES), lambda g, n, te, nv: (g, 0, 0),
                               pipeline_mode=pl.Buffered(1)),
        scratch_shapes=[pltpu.VMEM((rows * SUB, LANES), F32), pltpu.VMEM((2, rows * SUB, LANES), F32)],
    )
    return pl.pallas_call(
        _routed_body,
        grid_spec=grid_spec,
        out_shape=jax.ShapeDtypeStruct((groups, (tg + 1) * SUB, LANES), F32),
        compiler_params=_params(("arbitrary", "arbitrary")),
        name="routed",
    )(te, nvp, tok_g, tok_s, wv, hv, w_eg, w_eu, w_ed)


def _combine_body(h_ref, r_ref, x1_ref, g2_ref, sg_ref, su_ref, sd_ref, o_ref):
    tm = x1_ref.shape[0]
    h = jnp.concatenate([h_ref[pl.ds(s, tm, stride=SUB), :] for s in range(SUB)], axis=1)
    routed = jnp.concatenate([r_ref[0, pl.ds(s, tm, stride=SUB), :] for s in range(SUB)], axis=1)
    hb = h.astype(BF16)
    a = _dot(hb, sg_ref[...])
    b = _dot(hb, su_ref[...])
    shared = _dot((a * _sigmoid(a) * b).astype(BF16), sd_ref[...])
    o_ref[...] = x1_ref[...] + g2_ref[0] * (routed + shared)


def _combine(hv, routed, x1, g2, w_sg, w_su, w_sd, seq, tg):
    t, d = x1.shape
    tm = min(1024, seq, tg)
    per_b = seq // tm
    per_g = tg // tm
    f = w_sg.shape[1]
    const = lambda i: (0, 0)
    row = lambda i: (i, 0)
    return pl.pallas_call(
        _combine_body,
        grid=(t // tm,),
        in_specs=[pl.BlockSpec((tm * SUB, LANES), row),
                  pl.BlockSpec((1, tm * SUB, LANES), lambda i: (i // per_g, i % per_g, 0)),
                  pl.BlockSpec((tm, d), row),
                  pl.BlockSpec((1, 1, d), lambda i: (i // per_b, 0, 0)),
                  pl.BlockSpec((d, f), const),
                  pl.BlockSpec((d, f), const),
                  pl.BlockSpec((f, d), const)],
        out_specs=pl.BlockSpec((tm, d), row),
        out_shape=jax.ShapeDtypeStruct((t, d), F32),
        compiler_params=_params(("arbitrary",)),
        name="combine",
    )(hv, routed, x1, g2, w_sg, w_su, w_sd)


def kernel(x, c, positions, w_ada, b_ada, norm1_g, w_in, q_norm_g, k_norm_g, lambda_qk, subln_g, conv_w, conv_b, dt_bias, a_log, d_skip, ssd_norm_g, w_branch_a, w_branch_b, w_out, norm2_g, w_router, router_bias, w_exp_gate, w_exp_up, w_exp_down, w_sh_gate, w_sh_up, w_sh_down):
    batch, seq, d = x.shape
    t = batch * seq
    tq = min(1024, seq)
    tk = tq // 2
    xf = x.reshape(t, d)
    for layer in range(w_ada.shape[0]):
        lambda_init = 0.8 - 0.6 * math.exp(-0.3 * layer)
        c_pad = jnp.pad(c.astype(F32), ((0, 8 - batch), (0, 0)))
        mod = _ada(c_pad, w_ada[layer], b_ada[layer])[:batch].reshape(batch, 6, 1, d)
        sh1, sc1, g1, sh2, sc2, g2 = (mod[:, n] for n in range(6))

        w_l = w_in[layer]
        wdt = jnp.pad(w_l[:, MAIN_COLS:], ((0, 0), (0, LANES - SSD_HEADS)))
        wdt_hi, wdt_lo = _split2(wdt)
        proj, dt = _inproj(xf, norm1_g[layer], sc1, sh1, w_l[:, :MAIN_COLS].astype(BF16), wdt_hi, wdt_lo, seq)

        qt, kk, vt = _qkvprep(proj, positions, q_norm_g[layer], k_norm_g[layer], batch, seq, tk)
        ya = _attention(qt, kk, vt, lambda_qk[layer], subln_g[layer], lambda_init, batch, seq, tq, tk)
        yb = _ssd(proj, dt, conv_w[layer], conv_b[layer], dt_bias[layer], a_log[layer], d_skip[layer],
                  ssd_norm_g[layer], batch, seq)

        wr = jnp.pad(w_router[layer], ((0, 0), (0, LANES - N_EXPERTS))).astype(BF16)
        x1, h2, logits = _outproj(ya, yb, proj, xf, g1, sc2, sh2, norm2_g[layer],
                                  w_branch_a[layer].astype(BF16), w_branch_b[layer].astype(BF16),
                                  w_out[layer].astype(BF16), wr, seq)
        idx_t, w_t = _router(logits, router_bias[layer], seq)
        tg = min(4096, t)
        groups = t // tg
        ntiles = -(-tg * TOP_K // MOE_ROWS) + N_EXPERTS
        tables = _route_tables(idx_t, w_t, groups, tg, MOE_ROWS, ntiles)
        routed = _routed(h2, tables, w_exp_gate[layer], w_exp_up[layer], w_exp_down[layer], groups, tg, ntiles)
        xf = _combine(h2, routed, x1, g2, w_sh_gate[layer].astype(BF16), w_sh_up[layer].astype(BF16),
                      w_sh_down[layer].astype(BF16), seq, tg)
    return xf.reshape(batch, seq, d)
```

```python
import functools
import math

import jax
import jax.numpy as jnp
from jax import lax
from jax.experimental import pallas as pl
from jax.experimental.pallas import tpu as pltpu

F32 = jnp.float32
BF16 = jnp.bfloat16

D_MODEL = 1024
ATTN_HEADS = 8
ATTN_HEAD_DIM = 64
ATTN_V_DIM = 128
ROPE_THETA = 10000.0
SSD_D_INNER = 2048
SSD_HEAD_DIM = 64
SSD_HEADS = 32
SSD_GROUPS = 4
SSD_D_STATE = 128
SSD_CONV = 4
SSD_CHUNK = 128
SSD_CONV_DIM = 3072
N_EXPERTS = 64
TOP_K = 8
N_EXPERT_GROUPS = 8
TOPK_GROUPS = 4
EXPERT_DIM = 256
ROUTED_SCALE = 2.5
EPS = 1e-6
NEG = -1e30
LOG2E = math.log2(math.e)
Q_SCALE = ATTN_HEAD_DIM ** -0.5 * LOG2E
SSD_HALO = 16

LANES = 128
SUB = 8
MOE_ROWS = 576
MAIN_COLS = 10240
VMEM_LIMIT = 56 * 1024 * 1024


def _dot(a, b):
    return jnp.dot(a, b, preferred_element_type=F32)


def _sigmoid(x):
    return 1.0 / (1.0 + jnp.exp(-x))


def _split2(a):
    hi = a.astype(BF16)
    lo = (a - hi.astype(F32)).astype(BF16)
    return hi, lo


def _split3(a):
    hi = a.astype(BF16)
    r = a - hi.astype(F32)
    mid = r.astype(BF16)
    lo = (r - mid.astype(F32)).astype(BF16)
    return hi, mid, lo


def _params(sem, vmem=VMEM_LIMIT):
    return pltpu.CompilerParams(dimension_semantics=sem, vmem_limit_bytes=vmem)


def _ada_body(c_ref, w_ref, b_ref, o_ref):
    c = c_ref[...]
    sc = c * _sigmoid(c)
    chi, clo = _split2(sc)
    whi, wlo = _split2(w_ref[...])
    o_ref[...] = _dot(chi, whi) + _dot(chi, wlo) + _dot(clo, whi) + b_ref[...]


def _ada(c_pad, w, b):
    d, n = w.shape
    tn = 1024
    return pl.pallas_call(
        _ada_body,
        grid=(n // tn,),
        in_specs=[pl.BlockSpec((8, d), lambda j: (0, 0)),
                  pl.BlockSpec((d, tn), lambda j: (0, j)),
                  pl.BlockSpec((1, tn), lambda j: (0, j))],
        out_specs=pl.BlockSpec((8, tn), lambda j: (0, j)),
        out_shape=jax.ShapeDtypeStruct((8, n), F32),
        compiler_params=_params(("arbitrary",)),
        name="ada",
    )(c_pad, w, b.reshape(1, n))


def _inproj_body(x_ref, g_ref, sc_ref, sh_ref, w_ref, wdh_ref, wdl_ref, o_ref, dt_ref, h_scr):
    @pl.when(pl.program_id(1) == 0)
    def _():
        x = x_ref[...]
        ms = jnp.mean(x * x, axis=-1, keepdims=True)
        h = x * lax.rsqrt(ms + EPS) * g_ref[...]
        h = h * (1.0 + sc_ref[0]) + sh_ref[0]
        hb, hl = _split2(h)
        h_scr[...] = hb
        dt_ref[...] = _dot(hb, wdh_ref[...]) + _dot(hb, wdl_ref[...]) + _dot(hl, wdh_ref[...])

    o_ref[...] = _dot(h_scr[...], w_ref[...]).astype(BF16)


def _inproj(x2d, g, sc, sh, w_main, wdt_hi, wdt_lo, seq):
    t, d = x2d.shape
    tm = min(1024, seq)
    tn = 2560
    per_b = seq // tm
    return pl.pallas_call(
        _inproj_body,
        grid=(t // tm, MAIN_COLS // tn),
        in_specs=[pl.BlockSpec((tm, d), lambda i, j: (i, 0)),
                  pl.BlockSpec((1, d), lambda i, j: (0, 0)),
                  pl.BlockSpec((1, 1, d), lambda i, j: (i // per_b, 0, 0)),
                  pl.BlockSpec((1, 1, d), lambda i, j: (i // per_b, 0, 0)),
                  pl.BlockSpec((d, tn), lambda i, j: (0, j)),
                  pl.BlockSpec((d, LANES), lambda i, j: (0, 0)),
                  pl.BlockSpec((d, LANES), lambda i, j: (0, 0))],
        out_specs=[pl.BlockSpec((tm, tn), lambda i, j: (i, j)),
                   pl.BlockSpec((tm, LANES), lambda i, j: (i, 0))],
        out_shape=[jax.ShapeDtypeStruct((t, MAIN_COLS), BF16),
                   jax.ShapeDtypeStruct((t, LANES), F32)],
        scratch_shapes=[pltpu.VMEM((tm, d), BF16)],
        compiler_params=_params(("arbitrary", "arbitrary")),
        name="inproj",
    )(x2d, g.reshape(1, d), sc, sh, w_main, wdt_hi, wdt_lo)


def _qkvprep_body(q_ref, k_ref, v_ref, pos_ref, invf_ref, gq_ref, gk_ref, qt_ref, ko_ref, vt_ref, *, tk):
    ts = q_ref.shape[0]
    half = ATTN_HEAD_DIM // 2
    ang = pos_ref[0].astype(F32) * invf_ref[...]
    cos = jnp.cos(ang)
    sin = jnp.sin(ang)
    zeros = jnp.zeros((ATTN_HEAD_DIM, ts), F32)

    def norm_rope(blk, g):
        ms = jnp.mean(blk * blk, axis=0, keepdims=True)
        r = blk * lax.rsqrt(ms + EPS) * g
        x1 = r[:half]
        x2 = r[half:]
        return jnp.concatenate([x1 * cos - x2 * sin, x2 * cos + x1 * sin], axis=0)

    for h in range(ATTN_HEADS):
        sl = slice(h * LANES, (h + 1) * LANES)
        qht = q_ref[:, sl].astype(F32).T
        q1 = norm_rope(qht[:ATTN_HEAD_DIM], gq_ref[...]) * Q_SCALE
        q2 = norm_rope(qht[ATTN_HEAD_DIM:], gq_ref[...]) * Q_SCALE
        qt_ref[0, 2 * h] = jnp.concatenate([q1, zeros], axis=0).astype(BF16)
        qt_ref[0, 2 * h + 1] = jnp.concatenate([zeros, q2], axis=0).astype(BF16)
        kht = k_ref[:, sl].astype(F32).T
        k1 = norm_rope(kht[:ATTN_HEAD_DIM], gk_ref[...])
        k2 = norm_rope(kht[ATTN_HEAD_DIM:], gk_ref[...])
        ko_ref[0, h] = jnp.concatenate([k1, k2], axis=0).T.astype(BF16)
        vht = v_ref[:, sl].astype(F32).T.astype(BF16)
        for cc in range(ts // tk):
            vt_ref[0, h, cc] = vht[:, cc * tk:(cc + 1) * tk]


def _qkvprep(proj, positions, q_norm_g, k_norm_g, batch, seq, tk):
    ts = min(512, seq)
    ns = seq // ts
    half = ATTN_HEAD_DIM // 2
    inv_freq = ROPE_THETA ** (-jnp.arange(half, dtype=F32) * 2.0 / ATTN_HEAD_DIM)
    invf_b = jnp.broadcast_to(inv_freq[:, None], (half, ts))
    gq_b = jnp.broadcast_to(q_norm_g.astype(F32)[:, None], (ATTN_HEAD_DIM, ts))
    gk_b = jnp.broadcast_to(k_norm_g.astype(F32)[:, None], (ATTN_HEAD_DIM, ts))
    pos3 = positions.reshape(batch, 1, seq)
    const = lambda b, i: (0, 0)
    return pl.pallas_call(
        functools.partial(_qkvprep_body, tk=tk),
        grid=(batch, ns),
        in_specs=[pl.BlockSpec((ts, D_MODEL), lambda b, i: (b * ns + i, 0)),
                  pl.BlockSpec((ts, D_MODEL), lambda b, i: (b * ns + i, 1)),
                  pl.BlockSpec((ts, D_MODEL), lambda b, i: (b * ns + i, 2)),
                  pl.BlockSpec((1, 1, ts), lambda b, i: (b, 0, i)),
                  pl.BlockSpec((half, ts), const),
                  pl.BlockSpec((ATTN_HEAD_DIM, ts), const),
                  pl.BlockSpec((ATTN_HEAD_DIM, ts), const)],
        out_specs=[pl.BlockSpec((1, 2 * ATTN_HEADS, LANES, ts), lambda b, i: (b, 0, 0, i)),
                   pl.BlockSpec((1, ATTN_HEADS, ts, LANES), lambda b, i: (b, 0, i, 0)),
                   pl.BlockSpec((1, ATTN_HEADS, ts // tk, LANES, tk), lambda b, i: (b, 0, i, 0, 0))],
        out_shape=[jax.ShapeDtypeStruct((batch, 2 * ATTN_HEADS, LANES, seq), BF16),
                   jax.ShapeDtypeStruct((batch, ATTN_HEADS, seq, LANES), BF16),
                   jax.ShapeDtypeStruct((batch, ATTN_HEADS, seq // tk, LANES, tk), BF16)],
        compiler_params=_params(("arbitrary", "arbitrary")),
        name="qkvprep",
    )(proj, proj, proj, pos3, invf_b, gq_b, gk_b)


def _attn_body(lam_ref, sg_ref, qt_ref, k_ref, vt_ref, o_ref, acc_scr, sa_scr, sb_scr, *, tq, tk, lambda_init):
    i = pl.program_id(2)
    q_maps = (qt_ref[0, 0], qt_ref[0, 1])
    acc_scr[...] = jnp.zeros(acc_scr.shape, F32)
    tri = lax.broadcasted_iota(jnp.int32, (tk, tk), 0) <= lax.broadcasted_iota(jnp.int32, (tk, tk), 1)

    def qk(j, s_scr, lo=0):
        kb = k_ref[0, 0, pl.ds(pl.multiple_of(j * tk, tk), tk), :]
        for mi in range(2):
            s_scr[mi, :, lo:] = _dot(kb, q_maps[mi][:, lo:])

    def process(j, s_scr, carry, diag=None, lo=0):
        vb = vt_ref[0, 0, j]
        w = tq - lo
        out = []
        for mi in range(2):
            m_old, l_old = carry[2 * mi][:, lo:], carry[2 * mi + 1][:, lo:]
            if diag is None:
                s = s_scr[mi]
            elif diag == 0:
                s = jnp.concatenate([jnp.where(tri, s_scr[mi, :, :tk], NEG), s_scr[mi, :, tk:]], axis=1)
            else:
                s = jnp.where(tri, s_scr[mi, :, lo:], NEG)
            m_new = jnp.maximum(m_old, jnp.max(s, axis=0, keepdims=True))
            alpha = jnp.exp2(m_old - m_new)
            p = jnp.exp2(s - m_new[0:1])
            l_new = alpha * l_old + jnp.sum(p, axis=0, keepdims=True)
            acc_scr[mi, :, lo:] = (jnp.broadcast_to(alpha[0:1], (ATTN_V_DIM, w)) * acc_scr[mi, :, lo:]
                                   + _dot(vb, p.astype(BF16)))
            if lo:
                m_new = jnp.concatenate([carry[2 * mi][:, :lo], m_new], axis=1)
                l_new = jnp.concatenate([carry[2 * mi + 1][:, :lo], l_new], axis=1)
            out += [m_new, l_new]
        return tuple(out)

    def pair(p, carry):
        qk(2 * p + 1, sb_scr)
        carry = process(2 * p, sa_scr, carry)
        qk(2 * p + 2, sa_scr)
        return process(2 * p + 1, sb_scr, carry)

    init = (jnp.full((SUB, tq), NEG, F32), jnp.zeros((SUB, tq), F32)) * 2
    qk(0, sa_scr)
    carry = lax.fori_loop(0, i, pair, init)
    qk(2 * i + 1, sb_scr, lo=tk)
    carry = process(2 * i, sa_scr, carry, diag=0)
    carry = process(2 * i + 1, sb_scr, carry, diag=1, lo=tk)

    lv = lam_ref[...]
    lam = (jnp.exp(jnp.sum(lv[0:1] * lv[1:2], keepdims=True))
           - jnp.exp(jnp.sum(lv[2:3] * lv[3:4], keepdims=True)) + lambda_init)
    o = acc_scr[0] / carry[1][0:1] - lam * (acc_scr[1] / carry[3][0:1])
    ms = jnp.mean(o * o, axis=0, keepdims=True)
    o = o * lax.rsqrt(ms + EPS) * sg_ref[...] * (1.0 - lambda_init)
    o_ref[...] = o.T.astype(BF16)


def _attention(qt, kk, vt, lambda_qk, subln_g, lambda_init, batch, seq, tq, tk):
    nq = seq // tq
    sg_b = jnp.broadcast_to(subln_g.astype(F32)[:, None], (ATTN_V_DIM, tq))
    return pl.pallas_call(
        functools.partial(_attn_body, tq=tq, tk=tk, lambda_init=lambda_init),
        grid=(batch, ATTN_HEADS, nq),
        in_specs=[pl.BlockSpec((4, ATTN_HEAD_DIM), lambda b, h, i: (0, 0)),
                  pl.BlockSpec((ATTN_V_DIM, tq), lambda b, h, i: (0, 0)),
                  pl.BlockSpec((1, 2, LANES, tq), lambda b, h, i: (b, h, 0, i)),
                  pl.BlockSpec((1, 1, seq, LANES), lambda b, h, i: (b, h, 0, 0)),
                  pl.BlockSpec((1, 1, seq // tk, LANES, tk), lambda b, h, i: (b, h, 0, 0, 0))],
        out_specs=pl.BlockSpec((tq, LANES), lambda b, h, i: (b * nq + i, h)),
        out_shape=jax.ShapeDtypeStruct((batch * seq, ATTN_HEADS * ATTN_V_DIM), BF16),
        scratch_shapes=[pltpu.VMEM((2, ATTN_V_DIM, tq), F32),
                        pltpu.VMEM((2, tk, tq), F32),
                        pltpu.VMEM((2, tk, tq), F32)],
        compiler_params=_params(("arbitrary", "arbitrary", "arbitrary")),
        name="attn",
    )(lambda_qk.astype(F32), sg_b, qt, kk, vt)


def _ssd_body(z0_ref, z1_ref, x0_ref, x1_ref, bc_ref, dt_ref, cw_ref, cb_ref, dtb_ref, alog_ref,
              dsk_ref, ng_ref, tri_ref, shift_ref, y_ref, ext_scr, st_scr):
    L = SSD_CHUNK
    halo = SSD_HALO
    c = pl.program_id(1)

    @pl.when(c == 0)
    def _():
        ext_scr[0:halo, :] = jnp.zeros((halo, SSD_CONV_DIM), BF16)
        st_scr[...] = jnp.zeros(st_scr.shape, F32)

    ext_scr[halo:halo + L, 0:1024] = x0_ref[...]
    ext_scr[halo:halo + L, 1024:2048] = x1_ref[...]
    ext_scr[halo:halo + L, 2048:3072] = bc_ref[...]
    shift = shift_ref[...]

    def conv_silu(lo, hi):
        sh = _dot(shift, ext_scr[:, lo:hi])
        acc = cb_ref[:, lo:hi] + cw_ref[SSD_CONV - 1:SSD_CONV, lo:hi] * ext_scr[halo:halo + L, lo:hi].astype(F32)
        for k in range(SSD_CONV - 1):
            acc = acc + cw_ref[k:k + 1, lo:hi] * sh[k * L:(k + 1) * L]
        return acc * _sigmoid(acc)

    xdt_in = dt_ref[...] + dtb_ref[...]
    dtp = jnp.maximum(xdt_in, 0.0) + jnp.log(1.0 + jnp.exp(-jnp.abs(xdt_in)))
    a = dtp * (-jnp.exp(alog_ref[...]) * LOG2E)
    tri = tri_ref[...]
    a_hi, a_mid, a_lo = _split3(a)
    acum = _dot(tri, a_hi) + _dot(tri, a_mid) + _dot(tri, a_lo)
    acum_t = acum.T
    lane = lax.broadcasted_iota(jnp.int32, (L, LANES), 1)
    first = lane < SSD_HEAD_DIM
    tril = lax.broadcasted_iota(jnp.int32, (L, L), 0) >= lax.broadcasted_iota(jnp.int32, (L, L), 1)
    gn = SSD_GROUPS * SSD_D_STATE
    gw = SSD_D_INNER // SSD_GROUPS

    def col(mat, h):
        return jnp.broadcast_to(mat[:, h:h + 1], (L, LANES))

    for g in range(SSD_GROUPS):
        bg = conv_silu(SSD_D_INNER + g * SSD_D_STATE, SSD_D_INNER + (g + 1) * SSD_D_STATE)
        cg = conv_silu(SSD_D_INNER + gn + g * SSD_D_STATE, SSD_D_INNER + gn + (g + 1) * SSD_D_STATE).astype(BF16)
        xg = conv_silu(g * gw, (g + 1) * gw)
        bgt = bg.T.astype(BF16)
        cb = _dot(cg, bgt)
        sprev = st_scr[g]
        yoff = _dot(cg, sprev.astype(BF16))
        y_parts, xdd_parts, dec_parts = [], [], []
        for p in range(4):
            h0 = 8 * g + 2 * p
            c0, c1 = col(acum, h0), col(acum, h0 + 1)
            ac_exp = jnp.where(first, c0, c1)
            dt_exp = jnp.where(first, col(dtp, h0), col(dtp, h0 + 1))
            xsl = xg[:, p * LANES:(p + 1) * LANES]
            xdt = xsl * dt_exp
            l0 = jnp.exp2(jnp.where(tril, c0 - acum_t[h0:h0 + 1, :], NEG))
            l1 = jnp.exp2(jnp.where(tril, c1 - acum_t[h0 + 1:h0 + 2, :], NEG))
            m2 = jnp.concatenate([(cb * l0).astype(BF16), (cb * l1).astype(BF16)], axis=1)
            xb = xdt.astype(BF16)
            zb = jnp.zeros_like(xb)
            rhs = jnp.concatenate([jnp.where(first, xb, zb), jnp.where(first, zb, xb)], axis=0)
            yd = _dot(m2, rhs)
            yo = yoff[:, p * LANES:(p + 1) * LANES] * jnp.exp2(ac_exp)
            dsk = dsk_ref[:, h0 * SSD_HEAD_DIM:h0 * SSD_HEAD_DIM + LANES]
            y_parts.append(yd + yo + xsl * dsk)
            last = ac_exp[L - 1:L, :]
            xdd_parts.append((xdt * jnp.exp2(last - ac_exp)).astype(BF16))
            dec_parts.append(jnp.exp2(last))
        xdd = jnp.concatenate(xdd_parts, axis=1)
        dec = jnp.concatenate(dec_parts, axis=1)
        st_scr[g] = sprev * dec + _dot(bgt, xdd)
        yg = jnp.concatenate(y_parts, axis=1)
        zref = z0_ref if g < 2 else z1_ref
        zg = zref[:, (g % 2) * gw:(g % 2 + 1) * gw].astype(F32)
        yg = yg * (zg * _sigmoid(zg))
        ms = jnp.mean(yg * yg, axis=-1, keepdims=True)
        yg = yg * lax.rsqrt(ms + EPS) * ng_ref[:, g * gw:(g + 1) * gw]
        y_ref[:, g * gw:(g + 1) * gw] = yg.astype(BF16)
    ext_scr[0:halo, :] = ext_scr[L:L + halo, :]


def _ssd(proj, dt, conv_w, conv_b, dt_bias, a_log, d_skip, norm_g, batch, seq):
    L = SSD_CHUNK
    nc = seq // L
    pad = LANES - SSD_HEADS
    dtb = jnp.pad(dt_bias.astype(F32), (0, pad)).reshape(1, LANES)
    alog = jnp.pad(a_log.astype(F32), (0, pad)).reshape(1, LANES)
    dsk = jnp.repeat(d_skip.astype(F32), SSD_HEAD_DIM).reshape(1, SSD_D_INNER)
    tri = jnp.tril(jnp.ones((L, L), F32)).astype(BF16)
    tt = jnp.arange(L)
    shift = jnp.concatenate(
        [jax.nn.one_hot(SSD_HALO + tt - (SSD_CONV - 1) + k, L + SSD_HALO, dtype=BF16) for k in range(SSD_CONV - 1)],
        axis=0)
    row = lambda b, c: (b * nc + c)
    const = lambda b, c: (0, 0)
    return pl.pallas_call(
        _ssd_body,
        grid=(batch, nc),
        in_specs=[pl.BlockSpec((L, 1024), lambda b, c: (row(b, c), 5)),
                  pl.BlockSpec((L, 1024), lambda b, c: (row(b, c), 6)),
                  pl.BlockSpec((L, 1024), lambda b, c: (row(b, c), 7)),
                  pl.BlockSpec((L, 1024), lambda b, c: (row(b, c), 8)),
                  pl.BlockSpec((L, 1024), lambda b, c: (row(b, c), 9)),
                  pl.BlockSpec((L, LANES), lambda b, c: (row(b, c), 0)),
                  pl.BlockSpec((SSD_CONV, SSD_CONV_DIM), const),
                  pl.BlockSpec((1, SSD_CONV_DIM), const),
                  pl.BlockSpec((1, LANES), const),
                  pl.BlockSpec((1, LANES), const),
                  pl.BlockSpec((1, SSD_D_INNER), const),
                  pl.BlockSpec((1, SSD_D_INNER), const),
                  pl.BlockSpec((L, L), const),
                  pl.BlockSpec(((SSD_CONV - 1) * L, L + SSD_HALO), const)],
        out_specs=pl.BlockSpec((L, SSD_D_INNER), lambda b, c: (row(b, c), 0)),
        out_shape=jax.ShapeDtypeStruct((batch * seq, SSD_D_INNER), BF16),
        scratch_shapes=[pltpu.VMEM((L + SSD_HALO, SSD_CONV_DIM), BF16),
                        pltpu.VMEM((SSD_GROUPS, SSD_D_STATE, SSD_D_INNER // SSD_GROUPS), F32)],
        compiler_params=_params(("arbitrary", "arbitrary")),
        name="ssd",
    )(proj, proj, proj, proj, proj, dt, conv_w.astype(F32), conv_b.astype(F32).reshape(1, -1), dtb, alog,
      dsk, norm_g.astype(F32).reshape(1, -1), tri, shift)


def _outproj_body(ya_ref, yb_ref, ga_ref, gb_ref, x_ref, g1_ref, sc_ref, sh_ref, n2_ref,
                  wa_ref, wb_ref, wo_ref, wr_ref, x1_ref, h2_ref, lg_ref):
    pa = _dot(ya_ref[...], wa_ref[...])
    pb = _dot(yb_ref[...], wb_ref[...])
    merged = _sigmoid(ga_ref[...].astype(F32)) * pa + _sigmoid(gb_ref[...].astype(F32)) * pb
    x1 = x_ref[...] + g1_ref[0] * _dot(merged.astype(BF16), wo_ref[...])
    x1_ref[...] = x1
    ms = jnp.mean(x1 * x1, axis=-1, keepdims=True)
    h2 = x1 * lax.rsqrt(ms + EPS) * n2_ref[...]
    h2 = h2 * (1.0 + sc_ref[0]) + sh_ref[0]
    tm = h2.shape[0]
    for s in range(SUB):
        h2_ref[pl.ds(s, tm, stride=SUB), :] = h2[:, s * LANES:(s + 1) * LANES]
    lg_ref[...] = _dot(h2.astype(BF16), wr_ref[...])


def _outproj(ya, yb, proj, x2d, g1, sc2, sh2, norm2_g, wa, wb, wo, wr, seq):
    t, d = x2d.shape
    tm = min(512, seq)
    per_b = seq // tm
    const = lambda i: (0, 0)
    mod = lambda i: (i // per_b, 0, 0)
    return pl.pallas_call(
        _outproj_body,
        grid=(t // tm,),
        in_specs=[pl.BlockSpec((tm, d), lambda i: (i, 0)),
                  pl.BlockSpec((tm, SSD_D_INNER), lambda i: (i, 0)),
                  pl.BlockSpec((tm, d), lambda i: (i, 3)),
                  pl.BlockSpec((tm, d), lambda i: (i, 4)),
                  pl.BlockSpec((tm, d), lambda i: (i, 0)),
                  pl.BlockSpec((1, 1, d), mod),
                  pl.BlockSpec((1, 1, d), mod),
                  pl.BlockSpec((1, 1, d), mod),
                  pl.BlockSpec((1, d), const),
                  pl.BlockSpec((d, d), const),
                  pl.BlockSpec((SSD_D_INNER, d), const),
                  pl.BlockSpec((d, d), const),
                  pl.BlockSpec((d, LANES), const)],
        out_specs=[pl.BlockSpec((tm, d), lambda i: (i, 0)),
                   pl.BlockSpec((tm * SUB, LANES), lambda i: (i, 0)),
                   pl.BlockSpec((tm, LANES), lambda i: (i, 0))],
        out_shape=[jax.ShapeDtypeStruct((t, d), F32),
                   jax.ShapeDtypeStruct((t * SUB, LANES), F32),
                   jax.ShapeDtypeStruct((t, LANES), F32)],
        compiler_params=_params(("arbitrary",)),
        name="outproj",
    )(ya, yb, proj, proj, x2d, g1, sc2, sh2, norm2_g.astype(F32).reshape(1, d), wa, wb, wo, wr)


def _router_body(lg_ref, bias_ref, idx_ref, w_ref):
    tr = lg_ref.shape[0]
    per = N_EXPERTS // N_EXPERT_GROUPS
    lt = lg_ref[...].T
    scores = _sigmoid(lt[:N_EXPERTS])
    choice = scores + bias_ref[...]
    iota = lax.broadcasted_iota(jnp.int32, (per, tr), 0)
    grp = [choice[g * per:(g + 1) * per] for g in range(N_EXPERT_GROUPS)]
    sc_g = [scores[g * per:(g + 1) * per] for g in range(N_EXPERT_GROUPS)]

    gsc = jnp.zeros((N_EXPERT_GROUPS, tr), F32)
    for g in range(N_EXPERT_GROUPS):
        top1 = jnp.max(grp[g], axis=0, keepdims=True)
        idx1 = jnp.min(jnp.where(grp[g] == top1, iota, per), axis=0, keepdims=True)
        top2 = jnp.max(jnp.where(iota == idx1, NEG, grp[g]), axis=0, keepdims=True)
        gsc = jnp.where(iota == g, top1 + top2, gsc)

    gsel = jnp.zeros((N_EXPERT_GROUPS, tr), F32)
    cur = gsc
    for _ in range(TOPK_GROUPS):
        mx = jnp.max(cur, axis=0, keepdims=True)
        idx = jnp.min(jnp.where(cur == mx, iota, N_EXPERT_GROUPS), axis=0, keepdims=True)
        hit = iota == idx
        gsel = jnp.where(hit, 1.0, gsel)
        cur = jnp.where(hit, NEG, cur)

    cur_g = [jnp.where(gsel[g:g + 1] > 0.0, grp[g], NEG) for g in range(N_EXPERT_GROUPS)]
    idx_rows = jnp.zeros((TOP_K, tr), jnp.int32)
    w_rows = jnp.zeros((TOP_K, tr), F32)
    for kk in range(TOP_K):
        mx = cur_g[0]
        for g in range(1, N_EXPERT_GROUPS):
            mx = jnp.maximum(mx, cur_g[g])
        mx = jnp.max(mx, axis=0, keepdims=True)
        idx = jnp.where(cur_g[0] == mx, iota, N_EXPERTS)
        for g in range(1, N_EXPERT_GROUPS):
            idx = jnp.minimum(idx, jnp.where(cur_g[g] == mx, iota + g * per, N_EXPERTS))
        idx = jnp.min(idx, axis=0, keepdims=True)
        wk = jnp.zeros((per, tr), F32)
        for g in range(N_EXPERT_GROUPS):
            hit = (iota + g * per) == idx
            wk = wk + jnp.where(hit, sc_g[g], 0.0)
            cur_g[g] = jnp.where(hit, NEG, cur_g[g])
        idx_rows = jnp.where(iota == kk, idx, idx_rows)
        w_rows = jnp.where(iota == kk, jnp.sum(wk, axis=0, keepdims=True), w_rows)

    wsum = jnp.sum(w_rows, axis=0, keepdims=True)
    idx_ref[...] = idx_rows
    w_ref[...] = w_rows / wsum * ROUTED_SCALE


def _router(logits, router_bias, seq):
    t = logits.shape[0]
    tr = min(1024, seq)
    bias_b = jnp.broadcast_to(router_bias.astype(F32)[:, None], (N_EXPERTS, tr))
    return pl.pallas_call(
        _router_body,
        grid=(t // tr,),
        in_specs=[pl.BlockSpec((tr, LANES), lambda i: (i, 0)),
                  pl.BlockSpec((N_EXPERTS, tr), lambda i: (0, 0))],
        out_specs=[pl.BlockSpec((TOP_K, tr), lambda i: (0, i)),
                   pl.BlockSpec((TOP_K, tr), lambda i: (0, i))],
        out_shape=[jax.ShapeDtypeStruct((TOP_K, t), jnp.int32),
                   jax.ShapeDtypeStruct((TOP_K, t), F32)],
        compiler_params=_params(("arbitrary",)),
        name="router",
    )(logits, bias_b)


def _route_tables(idx_t, w_t, groups, tg, rows, ntiles):
    i32 = jnp.int32
    e = idx_t.reshape(TOP_K, groups, tg).transpose(1, 0, 2).reshape(groups, TOP_K * tg)
    w = w_t.reshape(TOP_K, groups, tg).transpose(1, 0, 2).reshape(groups, TOP_K * tg)
    tl = jnp.broadcast_to(jnp.arange(tg, dtype=i32)[None, None, :], (groups, TOP_K, tg)).reshape(groups, TOP_K * tg)
    skey, sw = lax.sort((e * tg + tl, w), dimension=1, num_keys=1)
    stok = skey % tg
    cnt = jnp.sum((e[:, :, None] == jnp.arange(N_EXPERTS, dtype=i32)[None, None, :]).astype(i32), axis=1)
    off = jnp.cumsum(cnt, axis=1) - cnt
    ntile = (cnt + rows - 1) // rows
    tend = jnp.cumsum(ntile, axis=1)
    tstart = tend - ntile
    n = jnp.arange(ntiles, dtype=i32)
    te = jnp.sum((n[None, :, None] >= tend[:, None, :]).astype(i32), axis=2)
    valid_tile = n[None, :] < tend[:, -1:]
    te = jnp.minimum(te, N_EXPERTS - 1)
    k0 = (n[None, :] - jnp.take_along_axis(tstart, te, axis=1)) * rows
    nvalid = jnp.where(valid_tile, jnp.clip(jnp.take_along_axis(cnt, te, axis=1) - k0, 0, rows), 0)
    start = jnp.clip(jnp.take_along_axis(off, te, axis=1) + k0, 0, TOP_K * tg - 1)
    tok_g, tok_s, wv = _tile_windows(start, nvalid, stok, sw, groups, tg, rows, ntiles)
    return te, nvalid, tok_g, tok_s, wv


def _windows_body(start_ref, nv_ref, stok_ref, sw_ref, tokg_ref, toks_ref, wt_ref, *, rows, tg):
    g = pl.program_id(0)
    nt = tokg_ref.shape[1]
    nl = -(-rows // LANES)
    lane = lax.broadcasted_iota(jnp.int32, (1, LANES), 1)
    rr = lax.broadcasted_iota(jnp.int32, (1, nl * LANES), 1)

    def window(src, q, sh):
        blk = src[0, pl.ds(q, SUB), :]
        rolled = pltpu.roll(blk, (LANES - sh) % LANES, 1)
        parts = [jnp.where(lane < LANES - sh, rolled[k:k + 1], rolled[k + 1:k + 2]) for k in range(nl)]
        return jnp.concatenate(parts, axis=1)

    def body(n, carry):
        s = start_ref[g, n]
        q = s // LANES
        sh = s % LANES
        valid = rr < nv_ref[g, n]
        tok8 = window(stok_ref, q, sh) * SUB
        wv = window(sw_ref, q, sh)
        tokg_ref[0, n] = jnp.where(valid, tok8, 0)[:, :rows]
        toks_ref[0, n] = jnp.where(valid, tok8, tg * SUB)[:, :rows]
        wt_ref[0, n] = jnp.where(valid, wv, 0.0)[:, :rows]
        return carry

    lax.fori_loop(0, nt, body, 0, unroll=4)


def _tile_windows(start, nvalid, stok, sw, groups, tg, rows, ntiles):
    n_list = TOP_K * tg
    n_rows = n_list // LANES + SUB
    pad = n_rows * LANES - n_list
    stok2 = jnp.pad(stok, ((0, 0), (0, pad))).reshape(groups, n_rows, LANES)
    sw2 = jnp.pad(sw, ((0, 0), (0, pad))).reshape(groups, n_rows, LANES)
    lst = pl.BlockSpec((1, n_rows, LANES), lambda g, st, nv: (g, 0, 0))
    out = pl.BlockSpec((1, ntiles, 1, rows), lambda g, st, nv: (g, 0, 0, 0))
    grid_spec = pltpu.PrefetchScalarGridSpec(
        num_scalar_prefetch=2, grid=(groups,), in_specs=[lst, lst], out_specs=[out, out, out])
    shape = (groups, ntiles, 1, rows)
    return pl.pallas_call(
        functools.partial(_windows_body, rows=rows, tg=tg),
        grid_spec=grid_spec,
        out_shape=[jax.ShapeDtypeStruct(shape, jnp.int32), jax.ShapeDtypeStruct(shape, jnp.int32),
                   jax.ShapeDtypeStruct(shape, F32)],
        compiler_params=_params(("arbitrary",)),
        name="windows",
    )(start, nvalid, stok2, sw2)


def _routed_body(te_ref, nv_ref, tokg_ref, toks_ref, wt_ref, h_ref, wg_ref, wu_ref, wd_ref, acc_ref,
                 xbuf, ybuf):
    g = pl.program_id(0)
    n = pl.program_id(1)
    rows = MOE_ROWS
    slot = n % 2
    has_cur = nv_ref[g, n + 1] > 0
    has_prev = nv_ref[g, n] > 0

    @pl.when(n == 0)
    def _():
        acc_ref[...] = jnp.zeros(acc_ref.shape, F32)

    def compute():
        for r in range(rows):
            t8 = pl.multiple_of(tokg_ref[0, 0, 0, r], SUB)
            xbuf[pl.ds(r * SUB, SUB), :] = h_ref[pl.ds(t8, SUB), :]
        x2d = jnp.concatenate([xbuf[pl.ds(s, rows, stride=SUB), :] for s in range(SUB)], axis=1).astype(BF16)
        a = _dot(x2d, wg_ref[0].astype(BF16))
        b = _dot(x2d, wu_ref[0].astype(BF16))
        act = (a * _sigmoid(a) * b).astype(BF16)
        y = _dot(act, wd_ref[0].astype(BF16))
        for s in range(SUB):
            ybuf[slot, pl.ds(s, rows, stride=SUB), :] = y[:, s * LANES:(s + 1) * LANES]

    def scatter_prev():
        for c in range(rows // SUB):
            upd = []
            for r in range(c * SUB, (c + 1) * SUB):
                t8 = pl.multiple_of(toks_ref[0, 0, 0, r], SUB)
                upd.append((t8, acc_ref[0, pl.ds(t8, SUB), :]
                            + wt_ref[0, 0, 0, r] * ybuf[1 - slot, pl.ds(r * SUB, SUB), :]))
            for t8, v in upd:
                acc_ref[0, pl.ds(t8, SUB), :] = v

    @pl.when(jnp.logical_and(has_cur, has_prev))
    def _():
        scatter_prev()
        compute()

    @pl.when(jnp.logical_and(has_cur, jnp.logical_not(has_prev)))
    def _():
        compute()

    @pl.when(jnp.logical_and(jnp.logical_not(has_cur), has_prev))
    def _():
        scatter_prev()


def _routed(hv, tables, w_eg, w_eu, w_ed, groups, tg, ntiles):
    te, nvalid, tok_g, tok_s, wv = tables
    d, f = D_MODEL, EXPERT_DIM
    rows = MOE_ROWS
    i32 = jnp.int32
    te = jnp.concatenate([te, te[:, -1:]], axis=1)
    zero = jnp.zeros((groups, 1), i32)
    nvp = jnp.concatenate([zero, nvalid, zero], axis=1)
    cur = lambda g, n, te, nv: (g, jnp.minimum(n, ntiles - 1), 0, 0)
    prev = lambda g, n, te, nv: (g, jnp.maximum(n - 1, 0), 0, 0)
    smem = functools.partial(pl.BlockSpec, (1, 1, 1, rows), memory_space=pltpu.SMEM)
    grid_spec = pltpu.PrefetchScalarGridSpec(
        num_scalar_prefetch=2,
        grid=(groups, ntiles + 1),
        in_specs=[smem(index_map=cur), smem(index_map=prev), smem(index_map=prev),
                  pl.BlockSpec((tg * SUB, LANES), lambda g, n, te, nv: (g, 0), pipeline_mode=pl.Buffered(1)),
                  pl.BlockSpec((1, d, f), lambda g, n, te, nv: (te[g, n], 0, 0)),
                  pl.BlockSpec((1, d, f), lambda g, n, te, nv: (te[g, n], 0, 0)),
                  pl.BlockSpec((1, f, d), lambda g, n, te, nv: (te[g, n], 0, 0))],
        out_specs=pl.BlockSpec((1, (tg + 1) * SUB, LANES), lambda g, n, te, nv: (g, 0, 0),
                               pipeline_mode=pl.Buffered(1)),
        scratch_shapes=[pltpu.VMEM((rows * SUB, LANES), F32), pltpu.VMEM((2, rows * SUB, LANES), F32)],
    )
    return pl.pallas_call(
        _routed_body,
        grid_spec=grid_spec,
        out_shape=jax.ShapeDtypeStruct((groups, (tg + 1) * SUB, LANES), F32),
        compiler_params=_params(("arbitrary", "arbitrary")),
        name="routed",
    )(te, nvp, tok_g, tok_s, wv, hv, w_eg, w_eu, w_ed)


def _combine_body(h_ref, r_ref, x1_ref, g2_ref, sg_ref, su_ref, sd_ref, o_ref):
    tm = x1_ref.shape[0]
    h = jnp.concatenate([h_ref[pl.ds(s, tm, stride=SUB), :] for s in range(SUB)], axis=1)
    routed = jnp.concatenate([r_ref[0, pl.ds(s, tm, stride=SUB), :] for s in range(SUB)], axis=1)
    hb = h.astype(BF16)
    a = _dot(hb, sg_ref[...])
    b = _dot(hb, su_ref[...])
    shared = _dot((a * _sigmoid(a) * b).astype(BF16), sd_ref[...])
    o_ref[...] = x1_ref[...] + g2_ref[0] * (routed + shared)


def _combine(hv, routed, x1, g2, w_sg, w_su, w_sd, seq, tg):
    t, d = x1.shape
    tm = min(1024, seq, tg)
    per_b = seq // tm
    per_g = tg // tm
    f = w_sg.shape[1]
    const = lambda i: (0, 0)
    row = lambda i: (i, 0)
    return pl.pallas_call(
        _combine_body,
        grid=(t // tm,),
        in_specs=[pl.BlockSpec((tm * SUB, LANES), row),
                  pl.BlockSpec((1, tm * SUB, LANES), lambda i: (i // per_g, i % per_g, 0)),
                  pl.BlockSpec((tm, d), row),
                  pl.BlockSpec((1, 1, d), lambda i: (i // per_b, 0, 0)),
                  pl.BlockSpec((d, f), const),
                  pl.BlockSpec((d, f), const),
                  pl.BlockSpec((f, d), const)],
        out_specs=pl.BlockSpec((tm, d), row),
        out_shape=jax.ShapeDtypeStruct((t, d), F32),
        compiler_params=_params(("arbitrary",)),
        name="combine",
    )(hv, routed, x1, g2, w_sg, w_su, w_sd)


def kernel(x, c, positions, w_ada, b_ada, norm1_g, w_in, q_norm_g, k_norm_g, lambda_qk, subln_g, conv_w, conv_b, dt_bias, a_log, d_skip, ssd_norm_g, w_branch_a, w_branch_b, w_out, norm2_g, w_router, router_bias, w_exp_gate, w_exp_up, w_exp_down, w_sh_gate, w_sh_up, w_sh_down):
    batch, seq, d = x.shape
    t = batch * seq
    tq = min(1024, seq)
    tk = tq // 2
    xf = x.reshape(t, d)
    for layer in range(w_ada.shape[0]):
        lambda_init = 0.8 - 0.6 * math.exp(-0.3 * layer)
        c_pad = jnp.pad(c.astype(F32), ((0, 8 - batch), (0, 0)))
        mod = _ada(c_pad, w_ada[layer], b_ada[layer])[:batch].reshape(batch, 6, 1, d)
        sh1, sc1, g1, sh2, sc2, g2 = (mod[:, n] for n in range(6))

        w_l = w_in[layer]
        wdt = jnp.pad(w_l[:, MAIN_COLS:], ((0, 0), (0, LANES - SSD_HEADS)))
        wdt_hi, wdt_lo = _split2(wdt)
        proj, dt = _inproj(xf, norm1_g[layer], sc1, sh1, w_l[:, :MAIN_COLS].astype(BF16), wdt_hi, wdt_lo, seq)

        qt, kk, vt = _qkvprep(proj, positions, q_norm_g[layer], k_norm_g[layer], batch, seq, tk)
        ya = _attention(qt, kk, vt, lambda_qk[layer], subln_g[layer], lambda_init, batch, seq, tq, tk)
        yb = _ssd(proj, dt, conv_w[layer], conv_b[layer], dt_bias[layer], a_log[layer], d_skip[layer],
                  ssd_norm_g[layer], batch, seq)

        wr = jnp.pad(w_router[layer], ((0, 0), (0, LANES - N_EXPERTS))).astype(BF16)
        x1, h2, logits = _outproj(ya, yb, proj, xf, g1, sc2, sh2, norm2_g[layer],
                                  w_branch_a[layer].astype(BF16), w_branch_b[layer].astype(BF16),
                                  w_out[layer].astype(BF16), wr, seq)
        idx_t, w_t = _router(logits, router_bias[layer], seq)
        tg = min(4096, t)
        groups = t // tg
        ntiles = -(-tg * TOP_K // MOE_ROWS) + N_EXPERTS
        tables = _route_tables(idx_t, w_t, groups, tg, MOE_ROWS, ntiles)
        routed = _routed(h2, tables, w_exp_gate[layer], w_exp_up[layer], w_exp_down[layer], groups, tg, ntiles)
        xf = _combine(h2, routed, x1, g2, w_sh_gate[layer].astype(BF16), w_sh_up[layer].astype(BF16),
                      w_sh_down[layer].astype(BF16), seq, tg)
    return xf.reshape(batch, seq, d)
```

```python
import functools
import math

import jax
import jax.numpy as jnp
from jax import lax
from jax.experimental import pallas as pl
from jax.experimental.pallas import tpu as pltpu

F32 = jnp.float32
BF16 = jnp.bfloat16

D_MODEL = 1024
ATTN_HEADS = 8
ATTN_HEAD_DIM = 64
ATTN_V_DIM = 128
ROPE_THETA = 10000.0
SSD_D_INNER = 2048
SSD_HEAD_DIM = 64
SSD_HEADS = 32
SSD_GROUPS = 4
SSD_D_STATE = 128
SSD_CONV = 4
SSD_CHUNK = 128
SSD_CONV_DIM = 3072
N_EXPERTS = 64
TOP_K = 8
N_EXPERT_GROUPS = 8
TOPK_GROUPS = 4
EXPERT_DIM = 256
ROUTED_SCALE = 2.5
EPS = 1e-6
NEG = -1e30
LOG2E = math.log2(math.e)
Q_SCALE = ATTN_HEAD_DIM ** -0.5 * LOG2E
SSD_HALO = 16

LANES = 128
SUB = 8
MOE_ROWS = 560
MAIN_COLS = 10240
VMEM_LIMIT = 56 * 1024 * 1024


def _dot(a, b):
    return jnp.dot(a, b, preferred_element_type=F32)


def _sigmoid(x):
    return 1.0 / (1.0 + jnp.exp(-x))


def _split2(a):
    hi = a.astype(BF16)
    lo = (a - hi.astype(F32)).astype(BF16)
    return hi, lo


def _split3(a):
    hi = a.astype(BF16)
    r = a - hi.astype(F32)
    mid = r.astype(BF16)
    lo = (r - mid.astype(F32)).astype(BF16)
    return hi, mid, lo


def _params(sem, vmem=VMEM_LIMIT):
    return pltpu.CompilerParams(dimension_semantics=sem, vmem_limit_bytes=vmem)


def _ada_body(c_ref, w_ref, b_ref, o_ref):
    c = c_ref[...]
    sc = c * _sigmoid(c)
    chi, clo = _split2(sc)
    whi, wlo = _split2(w_ref[...])
    o_ref[...] = _dot(chi, whi) + _dot(chi, wlo) + _dot(clo, whi) + b_ref[...]


def _ada(c_pad, w, b):
    d, n = w.shape
    tn = 1024
    return pl.pallas_call(
        _ada_body,
        grid=(n // tn,),
        in_specs=[pl.BlockSpec((8, d), lambda j: (0, 0)),
                  pl.BlockSpec((d, tn), lambda j: (0, j)),
                  pl.BlockSpec((1, tn), lambda j: (0, j))],
        out_specs=pl.BlockSpec((8, tn), lambda j: (0, j)),
        out_shape=jax.ShapeDtypeStruct((8, n), F32),
        compiler_params=_params(("arbitrary",)),
        name="ada",
    )(c_pad, w, b.reshape(1, n))


def _inproj_body(x_ref, g_ref, sc_ref, sh_ref, w_ref, wdh_ref, wdl_ref, o_ref, dt_ref, h_scr):
    @pl.when(pl.program_id(1) == 0)
    def _():
        x = x_ref[...]
        ms = jnp.mean(x * x, axis=-1, keepdims=True)
        h = x * lax.rsqrt(ms + EPS) * g_ref[...]
        h = h * (1.0 + sc_ref[0]) + sh_ref[0]
        hb, hl = _split2(h)
        h_scr[...] = hb
        dt_ref[...] = _dot(hb, wdh_ref[...]) + _dot(hb, wdl_ref[...]) + _dot(hl, wdh_ref[...])

    o_ref[...] = _dot(h_scr[...], w_ref[...]).astype(BF16)


def _inproj(x2d, g, sc, sh, w_main, wdt_hi, wdt_lo, seq):
    t, d = x2d.shape
    tm = min(1024, seq)
    tn = 2560
    per_b = seq // tm
    return pl.pallas_call(
        _inproj_body,
        grid=(t // tm, MAIN_COLS // tn),
        in_specs=[pl.BlockSpec((tm, d), lambda i, j: (i, 0)),
                  pl.BlockSpec((1, d), lambda i, j: (0, 0)),
                  pl.BlockSpec((1, 1, d), lambda i, j: (i // per_b, 0, 0)),
                  pl.BlockSpec((1, 1, d), lambda i, j: (i // per_b, 0, 0)),
                  pl.BlockSpec((d, tn), lambda i, j: (0, j)),
                  pl.BlockSpec((d, LANES), lambda i, j: (0, 0)),
                  pl.BlockSpec((d, LANES), lambda i, j: (0, 0))],
        out_specs=[pl.BlockSpec((tm, tn), lambda i, j: (i, j)),
                   pl.BlockSpec((tm, LANES), lambda i, j: (i, 0))],
        out_shape=[jax.ShapeDtypeStruct((t, MAIN_COLS), BF16),
                   jax.ShapeDtypeStruct((t, LANES), F32)],
        scratch_shapes=[pltpu.VMEM((tm, d), BF16)],
        compiler_params=_params(("arbitrary", "arbitrary")),
        name="inproj",
    )(x2d, g.reshape(1, d), sc, sh, w_main, wdt_hi, wdt_lo)


def _qkvprep_body(q_ref, k_ref, v_ref, pos_ref, invf_ref, gq_ref, gk_ref, qt_ref, ko_ref, vt_ref, *, tk):
    ts = q_ref.shape[0]
    half = ATTN_HEAD_DIM // 2
    ang = pos_ref[0].astype(F32) * invf_ref[...]
    cos = jnp.cos(ang)
    sin = jnp.sin(ang)
    zeros = jnp.zeros((ATTN_HEAD_DIM, ts), F32)

    def norm_rope(blk, g):
        ms = jnp.mean(blk * blk, axis=0, keepdims=True)
        r = blk * lax.rsqrt(ms + EPS) * g
        x1 = r[:half]
        x2 = r[half:]
        return jnp.concatenate([x1 * cos - x2 * sin, x2 * cos + x1 * sin], axis=0)

    for h in range(ATTN_HEADS):
        sl = slice(h * LANES, (h + 1) * LANES)
        qht = q_ref[:, sl].astype(F32).T
        q1 = norm_rope(qht[:ATTN_HEAD_DIM], gq_ref[...]) * Q_SCALE
        q2 = norm_rope(qht[ATTN_HEAD_DIM:], gq_ref[...]) * Q_SCALE
        qt_ref[0, 2 * h] = jnp.concatenate([q1, zeros], axis=0).astype(BF16)
        qt_ref[0, 2 * h + 1] = jnp.concatenate([zeros, q2], axis=0).astype(BF16)
        kht = k_ref[:, sl].astype(F32).T
        k1 = norm_rope(kht[:ATTN_HEAD_DIM], gk_ref[...])
        k2 = norm_rope(kht[ATTN_HEAD_DIM:], gk_ref[...])
        ko_ref[0, h] = jnp.concatenate([k1, k2], axis=0).T.astype(BF16)
        vht = v_ref[:, sl].astype(F32).T.astype(BF16)
        for cc in range(ts // tk):
            vt_ref[0, h, cc] = vht[:, cc * tk:(cc + 1) * tk]


def _qkvprep(proj, positions, q_norm_g, k_norm_g, batch, seq, tk):
    ts = min(512, seq)
    ns = seq // ts
    half = ATTN_HEAD_DIM // 2
    inv_freq = ROPE_THETA ** (-jnp.arange(half, dtype=F32) * 2.0 / ATTN_HEAD_DIM)
    invf_b = jnp.broadcast_to(inv_freq[:, None], (half, ts))
    gq_b = jnp.broadcast_to(q_norm_g.astype(F32)[:, None], (ATTN_HEAD_DIM, ts))
    gk_b = jnp.broadcast_to(k_norm_g.astype(F32)[:, None], (ATTN_HEAD_DIM, ts))
    pos3 = positions.reshape(batch, 1, seq)
    const = lambda b, i: (0, 0)
    return pl.pallas_call(
        functools.partial(_qkvprep_body, tk=tk),
        grid=(batch, ns),
        in_specs=[pl.BlockSpec((ts, D_MODEL), lambda b, i: (b * ns + i, 0)),
                  pl.BlockSpec((ts, D_MODEL), lambda b, i: (b * ns + i, 1)),
                  pl.BlockSpec((ts, D_MODEL), lambda b, i: (b * ns + i, 2)),
                  pl.BlockSpec((1, 1, ts), lambda b, i: (b, 0, i)),
                  pl.BlockSpec((half, ts), const),
                  pl.BlockSpec((ATTN_HEAD_DIM, ts), const),
                  pl.BlockSpec((ATTN_HEAD_DIM, ts), const)],
        out_specs=[pl.BlockSpec((1, 2 * ATTN_HEADS, LANES, ts), lambda b, i: (b, 0, 0, i)),
                   pl.BlockSpec((1, ATTN_HEADS, ts, LANES), lambda b, i: (b, 0, i, 0)),
                   pl.BlockSpec((1, ATTN_HEADS, ts // tk, LANES, tk), lambda b, i: (b, 0, i, 0, 0))],
        out_shape=[jax.ShapeDtypeStruct((batch, 2 * ATTN_HEADS, LANES, seq), BF16),
                   jax.ShapeDtypeStruct((batch, ATTN_HEADS, seq, LANES), BF16),
                   jax.ShapeDtypeStruct((batch, ATTN_HEADS, seq // tk, LANES, tk), BF16)],
        compiler_params=_params(("arbitrary", "arbitrary")),
        name="qkvprep",
    )(proj, proj, proj, pos3, invf_b, gq_b, gk_b)


def _attn_body(lam_ref, sg_ref, qt_ref, k_ref, vt_ref, o_ref, acc_scr, sa_scr, sb_scr, *, tq, tk, lambda_init):
    i = pl.program_id(2)
    q_maps = (qt_ref[0, 0], qt_ref[0, 1])
    acc_scr[...] = jnp.zeros(acc_scr.shape, F32)
    tri = lax.broadcasted_iota(jnp.int32, (tk, tk), 0) <= lax.broadcasted_iota(jnp.int32, (tk, tk), 1)

    def qk(j, s_scr, lo=0):
        kb = k_ref[0, 0, pl.ds(pl.multiple_of(j * tk, tk), tk), :]
        for mi in range(2):
            s_scr[mi, :, lo:] = _dot(kb, q_maps[mi][:, lo:])

    def process(j, s_scr, carry, diag=None, lo=0):
        vb = vt_ref[0, 0, j]
        w = tq - lo
        out = []
        for mi in range(2):
            m_old, l_old = carry[2 * mi][:, lo:], carry[2 * mi + 1][:, lo:]
            if diag is None:
                s = s_scr[mi]
            elif diag == 0:
                s = jnp.concatenate([jnp.where(tri, s_scr[mi, :, :tk], NEG), s_scr[mi, :, tk:]], axis=1)
            else:
                s = jnp.where(tri, s_scr[mi, :, lo:], NEG)
            m_new = jnp.maximum(m_old, jnp.max(s, axis=0, keepdims=True))
            alpha = jnp.exp2(m_old - m_new)
            p = jnp.exp2(s - m_new[0:1])
            l_new = alpha * l_old + jnp.sum(p, axis=0, keepdims=True)
            acc_scr[mi, :, lo:] = (jnp.broadcast_to(alpha[0:1], (ATTN_V_DIM, w)) * acc_scr[mi, :, lo:]
                                   + _dot(vb, p.astype(BF16)))
            if lo:
                m_new = jnp.concatenate([carry[2 * mi][:, :lo], m_new], axis=1)
                l_new = jnp.concatenate([carry[2 * mi + 1][:, :lo], l_new], axis=1)
            out += [m_new, l_new]
        return tuple(out)

    def pair(p, carry):
        qk(2 * p + 1, sb_scr)
        carry = process(2 * p, sa_scr, carry)
        qk(2 * p + 2, sa_scr)
        return process(2 * p + 1, sb_scr, carry)

    init = (jnp.full((SUB, tq), NEG, F32), jnp.zeros((SUB, tq), F32)) * 2
    qk(0, sa_scr)
    carry = lax.fori_loop(0, i, pair, init)
    qk(2 * i + 1, sb_scr, lo=tk)
    carry = process(2 * i, sa_scr, carry, diag=0)
    carry = process(2 * i + 1, sb_scr, carry, diag=1, lo=tk)

    lv = lam_ref[...]
    lam = (jnp.exp(jnp.sum(lv[0:1] * lv[1:2], keepdims=True))
           - jnp.exp(jnp.sum(lv[2:3] * lv[3:4], keepdims=True)) + lambda_init)
    o = acc_scr[0] / carry[1][0:1] - lam * (acc_scr[1] / carry[3][0:1])
    ms = jnp.mean(o * o, axis=0, keepdims=True)
    o = o * lax.rsqrt(ms + EPS) * sg_ref[...] * (1.0 - lambda_init)
    o_ref[...] = o.T.astype(BF16)


def _attention(qt, kk, vt, lambda_qk, subln_g, lambda_init, batch, seq, tq, tk):
    nq = seq // tq
    sg_b = jnp.broadcast_to(subln_g.astype(F32)[:, None], (ATTN_V_DIM, tq))
    return pl.pallas_call(
        functools.partial(_attn_body, tq=tq, tk=tk, lambda_init=lambda_init),
        grid=(batch, ATTN_HEADS, nq),
        in_specs=[pl.BlockSpec((4, ATTN_HEAD_DIM), lambda b, h, i: (0, 0)),
                  pl.BlockSpec((ATTN_V_DIM, tq), lambda b, h, i: (0, 0)),
                  pl.BlockSpec((1, 2, LANES, tq), lambda b, h, i: (b, h, 0, i)),
                  pl.BlockSpec((1, 1, seq, LANES), lambda b, h, i: (b, h, 0, 0)),
                  pl.BlockSpec((1, 1, seq // tk, LANES, tk), lambda b, h, i: (b, h, 0, 0, 0))],
        out_specs=pl.BlockSpec((tq, LANES), lambda b, h, i: (b * nq + i, h)),
        out_shape=jax.ShapeDtypeStruct((batch * seq, ATTN_HEADS * ATTN_V_DIM), BF16),
        scratch_shapes=[pltpu.VMEM((2, ATTN_V_DIM, tq), F32),
                        pltpu.VMEM((2, tk, tq), F32),
                        pltpu.VMEM((2, tk, tq), F32)],
        compiler_params=_params(("arbitrary", "arbitrary", "arbitrary")),
        name="attn",
    )(lambda_qk.astype(F32), sg_b, qt, kk, vt)


def _ssd_body(z0_ref, z1_ref, x0_ref, x1_ref, bc_ref, dt_ref, cw_ref, cb_ref, dtb_ref, alog_ref,
              dsk_ref, ng_ref, tri_ref, shift_ref, y_ref, ext_scr, st_scr):
    L = SSD_CHUNK
    halo = SSD_HALO
    c = pl.program_id(1)

    @pl.when(c == 0)
    def _():
        ext_scr[0:halo, :] = jnp.zeros((halo, SSD_CONV_DIM), BF16)
        st_scr[...] = jnp.zeros(st_scr.shape, F32)

    ext_scr[halo:halo + L, 0:1024] = x0_ref[...]
    ext_scr[halo:halo + L, 1024:2048] = x1_ref[...]
    ext_scr[halo:halo + L, 2048:3072] = bc_ref[...]
    shift = shift_ref[...]

    def conv_silu(lo, hi):
        sh = _dot(shift, ext_scr[:, lo:hi])
        acc = cb_ref[:, lo:hi] + cw_ref[SSD_CONV - 1:SSD_CONV, lo:hi] * ext_scr[halo:halo + L, lo:hi].astype(F32)
        for k in range(SSD_CONV - 1):
            acc = acc + cw_ref[k:k + 1, lo:hi] * sh[k * L:(k + 1) * L]
        return acc * _sigmoid(acc)

    xdt_in = dt_ref[...] + dtb_ref[...]
    dtp = jnp.maximum(xdt_in, 0.0) + jnp.log(1.0 + jnp.exp(-jnp.abs(xdt_in)))
    a = dtp * (-jnp.exp(alog_ref[...]) * LOG2E)
    tri = tri_ref[...]
    a_hi, a_mid, a_lo = _split3(a)
    acum = _dot(tri, a_hi) + _dot(tri, a_mid) + _dot(tri, a_lo)
    acum_t = acum.T
    lane = lax.broadcasted_iota(jnp.int32, (L, LANES), 1)
    first = lane < SSD_HEAD_DIM
    tril = lax.broadcasted_iota(jnp.int32, (L, L), 0) >= lax.broadcasted_iota(jnp.int32, (L, L), 1)
    gn = SSD_GROUPS * SSD_D_STATE
    gw = SSD_D_INNER // SSD_GROUPS

    def col(mat, h):
        return jnp.broadcast_to(mat[:, h:h + 1], (L, LANES))

    for g in range(SSD_GROUPS):
        bg = conv_silu(SSD_D_INNER + g * SSD_D_STATE, SSD_D_INNER + (g + 1) * SSD_D_STATE)
        cg = conv_silu(SSD_D_INNER + gn + g * SSD_D_STATE, SSD_D_INNER + gn + (g + 1) * SSD_D_STATE).astype(BF16)
        xg = conv_silu(g * gw, (g + 1) * gw)
        bgt = bg.T.astype(BF16)
        cb = _dot(cg, bgt)
        sprev = st_scr[g]
        yoff = _dot(cg, sprev.astype(BF16))
        y_parts, xdd_parts, dec_parts = [], [], []
        for p in range(4):
            h0 = 8 * g + 2 * p
            c0, c1 = col(acum, h0), col(acum, h0 + 1)
            ac_exp = jnp.where(first, c0, c1)
            dt_exp = jnp.where(first, col(dtp, h0), col(dtp, h0 + 1))
            xsl = xg[:, p * LANES:(p + 1) * LANES]
            xdt = xsl * dt_exp
            l0 = jnp.exp2(jnp.where(tril, c0 - acum_t[h0:h0 + 1, :], NEG))
            l1 = jnp.exp2(jnp.where(tril, c1 - acum_t[h0 + 1:h0 + 2, :], NEG))
            m2 = jnp.concatenate([(cb * l0).astype(BF16), (cb * l1).astype(BF16)], axis=1)
            xb = xdt.astype(BF16)
            zb = jnp.zeros_like(xb)
            rhs = jnp.concatenate([jnp.where(first, xb, zb), jnp.where(first, zb, xb)], axis=0)
            yd = _dot(m2, rhs)
            yo = yoff[:, p * LANES:(p + 1) * LANES] * jnp.exp2(ac_exp)
            dsk = dsk_ref[:, h0 * SSD_HEAD_DIM:h0 * SSD_HEAD_DIM + LANES]
            y_parts.append(yd + yo + xsl * dsk)
            last = ac_exp[L - 1:L, :]
            xdd_parts.append((xdt * jnp.exp2(last - ac_exp)).astype(BF16))
            dec_parts.append(jnp.exp2(last))
        xdd = jnp.concatenate(xdd_parts, axis=1)
        dec = jnp.concatenate(dec_parts, axis=1)
        st_scr[g] = sprev * dec + _dot(bgt, xdd)
        yg = jnp.concatenate(y_parts, axis=1)
        zref = z0_ref if g < 2 else z1_ref
        zg = zref[:, (g % 2) * gw:(g % 2 + 1) * gw].astype(F32)
        yg = yg * (zg * _sigmoid(zg))
        ms = jnp.mean(yg * yg, axis=-1, keepdims=True)
        yg = yg * lax.rsqrt(ms + EPS) * ng_ref[:, g * gw:(g + 1) * gw]
        y_ref[:, g * gw:(g + 1) * gw] = yg.astype(BF16)
    ext_scr[0:halo, :] = ext_scr[L:L + halo, :]


def _ssd(proj, dt, conv_w, conv_b, dt_bias, a_log, d_skip, norm_g, batch, seq):
    L = SSD_CHUNK
    nc = seq // L
    pad = LANES - SSD_HEADS
    dtb = jnp.pad(dt_bias.astype(F32), (0, pad)).reshape(1, LANES)
    alog = jnp.pad(a_log.astype(F32), (0, pad)).reshape(1, LANES)
    dsk = jnp.repeat(d_skip.astype(F32), SSD_HEAD_DIM).reshape(1, SSD_D_INNER)
    tri = jnp.tril(jnp.ones((L, L), F32)).astype(BF16)
    tt = jnp.arange(L)
    shift = jnp.concatenate(
        [jax.nn.one_hot(SSD_HALO + tt - (SSD_CONV - 1) + k, L + SSD_HALO, dtype=BF16) for k in range(SSD_CONV - 1)],
        axis=0)
    row = lambda b, c: (b * nc + c)
    const = lambda b, c: (0, 0)
    return pl.pallas_call(
        _ssd_body,
        grid=(batch, nc),
        in_specs=[pl.BlockSpec((L, 1024), lambda b, c: (row(b, c), 5)),
                  pl.BlockSpec((L, 1024), lambda b, c: (row(b, c), 6)),
                  pl.BlockSpec((L, 1024), lambda b, c: (row(b, c), 7)),
                  pl.BlockSpec((L, 1024), lambda b, c: (row(b, c), 8)),
                  pl.BlockSpec((L, 1024), lambda b, c: (row(b, c), 9)),
                  pl.BlockSpec((L, LANES), lambda b, c: (row(b, c), 0)),
                  pl.BlockSpec((SSD_CONV, SSD_CONV_DIM), const),
                  pl.BlockSpec((1, SSD_CONV_DIM), const),
                  pl.BlockSpec((1, LANES), const),
                  pl.BlockSpec((1, LANES), const),
                  pl.BlockSpec((1, SSD_D_INNER), const),
                  pl.BlockSpec((1, SSD_D_INNER), const),
                  pl.BlockSpec((L, L), const),
                  pl.BlockSpec(((SSD_CONV - 1) * L, L + SSD_HALO), const)],
        out_specs=pl.BlockSpec((L, SSD_D_INNER), lambda b, c: (row(b, c), 0)),
        out_shape=jax.ShapeDtypeStruct((batch * seq, SSD_D_INNER), BF16),
        scratch_shapes=[pltpu.VMEM((L + SSD_HALO, SSD_CONV_DIM), BF16),
                        pltpu.VMEM((SSD_GROUPS, SSD_D_STATE, SSD_D_INNER // SSD_GROUPS), F32)],
        compiler_params=_params(("arbitrary", "arbitrary")),
        name="ssd",
    )(proj, proj, proj, proj, proj, dt, conv_w.astype(F32), conv_b.astype(F32).reshape(1, -1), dtb, alog,
      dsk, norm_g.astype(F32).reshape(1, -1), tri, shift)


def _outproj_body(ya_ref, yb_ref, ga_ref, gb_ref, x_ref, g1_ref, sc_ref, sh_ref, n2_ref,
                  wa_ref, wb_ref, wo_ref, wr_ref, x1_ref, h2_ref, lg_ref):
    pa = _dot(ya_ref[...], wa_ref[...])
    pb = _dot(yb_ref[...], wb_ref[...])
    merged = _sigmoid(ga_ref[...].astype(F32)) * pa + _sigmoid(gb_ref[...].astype(F32)) * pb
    x1 = x_ref[...] + g1_ref[0] * _dot(merged.astype(BF16), wo_ref[...])
    x1_ref[...] = x1
    ms = jnp.mean(x1 * x1, axis=-1, keepdims=True)
    h2 = x1 * lax.rsqrt(ms + EPS) * n2_ref[...]
    h2 = h2 * (1.0 + sc_ref[0]) + sh_ref[0]
    tm = h2.shape[0]
    for s in range(SUB):
        h2_ref[pl.ds(s, tm, stride=SUB), :] = h2[:, s * LANES:(s + 1) * LANES]
    lg_ref[...] = _dot(h2.astype(BF16), wr_ref[...])


def _outproj(ya, yb, proj, x2d, g1, sc2, sh2, norm2_g, wa, wb, wo, wr, seq):
    t, d = x2d.shape
    tm = min(512, seq)
    per_b = seq // tm
    const = lambda i: (0, 0)
    mod = lambda i: (i // per_b, 0, 0)
    return pl.pallas_call(
        _outproj_body,
        grid=(t // tm,),
        in_specs=[pl.BlockSpec((tm, d), lambda i: (i, 0)),
                  pl.BlockSpec((tm, SSD_D_INNER), lambda i: (i, 0)),
                  pl.BlockSpec((tm, d), lambda i: (i, 3)),
                  pl.BlockSpec((tm, d), lambda i: (i, 4)),
                  pl.BlockSpec((tm, d), lambda i: (i, 0)),
                  pl.BlockSpec((1, 1, d), mod),
                  pl.BlockSpec((1, 1, d), mod),
                  pl.BlockSpec((1, 1, d), mod),
                  pl.BlockSpec((1, d), const),
                  pl.BlockSpec((d, d), const),
                  pl.BlockSpec((SSD_D_INNER, d), const),
                  pl.BlockSpec((d, d), const),
                  pl.BlockSpec((d, LANES), const)],
        out_specs=[pl.BlockSpec((tm, d), lambda i: (i, 0)),
                   pl.BlockSpec((tm * SUB, LANES), lambda i: (i, 0)),
                   pl.BlockSpec((tm, LANES), lambda i: (i, 0))],
        out_shape=[jax.ShapeDtypeStruct((t, d), F32),
                   jax.ShapeDtypeStruct((t * SUB, LANES), F32),
                   jax.ShapeDtypeStruct((t, LANES), F32)],
        compiler_params=_params(("arbitrary",)),
        name="outproj",
    )(ya, yb, proj, proj, x2d, g1, sc2, sh2, norm2_g.astype(F32).reshape(1, d), wa, wb, wo, wr)


def _router_body(lg_ref, bias_ref, idx_ref, w_ref):
    tr = lg_ref.shape[0]
    per = N_EXPERTS // N_EXPERT_GROUPS
    lt = lg_ref[...].T
    scores = _sigmoid(lt[:N_EXPERTS])
    choice = scores + bias_ref[...]
    iota = lax.broadcasted_iota(jnp.int32, (per, tr), 0)
    grp = [choice[g * per:(g + 1) * per] for g in range(N_EXPERT_GROUPS)]
    sc_g = [scores[g * per:(g + 1) * per] for g in range(N_EXPERT_GROUPS)]

    gsc = jnp.zeros((N_EXPERT_GROUPS, tr), F32)
    for g in range(N_EXPERT_GROUPS):
        top1 = jnp.max(grp[g], axis=0, keepdims=True)
        idx1 = jnp.min(jnp.where(grp[g] == top1, iota, per), axis=0, keepdims=True)
        top2 = jnp.max(jnp.where(iota == idx1, NEG, grp[g]), axis=0, keepdims=True)
        gsc = jnp.where(iota == g, top1 + top2, gsc)

    gsel = jnp.zeros((N_EXPERT_GROUPS, tr), F32)
    cur = gsc
    for _ in range(TOPK_GROUPS):
        mx = jnp.max(cur, axis=0, keepdims=True)
        idx = jnp.min(jnp.where(cur == mx, iota, N_EXPERT_GROUPS), axis=0, keepdims=True)
        hit = iota == idx
        gsel = jnp.where(hit, 1.0, gsel)
        cur = jnp.where(hit, NEG, cur)

    cur_g = [jnp.where(gsel[g:g + 1] > 0.0, grp[g], NEG) for g in range(N_EXPERT_GROUPS)]
    idx_rows = jnp.zeros((TOP_K, tr), jnp.int32)
    w_rows = jnp.zeros((TOP_K, tr), F32)
    for kk in range(TOP_K):
        mx = cur_g[0]
        for g in range(1, N_EXPERT_GROUPS):
            mx = jnp.maximum(mx, cur_g[g])
        mx = jnp.max(mx, axis=0, keepdims=True)
        idx = jnp.where(cur_g[0] == mx, iota, N_EXPERTS)
        for g in range(1, N_EXPERT_GROUPS):
            idx = jnp.minimum(idx, jnp.where(cur_g[g] == mx, iota + g * per, N_EXPERTS))
        idx = jnp.min(idx, axis=0, keepdims=True)
        wk = jnp.zeros((per, tr), F32)
        for g in range(N_EXPERT_GROUPS):
            hit = (iota + g * per) == idx
            wk = wk + jnp.where(hit, sc_g[g], 0.0)
            cur_g[g] = jnp.where(hit, NEG, cur_g[g])
        idx_rows = jnp.where(iota == kk, idx, idx_rows)
        w_rows = jnp.where(iota == kk, jnp.sum(wk, axis=0, keepdims=True), w_rows)

    wsum = jnp.sum(w_rows, axis=0, keepdims=True)
    idx_ref[...] = idx_rows
    w_ref[...] = w_rows / wsum * ROUTED_SCALE


def _router(logits, router_bias, seq):
    t = logits.shape[0]
    tr = min(1024, seq)
    bias_b = jnp.broadcast_to(router_bias.astype(F32)[:, None], (N_EXPERTS, tr))
    return pl.pallas_call(
        _router_body,
        grid=(t // tr,),
        in_specs=[pl.BlockSpec((tr, LANES), lambda i: (i, 0)),
                  pl.BlockSpec((N_EXPERTS, tr), lambda i: (0, 0))],
        out_specs=[pl.BlockSpec((TOP_K, tr), lambda i: (0, i)),
                   pl.BlockSpec((TOP_K, tr), lambda i: (0, i))],
        out_shape=[jax.ShapeDtypeStruct((TOP_K, t), jnp.int32),
                   jax.ShapeDtypeStruct((TOP_K, t), F32)],
        compiler_params=_params(("arbitrary",)),
        name="router",
    )(logits, bias_b)


def _route_tables(idx_t, w_t, groups, tg, rows, ntiles):
    i32 = jnp.int32
    e = idx_t.reshape(TOP_K, groups, tg).transpose(1, 0, 2).reshape(groups, TOP_K * tg)
    w = w_t.reshape(TOP_K, groups, tg).transpose(1, 0, 2).reshape(groups, TOP_K * tg)
    tl = jnp.broadcast_to(jnp.arange(tg, dtype=i32)[None, None, :], (groups, TOP_K, tg)).reshape(groups, TOP_K * tg)
    skey, sw = lax.sort((e * tg + tl, w), dimension=1, num_keys=1)
    stok = skey % tg
    cnt = jnp.sum((e[:, :, None] == jnp.arange(N_EXPERTS, dtype=i32)[None, None, :]).astype(i32), axis=1)
    off = jnp.cumsum(cnt, axis=1) - cnt
    ntile = (cnt + rows - 1) // rows
    tend = jnp.cumsum(ntile, axis=1)
    tstart = tend - ntile
    n = jnp.arange(ntiles, dtype=i32)
    te = jnp.sum((n[None, :, None] >= tend[:, None, :]).astype(i32), axis=2)
    valid_tile = n[None, :] < tend[:, -1:]
    te = jnp.minimum(te, N_EXPERTS - 1)
    k0 = (n[None, :] - jnp.take_along_axis(tstart, te, axis=1)) * rows
    nvalid = jnp.where(valid_tile, jnp.clip(jnp.take_along_axis(cnt, te, axis=1) - k0, 0, rows), 0)
    start = jnp.clip(jnp.take_along_axis(off, te, axis=1) + k0, 0, TOP_K * tg - 1)
    tok_g, tok_s, wv = _tile_windows(start, nvalid, stok, sw, groups, tg, rows, ntiles)
    return te, nvalid, tok_g, tok_s, wv


def _windows_body(start_ref, nv_ref, stok_ref, sw_ref, tokg_ref, toks_ref, wt_ref, *, rows, tg):
    g = pl.program_id(0)
    nt = tokg_ref.shape[1]
    nl = -(-rows // LANES)
    lane = lax.broadcasted_iota(jnp.int32, (1, LANES), 1)
    rr = lax.broadcasted_iota(jnp.int32, (1, nl * LANES), 1)

    def window(src, q, sh):
        blk = src[0, pl.ds(q, SUB), :]
        rolled = pltpu.roll(blk, (LANES - sh) % LANES, 1)
        parts = [jnp.where(lane < LANES - sh, rolled[k:k + 1], rolled[k + 1:k + 2]) for k in range(nl)]
        return jnp.concatenate(parts, axis=1)

    def body(n, carry):
        s = start_ref[g, n]
        q = s // LANES
        sh = s % LANES
        valid = rr < nv_ref[g, n]
        tok8 = window(stok_ref, q, sh) * SUB
        wv = window(sw_ref, q, sh)
        tokg_ref[0, n] = jnp.where(valid, tok8, 0)[:, :rows]
        toks_ref[0, n] = jnp.where(valid, tok8, tg * SUB)[:, :rows]
        wt_ref[0, n] = jnp.where(valid, wv, 0.0)[:, :rows]
        return carry

    lax.fori_loop(0, nt, body, 0, unroll=4)


def _tile_windows(start, nvalid, stok, sw, groups, tg, rows, ntiles):
    n_list = TOP_K * tg
    n_rows = n_list // LANES + SUB
    pad = n_rows * LANES - n_list
    stok2 = jnp.pad(stok, ((0, 0), (0, pad))).reshape(groups, n_rows, LANES)
    sw2 = jnp.pad(sw, ((0, 0), (0, pad))).reshape(groups, n_rows, LANES)
    lst = pl.BlockSpec((1, n_rows, LANES), lambda g, st, nv: (g, 0, 0))
    out = pl.BlockSpec((1, ntiles, 1, rows), lambda g, st, nv: (g, 0, 0, 0))
    grid_spec = pltpu.PrefetchScalarGridSpec(
        num_scalar_prefetch=2, grid=(groups,), in_specs=[lst, lst], out_specs=[out, out, out])
    shape = (groups, ntiles, 1, rows)
    return pl.pallas_call(
        functools.partial(_windows_body, rows=rows, tg=tg),
        grid_spec=grid_spec,
        out_shape=[jax.ShapeDtypeStruct(shape, jnp.int32), jax.ShapeDtypeStruct(shape, jnp.int32),
                   jax.ShapeDtypeStruct(shape, F32)],
        compiler_params=_params(("arbitrary",)),
        name="windows",
    )(start, nvalid, stok2, sw2)


def _routed_body(te_ref, nv_ref, tokg_ref, toks_ref, wt_ref, h_ref, wg_ref, wu_ref, wd_ref, acc_ref,
                 xbuf, ybuf):
    g = pl.program_id(0)
    n = pl.program_id(1)
    rows = MOE_ROWS
    slot = n % 2
    has_cur = nv_ref[g, n + 1] > 0
    has_prev = nv_ref[g, n] > 0

    @pl.when(n == 0)
    def _():
        acc_ref[...] = jnp.zeros(acc_ref.shape, F32)

    def compute():
        for r in range(rows):
            t8 = pl.multiple_of(tokg_ref[0, 0, 0, r], SUB)
            xbuf[pl.ds(r * SUB, SUB), :] = h_ref[pl.ds(t8, SUB), :]
        x2d = jnp.concatenate([xbuf[pl.ds(s, rows, stride=SUB), :] for s in range(SUB)], axis=1).astype(BF16)
        a = _dot(x2d, wg_ref[0].astype(BF16))
        b = _dot(x2d, wu_ref[0].astype(BF16))
        act = (a * _sigmoid(a) * b).astype(BF16)
        y = _dot(act, wd_ref[0].astype(BF16))
        for s in range(SUB):
            ybuf[slot, pl.ds(s, rows, stride=SUB), :] = y[:, s * LANES:(s + 1) * LANES]

    def scatter_prev():
        for c in range(rows // SUB):
            upd = []
            for r in range(c * SUB, (c + 1) * SUB):
                t8 = pl.multiple_of(toks_ref[0, 0, 0, r], SUB)
                upd.append((t8, acc_ref[0, pl.ds(t8, SUB), :]
                            + wt_ref[0, 0, 0, r] * ybuf[1 - slot, pl.ds(r * SUB, SUB), :]))
            for t8, v in upd:
                acc_ref[0, pl.ds(t8, SUB), :] = v

    @pl.when(jnp.logical_and(has_cur, has_prev))
    def _():
        scatter_prev()
        compute()

    @pl.when(jnp.logical_and(has_cur, jnp.logical_not(has_prev)))
    def _():
        compute()

    @pl.when(jnp.logical_and(jnp.logical_not(has_cur), has_prev))
    def _():
        scatter_prev()


def _routed(hv, tables, w_eg, w_eu, w_ed, groups, tg, ntiles):
    te, nvalid, tok_g, tok_s, wv = tables
    d, f = D_MODEL, EXPERT_DIM
    rows = MOE_ROWS
    i32 = jnp.int32
    te = jnp.concatenate([te, te[:, -1:]], axis=1)
    zero = jnp.zeros((groups, 1), i32)
    nvp = jnp.concatenate([zero, nvalid, zero], axis=1)
    cur = lambda g, n, te, nv: (g, jnp.minimum(n, ntiles - 1), 0, 0)
    prev = lambda g, n, te, nv: (g, jnp.maximum(n - 1, 0), 0, 0)
    smem = functools.partial(pl.BlockSpec, (1, 1, 1, rows), memory_space=pltpu.SMEM)
    grid_spec = pltpu.PrefetchScalarGridSpec(
        num_scalar_prefetch=2,
        grid=(groups, ntiles + 1),
        in_specs=[smem(index_map=cur), smem(index_map=prev), smem(index_map=prev),
                  pl.BlockSpec((tg * SUB, LANES), lambda g, n, te, nv: (g, 0), pipeline_mode=pl.Buffered(1)),
                  pl.BlockSpec((1, d, f), lambda g, n, te, nv: (te[g, n], 0, 0)),
                  pl.BlockSpec((1, d, f), lambda g, n, te, nv: (te[g, n], 0, 0)),
                  pl.BlockSpec((1, f, d), lambda g, n, te, nv: (te[g, n], 0, 0))],
        out_specs=pl.BlockSpec((1, (tg + 1) * SUB, LANES), lambda g, n, te, nv: (g, 0, 0),
                               pipeline_mode=pl.Buffered(1)),
        scratch_shapes=[pltpu.VMEM((rows * SUB, LANES), F32), pltpu.VMEM((2, rows * SUB, LANES), F32)],
    )
    return pl.pallas_call(
        _routed_body,
        grid_spec=grid_spec,
        out_shape=jax.ShapeDtypeStruct((groups, (tg + 1) * SUB, LANES), F32),
        compiler_params=_params(("arbitrary", "arbitrary")),
        name="routed",
    )(te, nvp, tok_g, tok_s, wv, hv, w_eg, w_eu, w_ed)


def _combine_body(h_ref, r_ref, x1_ref, g2_ref, sg_ref, su_ref, sd_ref, o_ref):
    tm = x1_ref.shape[0]
    h = jnp.concatenate([h_ref[pl.ds(s, tm, stride=SUB), :] for s in range(SUB)], axis=1)
    routed = jnp.concatenate([r_ref[0, pl.ds(s, tm, stride=SUB), :] for s in range(SUB)], axis=1)
    hb = h.astype(BF16)
    a = _dot(hb, sg_ref[...])
    b = _dot(hb, su_ref[...])
    shared = _dot((a * _sigmoid(a) * b).astype(BF16), sd_ref[...])
    o_ref[...] = x1_ref[...] + g2_ref[0] * (routed + shared)


def _combine(hv, routed, x1, g2, w_sg, w_su, w_sd, seq, tg):
    t, d = x1.shape
    tm = min(1024, seq, tg)
    per_b = seq // tm
    per_g = tg // tm
    f = w_sg.shape[1]
    const = lambda i: (0, 0)
    row = lambda i: (i, 0)
    return pl.pallas_call(
        _combine_body,
        grid=(t // tm,),
        in_specs=[pl.BlockSpec((tm * SUB, LANES), row),
                  pl.BlockSpec((1, tm * SUB, LANES), lambda i: (i // per_g, i % per_g, 0)),
                  pl.BlockSpec((tm, d), row),
                  pl.BlockSpec((1, 1, d), lambda i: (i // per_b, 0, 0)),
                  pl.BlockSpec((d, f), const),
                  pl.BlockSpec((d, f), const),
                  pl.BlockSpec((f, d), const)],
        out_specs=pl.BlockSpec((tm, d), row),
        out_shape=jax.ShapeDtypeStruct((t, d), F32),
        compiler_params=_params(("arbitrary",)),
        name="combine",
    )(hv, routed, x1, g2, w_sg, w_su, w_sd)


def kernel(x, c, positions, w_ada, b_ada, norm1_g, w_in, q_norm_g, k_norm_g, lambda_qk, subln_g, conv_w, conv_b, dt_bias, a_log, d_skip, ssd_norm_g, w_branch_a, w_branch_b, w_out, norm2_g, w_router, router_bias, w_exp_gate, w_exp_up, w_exp_down, w_sh_gate, w_sh_up, w_sh_down):
    batch, seq, d = x.shape
    t = batch * seq
    tq = min(1024, seq)
    tk = tq // 2
    xf = x.reshape(t, d)
    for layer in range(w_ada.shape[0]):
        lambda_init = 0.8 - 0.6 * math.exp(-0.3 * layer)
        c_pad = jnp.pad(c.astype(F32), ((0, 8 - batch), (0, 0)))
        mod = _ada(c_pad, w_ada[layer], b_ada[layer])[:batch].reshape(batch, 6, 1, d)
        sh1, sc1, g1, sh2, sc2, g2 = (mod[:, n] for n in range(6))

        w_l = w_in[layer]
        wdt = jnp.pad(w_l[:, MAIN_COLS:], ((0, 0), (0, LANES - SSD_HEADS)))
        wdt_hi, wdt_lo = _split2(wdt)
        proj, dt = _inproj(xf, norm1_g[layer], sc1, sh1, w_l[:, :MAIN_COLS].astype(BF16), wdt_hi, wdt_lo, seq)

        qt, kk, vt = _qkvprep(proj, positions, q_norm_g[layer], k_norm_g[layer], batch, seq, tk)
        ya = _attention(qt, kk, vt, lambda_qk[layer], subln_g[layer], lambda_init, batch, seq, tq, tk)
        yb = _ssd(proj, dt, conv_w[layer], conv_b[layer], dt_bias[layer], a_log[layer], d_skip[layer],
                  ssd_norm_g[layer], batch, seq)

        wr = jnp.pad(w_router[layer], ((0, 0), (0, LANES - N_EXPERTS))).astype(BF16)
        x1, h2, logits = _outproj(ya, yb, proj, xf, g1, sc2, sh2, norm2_g[layer],
                                  w_branch_a[layer].astype(BF16), w_branch_b[layer].astype(BF16),
                                  w_out[layer].astype(BF16), wr, seq)
        idx_t, w_t = _router(logits, router_bias[layer], seq)
        tg = min(4096, t)
        groups = t // tg
        ntiles = -(-tg * TOP_K // MOE_ROWS) + N_EXPERTS
        tables = _route_tables(idx_t, w_t, groups, tg, MOE_ROWS, ntiles)
        routed = _routed(h2, tables, w_exp_gate[layer], w_exp_up[layer], w_exp_down[layer], groups, tg, ntiles)
        xf = _combine(h2, routed, x1, g2, w_sh_gate[layer].astype(BF16), w_sh_up[layer].astype(BF16),
                      w_sh_down[layer].astype(BF16), seq, tg)
    return xf.reshape(batch, seq, d)
```
